```python
import math
import jax, jax.numpy as jnp
from jax import lax
import numpy as np


D_MODEL = 4096
BATCH = 2
SEQ = 4096
DEPTH = 2
DEC_BATCH = 1
DEC_SEQ = 8192
PAST_LEN = 128

GRID_W = 64
ROW_WIN = 8
COL_WIN = 16
MIX_WIDTH = D_MODEL
MEM_HEADS = 4
N_MEM = 256
MEM_WIDTH = MIX_WIDTH // 4
MEM_HEAD_DIM = MEM_WIDTH // MEM_HEADS
NA_HEAD_DIM = 128
NA_WIDTH = MIX_WIDTH - MEM_WIDTH
NA_HEADS = NA_WIDTH // NA_HEAD_DIM
S5_WIDTH = MIX_WIDTH - MEM_WIDTH
S5_GROUP = 16
S5_GROUPS = S5_WIDTH // S5_GROUP
S5_STATE = 64
S5_GROUP_CHUNK = math.gcd(S5_GROUPS, 32)
DT_MIN = 1e-3
DT_MAX = 1e-1
N_EXPERT_GROUPS = 4
EXPERTS_PER_GROUP = 4
N_EXPERTS = N_EXPERT_GROUPS * EXPERTS_PER_GROUP
TOP_K_IN_GROUP = 2
D_EXPERT = 1024
N_NA_LAYERS = (DEPTH + 1) // 2
N_S5_LAYERS = DEPTH // 2
RMS_EPS = 1e-6
NEG_INF = -1e30

kernel_name = 'hybrid_natten_s5_hmoe_encoder'


def rms_norm(x, g):
    x32 = x.astype(jnp.float32)
    y = x32 * lax.rsqrt(jnp.mean(x32 * x32, axis=-1, keepdims=True) + RMS_EPS)
    return (y * g.astype(jnp.float32)).astype(x.dtype)


def neighbourhood_attention(q, k, v, rpb):
    bsz, seq, heads, hd = q.shape
    rows = seq // GRID_W
    kr = min(ROW_WIN, rows)
    qg = q.reshape(bsz, rows, GRID_W, heads, hd)
    kg = k.reshape(bsz, rows, GRID_W, heads, hd)
    vg = v.reshape(bsz, rows, GRID_W, heads, hd)
    cols = jnp.arange(GRID_W)
    c_start = jnp.clip(cols - COL_WIN // 2, 0, GRID_W - COL_WIN)
    in_win = (cols[None, :] >= c_start[:, None]) & (cols[None, :] < c_start[:, None] + COL_WIN)
    dc = jnp.clip(cols[None, :] - cols[:, None], -(COL_WIN - 1), COL_WIN - 1) + COL_WIN - 1
    col_bias = jnp.where(in_win, rpb.astype(jnp.float32)[:, :, dc], NEG_INF)
    scale = hd ** -0.5

    def one_row(r):
        r0 = jnp.clip(r - kr // 2, 0, rows - kr)
        q_r = lax.dynamic_index_in_dim(qg, r, axis=1, keepdims=False)
        k_r = lax.dynamic_slice_in_dim(kg, r0, kr, axis=1)
        v_r = lax.dynamic_slice_in_dim(vg, r0, kr, axis=1)
        dr = r0 + jnp.arange(kr) - r + ROW_WIN - 1
        bias = jnp.take(col_bias, dr, axis=1).transpose(0, 2, 1, 3)
        s = jnp.einsum('bqhd,bkwhd->bhqkw', q_r, k_r).astype(jnp.float32) * scale + bias[None]
        p = jax.nn.softmax(s.reshape(bsz, heads, GRID_W, kr * GRID_W), axis=-1)
        p = p.reshape(bsz, heads, GRID_W, kr, GRID_W).astype(v.dtype)
        return jnp.einsum('bhqkw,bkwhd->bqhd', p, v_r)

    out = lax.map(one_row, jnp.arange(rows))
    return out.transpose(1, 0, 2, 3, 4).reshape(bsz, seq, heads * hd)


def s5_scan(u, a_re, a_im, log_dt, b_re, b_im, c_re, c_im, reverse):
    dt = jnp.exp(log_dt)[:, None]
    mag = jnp.exp(a_re * dt)
    ang = a_im * dt
    lam_re = mag * jnp.cos(ang)
    lam_im = mag * jnp.sin(ang)
    den = a_re * a_re + a_im * a_im
    n_re = lam_re - 1.0
    coef_re = ((n_re * a_re + lam_im * a_im) / den)[..., None]
    coef_im = ((lam_im * a_re - n_re * a_im) / den)[..., None]
    bb_re = coef_re * b_re - coef_im * b_im
    bb_im = coef_re * b_im + coef_im * b_re
    bu_re = jnp.einsum('blgc,gpc->blgp', u, bb_re)
    bu_im = jnp.einsum('blgc,gpc->blgp', u, bb_im)
    lr = jnp.broadcast_to(lam_re, bu_re.shape)
    li = jnp.broadcast_to(lam_im, bu_re.shape)

    def combine(e1, e2):
        a1r, a1i, b1r, b1i = e1
        a2r, a2i, b2r, b2i = e2
        return (a2r * a1r - a2i * a1i,
                a2r * a1i + a2i * a1r,
                a2r * b1r - a2i * b1i + b2r,
                a2r * b1i + a2i * b1r + b2i)

    _, _, xr, xi = lax.associative_scan(combine, (lr, li, bu_re, bu_im), reverse=reverse, axis=1)
    return jnp.einsum('blgp,gcp->blgc', xr, c_re) - jnp.einsum('blgp,gcp->blgc', xi, c_im)


def s5_bidirectional(u, a_re, a_im, log_dt, b_re, b_im, c_re, c_im, d):
    bsz, seq, groups, ch = u.shape
    nc = groups // S5_GROUP_CHUNK
    u_c = u.astype(jnp.float32).reshape(bsz, seq, nc, S5_GROUP_CHUNK, ch).transpose(2, 0, 1, 3, 4)

    def chunk_params(p):
        p = p.astype(jnp.float32)
        return jnp.moveaxis(p.reshape((2, nc, S5_GROUP_CHUNK) + p.shape[2:]), 1, 0)

    d_c = d.astype(jnp.float32).reshape(nc, S5_GROUP_CHUNK, ch)

    def one_chunk(args):
        u_k, ar, ai, ldt, br, bi, cr, ci, dk = args
        y = dk * u_k
        for direction, rev in ((0, False), (1, True)):
            y = y + s5_scan(u_k, ar[direction], ai[direction], ldt[direction], br[direction],
                            bi[direction], cr[direction], ci[direction], rev)
        return y

    y = lax.map(one_chunk, (u_c, chunk_params(a_re), chunk_params(a_im), chunk_params(log_dt),
                            chunk_params(b_re), chunk_params(b_im), chunk_params(c_re),
                            chunk_params(c_im), d_c))
    return y.transpose(1, 2, 0, 3, 4).reshape(bsz, seq, groups * ch)


def memory_attention(qm, mem_n, w_kv):
    bsz, seq, _ = qm.shape
    km, vm = jnp.split(mem_n @ w_kv, 2, axis=-1)
    q = qm.reshape(bsz, seq, MEM_HEADS, MEM_HEAD_DIM)
    km = km.reshape(bsz, -1, MEM_HEADS, MEM_HEAD_DIM)
    vm = vm.reshape(bsz, -1, MEM_HEADS, MEM_HEAD_DIM)
    s = jnp.einsum('blhd,bmhd->bhlm', q, km).astype(jnp.float32) * (MEM_HEAD_DIM ** -0.5)
    p = jax.nn.softmax(s, axis=-1).astype(vm.dtype)
    return jnp.einsum('bhlm,bmhd->blhd', p, vm).reshape(bsz, seq, MEM_WIDTH)


def hier_moe(h, w_rg, b_rg, w_re, b_re, w_gate, w_up, w_down):
    bsz, seq, dm = h.shape
    t = h.reshape(-1, dm)
    n_tok = t.shape[0]
    p_group = jax.nn.softmax((t @ w_rg).astype(jnp.float32) + b_rg.astype(jnp.float32), axis=-1)
    g_val, g_idx = lax.top_k(p_group, 1)
    e_logits = ((t @ w_re).astype(jnp.float32) + b_re.astype(jnp.float32)).reshape(
        n_tok, N_EXPERT_GROUPS, EXPERTS_PER_GROUP)
    e_sel = jnp.take_along_axis(e_logits, g_idx[:, :, None], axis=1)[:, 0]
    e_val, e_idx = lax.top_k(e_sel, TOP_K_IN_GROUP)
    e_w = jax.nn.softmax(e_val, axis=-1) * g_val
    expert_id = g_idx * EXPERTS_PER_GROUP + e_idx
    combine = jnp.sum(jax.nn.one_hot(expert_id, N_EXPERTS, dtype=jnp.float32) * e_w[..., None], axis=1)
    y = jnp.zeros((n_tok, dm), jnp.float32)
    for e in range(N_EXPERTS):
        hid = jax.nn.silu(t @ w_gate[e]) * (t @ w_up[e])
        y = y + combine[:, e:e + 1] * (hid @ w_down[e]).astype(jnp.float32)
    return y.reshape(bsz, seq, dm).astype(h.dtype)


def encoder_trunk(x, mem, p):
    bsz, seq, _ = x.shape
    for i in range(DEPTH):
        j = i // 2
        h = rms_norm(x, p['norm_mix'][i])
        mem_n = rms_norm(mem, p['norm_mem'][i])
        if i % 2 == 0:
            proj = h @ p['na_w_in'][j]
            q, k, v, qm = jnp.split(proj, [NA_WIDTH, 2 * NA_WIDTH, 3 * NA_WIDTH], axis=-1)
            q = q.reshape(bsz, seq, NA_HEADS, NA_HEAD_DIM)
            k = k.reshape(bsz, seq, NA_HEADS, NA_HEAD_DIM)
            v = v.reshape(bsz, seq, NA_HEADS, NA_HEAD_DIM)
            tok = neighbourhood_attention(q, k, v, p['na_rpb'][j])
        else:
            proj = h @ p['s5_w_in'][j]
            u, qm = jnp.split(proj, [S5_WIDTH], axis=-1)
            y = s5_bidirectional(u.reshape(bsz, seq, S5_GROUPS, S5_GROUP), p['s5_a_re'][j], p['s5_a_im'][j],
                                 p['s5_log_dt'][j], p['s5_b_re'][j], p['s5_b_im'][j], p['s5_c_re'][j],
                                 p['s5_c_im'][j], p['s5_d'][j])
            g = jax.nn.gelu(y).astype(x.dtype)
            tok = g * jax.nn.sigmoid(g @ p['s5_w_glu'][j] + p['s5_b_glu'][j])
        mo = memory_attention(qm, mem_n, p['mem_w_kv'][i])
        x = x + (jnp.concatenate([tok.astype(x.dtype), mo.astype(x.dtype)], axis=-1) @ p['w_out'][i]).astype(x.dtype)
        x = x + hier_moe(rms_norm(x, p['norm_ffn'][i]), p['moe_w_rg'][i], p['moe_b_rg'][i], p['moe_w_re'][i],
                         p['moe_b_re'][i], p['moe_w_gate'][i], p['moe_w_up'][i], p['moe_w_down'][i])
    return rms_norm(x, p['norm_final'])


def setup_inputs(seed: int = 0) -> dict:
    key = jax.random.key(seed)
    ks = iter(jax.random.split(key, 40))
    f32 = jnp.float32

    def normal(shape, scale):
        return scale * jax.random.normal(next(ks), shape, f32)

    def gain(shape):
        return 1.0 + normal(shape, 0.02)

    G, P, C = S5_GROUPS, S5_STATE, S5_GROUP
    s5p = (N_S5_LAYERS, 2, G, P)
    return {
        'x_prompt': normal((BATCH, SEQ, D_MODEL), 1.0),
        'x_sample': normal((DEC_BATCH, DEC_SEQ, D_MODEL), 1.0),
        'mem_prompt': normal((BATCH, N_MEM, D_MODEL), 1.0),
        'mem_sample': normal((DEC_BATCH, N_MEM, D_MODEL), 1.0),
        'norm_mix': gain((DEPTH, D_MODEL)),
        'norm_mem': gain((DEPTH, D_MODEL)),
        'norm_ffn': gain((DEPTH, D_MODEL)),
        'norm_final': gain((D_MODEL,)),
        'na_w_in': normal((N_NA_LAYERS, D_MODEL, 3 * NA_WIDTH + MEM_WIDTH), D_MODEL ** -0.5),
        'na_rpb': normal((N_NA_LAYERS, NA_HEADS, 2 * ROW_WIN - 1, 2 * COL_WIN - 1), 0.02),
        's5_w_in': normal((N_S5_LAYERS, D_MODEL, S5_WIDTH + MEM_WIDTH), D_MODEL ** -0.5),
        's5_a_re': -0.5 + normal(s5p, 0.01),
        's5_a_im': math.pi * jnp.arange(P, dtype=f32) + normal(s5p, 0.01),
        's5_log_dt': jax.random.uniform(next(ks), (N_S5_LAYERS, 2, G), f32, math.log(DT_MIN), math.log(DT_MAX)),
        's5_b_re': normal((N_S5_LAYERS, 2, G, P, C), (2 * C) ** -0.5),
        's5_b_im': normal((N_S5_LAYERS, 2, G, P, C), (2 * C) ** -0.5),
        's5_c_re': normal((N_S5_LAYERS, 2, G, C, P), (2 * P) ** -0.5),
        's5_c_im': normal((N_S5_LAYERS, 2, G, C, P), (2 * P) ** -0.5),
        's5_d': normal((N_S5_LAYERS, G, C), 0.5),
        's5_w_glu': normal((N_S5_LAYERS, S5_WIDTH, S5_WIDTH), S5_WIDTH ** -0.5),
        's5_b_glu': normal((N_S5_LAYERS, S5_WIDTH), 0.02),
        'mem_w_kv': normal((DEPTH, D_MODEL, 2 * MEM_WIDTH), D_MODEL ** -0.5),
        'w_out': normal((DEPTH, MIX_WIDTH, D_MODEL), MIX_WIDTH ** -0.5),
        'moe_w_rg': normal((DEPTH, D_MODEL, N_EXPERT_GROUPS), D_MODEL ** -0.5),
        'moe_b_rg': normal((DEPTH, N_EXPERT_GROUPS), 0.01),
        'moe_w_re': normal((DEPTH, D_MODEL, N_EXPERTS), D_MODEL ** -0.5),
        'moe_b_re': normal((DEPTH, N_EXPERTS), 0.01),
        'moe_w_gate': normal((DEPTH, N_EXPERTS, D_MODEL, D_EXPERT), D_MODEL ** -0.5),
        'moe_w_up': normal((DEPTH, N_EXPERTS, D_MODEL, D_EXPERT), D_MODEL ** -0.5),
        'moe_w_down': normal((DEPTH, N_EXPERTS, D_EXPERT, D_MODEL), D_EXPERT ** -0.5),
    }


def reference(x_prompt, x_sample, mem_prompt, mem_sample, norm_mix, norm_mem, norm_ffn, norm_final,
              na_w_in, na_rpb, s5_w_in, s5_a_re, s5_a_im, s5_log_dt, s5_b_re, s5_b_im, s5_c_re, s5_c_im,
              s5_d, s5_w_glu, s5_b_glu, mem_w_kv, w_out, moe_w_rg, moe_b_rg, moe_w_re, moe_b_re,
              moe_w_gate, moe_w_up, moe_w_down):
    p = dict(norm_mix=norm_mix, norm_mem=norm_mem, norm_ffn=norm_ffn, norm_final=norm_final,
             na_w_in=na_w_in, na_rpb=na_rpb, s5_w_in=s5_w_in, s5_a_re=s5_a_re, s5_a_im=s5_a_im,
             s5_log_dt=s5_log_dt, s5_b_re=s5_b_re, s5_b_im=s5_b_im, s5_c_re=s5_c_re, s5_c_im=s5_c_im,
             s5_d=s5_d, s5_w_glu=s5_w_glu, s5_b_glu=s5_b_glu, mem_w_kv=mem_w_kv, w_out=w_out,
             moe_w_rg=moe_w_rg, moe_b_rg=moe_b_rg, moe_w_re=moe_w_re, moe_b_re=moe_b_re,
             moe_w_gate=moe_w_gate, moe_w_up=moe_w_up, moe_w_down=moe_w_down)
    y_prompt = encoder_trunk(x_prompt, mem_prompt, p)
    y_sample = encoder_trunk(x_sample, mem_sample, p)
    return (y_prompt, y_sample)
```

```python
import functools
import math

import numpy as np
import jax
import jax.numpy as jnp
from jax import lax
from jax.experimental import pallas as pl
from jax.experimental.pallas import tpu as pltpu

F32 = jnp.float32
BF16 = jnp.bfloat16
U32 = jnp.uint32

LANE = 128
SUBLANE = 8
VMEM_LIMIT_BYTES = 56 * 1024 * 1024

RMS_EPS = 1e-6
NEG_INF = -1e30
GRID_W = 64
ROW_WIN = 8
COL_WIN = 16
NA_HEAD_DIM = 128
MEM_HEADS = 4
S5_GROUP = 16
S5_STATE = 64
N_EXPERT_GROUPS = 4
EXPERTS_PER_GROUP = 4
N_EXPERTS = N_EXPERT_GROUPS * EXPERTS_PER_GROUP

NA_Q_ROWS = 8
NA_K_ROWS = 2 * ROW_WIN
SUPER = 8192
S5_SEGMENTS = SUBLANE
S5_TBLK = 64
MOE_TILE = 256
GATHER_CHUNK = 512


def _cparams(*sem):
    return pltpu.CompilerParams(dimension_semantics=sem, vmem_limit_bytes=VMEM_LIMIT_BYTES)


def _rms_norm(x, gain):
    ms = jnp.mean(x * x, axis=-1, keepdims=True)
    return x * lax.rsqrt(ms + RMS_EPS) * gain


def _sigmoid(x):
    return 1.0 / (1.0 + jnp.exp(-x))


def _pack_bf16_pairs(x):
    k = x.shape[1] // 2
    bits = lax.bitcast_convert_type(x.astype(BF16).astype(F32), U32)
    return (bits[:, k:] & jnp.uint32(0xFFFF0000)) | (bits[:, :k] >> 16)


def _unpack_bf16_pairs(w):
    lo = lax.bitcast_convert_type(w << 16, F32)
    hi = lax.bitcast_convert_type(w & jnp.uint32(0xFFFF0000), F32)
    return lo, hi


def _norm_proj_kernel(x_ref, g_ref, w_ref, o_ref, h_scr):
    @pl.when(pl.program_id(1) == 0)
    def _():
        h_scr[...] = _rms_norm(x_ref[...], g_ref[...]).astype(BF16)

    acc = jnp.dot(h_scr[...], w_ref[...], preferred_element_type=F32)
    for c in range(o_ref.shape[0]):
        o_ref[c] = acc[:, c * LANE:(c + 1) * LANE].astype(o_ref.dtype)


def norm_proj(x, gain, w, out_dtype, tm, tn=512):
    m, d = x.shape
    n_out = w.shape[1]
    assert m % tm == 0 and n_out % tn == 0 and tn % LANE == 0
    return pl.pallas_call(
        _norm_proj_kernel,
        grid=(m // tm, n_out // tn),
        in_specs=[pl.BlockSpec((tm, d), lambda i, j: (i, 0)),
                  pl.BlockSpec((1, d), lambda i, j: (0, 0)),
                  pl.BlockSpec((d, tn), lambda i, j: (0, j))],
        out_specs=pl.BlockSpec((tn // LANE, tm, LANE), lambda i, j: (j, i, 0)),
        out_shape=jax.ShapeDtypeStruct((n_out // LANE, m, LANE), out_dtype),
        scratch_shapes=[pltpu.VMEM((tm, d), BF16)],
        compiler_params=_cparams("parallel", "arbitrary"),
        name="norm_proj",
    )(x, gain.reshape(1, d), w)


def _na_kernel(ks_ref, ty_ref, q_ref, k_ref, v_ref, b_ref, o_ref, *, nqb, scale):
    step = pl.program_id(1) * nqb + pl.program_id(2)
    start = pl.multiple_of(ks_ref[step] * GRID_W, GRID_W)
    nk = NA_K_ROWS * GRID_W
    q = q_ref[0]
    k = k_ref[0, pl.ds(start, nk), :]
    v = v_ref[0, pl.ds(start, nk), :]
    s = lax.dot_general(q, k, (((1,), (1,)), ((), ())), preferred_element_type=F32)
    s = s * scale + b_ref[0, 0]
    m = jnp.max(s, axis=-1, keepdims=True)
    p = jnp.exp(s - m)
    l = jnp.sum(p, axis=-1, keepdims=True)
    o = jnp.dot(p.astype(BF16), v, preferred_element_type=F32) / l
    o_ref[...] = o.astype(o_ref.dtype)


def _na_tables(seq_lens):
    ks, ty = [], []
    tok = 0
    per_super = SUPER // GRID_W
    for length in seq_lens:
        rows = length // GRID_W
        assert length % (NA_Q_ROWS * GRID_W) == 0 and rows >= 3 * NA_Q_ROWS
        assert tok // SUPER == (tok + length - 1) // SUPER
        row0 = (tok % SUPER) // GRID_W
        for qb in range(rows // NA_Q_ROWS):
            ks.append(row0 + int(np.clip(qb * NA_Q_ROWS - ROW_WIN // 2, 0, rows - NA_K_ROWS)))
            ty.append(0 if qb == 0 else (2 if qb == rows // NA_Q_ROWS - 1 else 1))
        tok += length
    assert tok % SUPER == 0 and len(ks) == (tok // SUPER) * (per_super // NA_Q_ROWS)
    return np.asarray(ks, np.int32), np.asarray(ty, np.int32)


def _na_bias(rpb):
    heads = rpb.shape[0]
    cols = np.arange(GRID_W)
    c_start = np.clip(cols - COL_WIN // 2, 0, GRID_W - COL_WIN)
    in_win = (cols[None, :] >= c_start[:, None]) & (cols[None, :] < c_start[:, None] + COL_WIN)
    dc = np.clip(cols[None, :] - cols[:, None], -(COL_WIN - 1), COL_WIN - 1) + COL_WIN - 1
    col_bias = jnp.where(jnp.asarray(in_win), rpb.astype(F32)[:, :, dc], NEG_INF)
    i = np.arange(NA_Q_ROWS)[:, None]
    j = np.arange(NA_K_ROWS)[None, :]
    half = ROW_WIN // 2
    q_rel = [i, i + half, i + ROW_WIN]
    w_rel = [np.maximum(i - half, 0), i, np.minimum(i + half, ROW_WIN)]
    out = []
    for qr, wr in zip(q_rel, w_rel):
        valid = (j >= wr) & (j < wr + ROW_WIN)
        dr = np.clip(j - qr + ROW_WIN - 1, 0, 2 * ROW_WIN - 2)
        b = jnp.where(jnp.asarray(valid)[None, :, :, None, None], col_bias[:, dr], NEG_INF)
        out.append(b.transpose(0, 1, 3, 2, 4).reshape(heads, NA_Q_ROWS * GRID_W, NA_K_ROWS * GRID_W))
    return jnp.stack(out, axis=1)


def neighbourhood_attention(qkv, rpb, seq_lens):
    heads = rpb.shape[0]
    n = qkv.shape[1]
    nsup = n // SUPER
    tq = NA_Q_ROWS * GRID_W
    nqb = SUPER // tq
    ks_tab, ty_tab = _na_tables(seq_lens)
    bias = _na_bias(rpb)
    grid_spec = pltpu.PrefetchScalarGridSpec(
        num_scalar_prefetch=2,
        grid=(heads, nsup, nqb),
        in_specs=[
            pl.BlockSpec((1, tq, LANE), lambda h, s, b, ks, ty: (h, s * nqb + b, 0)),
            pl.BlockSpec((1, SUPER, LANE), lambda h, s, b, ks, ty: (heads + h, s, 0)),
            pl.BlockSpec((1, SUPER, LANE), lambda h, s, b, ks, ty: (2 * heads + h, s, 0)),
            pl.BlockSpec((1, 1, tq, NA_K_ROWS * GRID_W), lambda h, s, b, ks, ty: (h, ty[s * nqb + b], 0, 0)),
        ],
        out_specs=pl.BlockSpec((tq, LANE), lambda h, s, b, ks, ty: (s * nqb + b, h)),
    )
    return pl.pallas_call(
        functools.partial(_na_kernel, nqb=nqb, scale=NA_HEAD_DIM ** -0.5),
        grid_spec=grid_spec,
        out_shape=jax.ShapeDtypeStruct((n, heads * LANE), BF16),
        compiler_params=_cparams("parallel", "arbitrary", "arbitrary"),
        name="neighbourhood_attention",
    )(jnp.asarray(ks_tab), jnp.asarray(ty_tab), qkv, qkv, qkv, bias)


def _mem_attn_kernel(seq_ref, q_ref, kv_ref, o_ref, *, scale):
    del seq_ref
    hd = 2 * LANE
    for h in range(MEM_HEADS):
        q = jnp.concatenate([q_ref[2 * h], q_ref[2 * h + 1]], axis=-1).astype(BF16)
        k = jnp.concatenate([kv_ref[2 * h], kv_ref[2 * h + 1]], axis=-1)
        v = jnp.concatenate([kv_ref[2 * MEM_HEADS + 2 * h], kv_ref[2 * MEM_HEADS + 2 * h + 1]], axis=-1)
        s = lax.dot_general(q, k, (((1,), (1,)), ((), ())), preferred_element_type=F32) * scale
        m = jnp.max(s, axis=-1, keepdims=True)
        p = jnp.exp(s - m)
        l = jnp.sum(p, axis=-1, keepdims=True)
        o = jnp.dot(p.astype(BF16), v, preferred_element_type=F32) / l
        o_ref[:, h * hd:(h + 1) * hd] = o.astype(o_ref.dtype)


def memory_attention(proj, q_slab0, kv, blk_seq, n_mem, tq=512):
    n = proj.shape[1]
    nq = 2 * MEM_HEADS
    assert q_slab0 % nq == 0
    grid_spec = pltpu.PrefetchScalarGridSpec(
        num_scalar_prefetch=1,
        grid=(n // tq,),
        in_specs=[pl.BlockSpec((nq, tq, LANE), lambda i, sq: (q_slab0 // nq, i, 0)),
                  pl.BlockSpec((2 * nq, n_mem, LANE), lambda i, sq: (0, sq[i], 0))],
        out_specs=pl.BlockSpec((tq, nq * LANE), lambda i, sq: (i, 0)),
    )
    return pl.pallas_call(
        functools.partial(_mem_attn_kernel, scale=(2 * LANE) ** -0.5),
        grid_spec=grid_spec,
        out_shape=jax.ShapeDtypeStruct((n, nq * LANE), BF16),
        compiler_params=_cparams("parallel"),
        name="memory_attention",
    )(blk_seq, proj, kv)


def _s5_param_kernel(are_ref, aim_ref, ldt_ref, bre_ref, bim_ref, lre_ref, lim_ref, bbre_ref, bbim_ref):
    a_re = are_ref[...]
    a_im = aim_ref[...]
    dt = jnp.exp(ldt_ref[...])
    mag = jnp.exp(a_re * dt)
    ang = a_im * dt
    lam_re = mag * jnp.cos(ang)
    lam_im = mag * jnp.sin(ang)
    den = a_re * a_re + a_im * a_im
    n_re = lam_re - 1.0
    coef_re = (n_re * a_re + lam_im * a_im) / den
    coef_im = (lam_im * a_re - n_re * a_im) / den
    lre_ref[...] = lam_re
    lim_ref[...] = lam_im
    for c in range(bre_ref.shape[0]):
        b_re = bre_ref[c]
        b_im = bim_ref[c]
        bbre_ref[c] = coef_re * b_re - coef_im * b_im
        bbim_ref[c] = coef_re * b_im + coef_im * b_re


def s5_discretise(a_re, a_im, log_dt, b_re, b_im):
    two, g, p = a_re.shape
    c = b_re.shape[-1]
    rows = two * g
    ldt = jnp.broadcast_to(log_dt.reshape(rows, 1), (rows, p))
    b_re_t = b_re.reshape(rows, p, c).transpose(2, 0, 1)
    b_im_t = b_im.reshape(rows, p, c).transpose(2, 0, 1)
    return pl.pallas_call(
        _s5_param_kernel,
        out_shape=(jax.ShapeDtypeStruct((rows, p), F32), jax.ShapeDtypeStruct((rows, p), F32),
                   jax.ShapeDtypeStruct((c, rows, p), F32), jax.ShapeDtypeStruct((c, rows, p), F32)),
        name="s5_discretise",
    )(a_re.reshape(rows, p), a_im.reshape(rows, p), ldt, b_re_t, b_im_t)


def _s5_operands(lam_re, lam_im, bb_re, bb_im, c_re, c_im, d):
    c, rows, p = bb_re.shape
    g = rows // 2
    gs = LANE // c
    ns = g // gs
    eye = jnp.eye(gs, dtype=F32)

    def bmat(bb):
        x = bb.reshape(c, 2, ns, gs, p).transpose(1, 2, 3, 0, 4)
        return jnp.einsum('dsgcp,gh->dsgchp', x, eye).reshape(2, ns, gs * c, gs * p)

    def cmat(cc):
        x = cc.astype(F32).reshape(2, ns, gs, c, p)
        return jnp.einsum('dsgcp,gh->dsgphc', x, eye).reshape(2, ns, gs * p, gs * c)

    b_op = jnp.concatenate([bmat(bb_re), bmat(bb_im)], axis=-1).astype(BF16)
    c_op = jnp.concatenate([cmat(c_re), -cmat(c_im)], axis=-2).astype(BF16)
    lam = jnp.concatenate([lam_re.reshape(2, ns, gs * p), lam_im.reshape(2, ns, gs * p)], axis=-1)
    lam = jnp.broadcast_to(lam[:, :, None, :], (2, ns, SUBLANE, 2 * gs * p))
    d_op = d.astype(F32).reshape(ns, 1, gs * c)
    return b_op, c_op, lam, d_op


def _s5_kernel(keep_ref, u_ref, bm_ref, cm_ref, lam_ref, d_ref, o_ref,
               y_scr, lhs_scr, bu_scr, xs_scr, yb_scr, e_scr, i_scr, *, lseg, tblk):
    sup = pl.program_id(0)
    nblk = lseg // tblk
    ns = bu_scr.shape[1] // 2
    nchunk = ns // LANE
    u2 = u_ref.at[0]
    y_scr[...] = u_ref[0] * d_ref[0]

    for direction in range(2):
        rev = direction == 1
        lam_r = [lam_ref[direction, 0, :, c * LANE:(c + 1) * LANE] for c in range(nchunk)]
        lam_i = [lam_ref[direction, 0, :, ns + c * LANE:ns + (c + 1) * LANE] for c in range(nchunk)]

        def local_time(blk, t, rev=rev):
            s = blk * tblk + t
            return (lseg - 1 - s) if rev else s

        def build_bu(blk, direction=direction, local_time=local_time):
            def gather_rows(t, carry):
                lhs_scr[pl.ds(pl.multiple_of(t * SUBLANE, SUBLANE), SUBLANE), :] = (
                    u2[pl.ds(local_time(blk, t), SUBLANE, stride=lseg), :])
                return carry
            lax.fori_loop(0, tblk, gather_rows, 0)
            bu_scr[...] = jnp.dot(lhs_scr[...].astype(BF16), bm_ref[direction, 0],
                                  preferred_element_type=F32)

        def scan_block(x, store, lam_r=lam_r, lam_i=lam_i):
            def step(t, x):
                base = pl.multiple_of(t * SUBLANE, SUBLANE)
                new_r, new_i = [], []
                for c in range(nchunk):
                    xr, xi = x[c], x[nchunk + c]
                    b_r = bu_scr[pl.ds(base, SUBLANE), c * LANE:(c + 1) * LANE]
                    b_i = bu_scr[pl.ds(base, SUBLANE), ns + c * LANE:ns + (c + 1) * LANE]
                    r = lam_r[c] * xr - lam_i[c] * xi + b_r
                    i = lam_r[c] * xi + lam_i[c] * xr + b_i
                    if store:
                        xs_scr[pl.ds(base, SUBLANE), c * LANE:(c + 1) * LANE] = r
                        xs_scr[pl.ds(base, SUBLANE), ns + c * LANE:ns + (c + 1) * LANE] = i
                    new_r.append(r)
                    new_i.append(i)
                return tuple(new_r + new_i)
            return lax.fori_loop(0, tblk, step, x, unroll=8)

        def pass_a(blk, x, build_bu=build_bu, scan_block=scan_block):
            build_bu(blk)
            return scan_block(x, False)
        zero = tuple(jnp.zeros((SUBLANE, LANE), F32) for _ in range(2 * nchunk))
        ends = lax.fori_loop(0, nblk, pass_a, zero)
        for c in range(2 * nchunk):
            e_scr[:, c * LANE:(c + 1) * LANE] = ends[c]

        p_r = lam_ref[direction, 0, 0:1, 0:ns]
        p_i = lam_ref[direction, 0, 0:1, ns:2 * ns]
        for _ in range(int(math.log2(lseg))):
            p_r, p_i = p_r * p_r - p_i * p_i, 2.0 * p_r * p_i
        order = list(range(S5_SEGMENTS))[::-1] if rev else list(range(S5_SEGMENTS))
        s_r = jnp.zeros((1, ns), F32)
        s_i = jnp.zeros((1, ns), F32)
        i_scr[order[0]:order[0] + 1, :] = jnp.zeros((1, 2 * ns), F32)
        for prev, cur in zip(order[:-1], order[1:]):
            e_r = e_scr[prev:prev + 1, 0:ns]
            e_i = e_scr[prev:prev + 1, ns:2 * ns]
            keep = keep_ref[(sup * 2 + direction) * S5_SEGMENTS + cur].astype(F32)
            n_r = (p_r * s_r - p_i * s_i + e_r) * keep
            n_i = (p_r * s_i + p_i * s_r + e_i) * keep
            i_scr[cur:cur + 1, 0:ns] = n_r
            i_scr[cur:cur + 1, ns:2 * ns] = n_i
            s_r, s_i = n_r, n_i

        def pass_b(blk, x, direction=direction, build_bu=build_bu, scan_block=scan_block,
                   local_time=local_time):
            build_bu(blk)
            x = scan_block(x, True)
            yb_scr[...] = jnp.dot(xs_scr[...].astype(BF16), cm_ref[direction, 0],
                                  preferred_element_type=F32)

            def scatter_rows(t, carry):
                rows = pl.ds(local_time(blk, t), SUBLANE, stride=lseg)
                y_scr[rows, :] = y_scr[rows, :] + yb_scr[pl.ds(pl.multiple_of(t * SUBLANE, SUBLANE), SUBLANE), :]
                return carry
            lax.fori_loop(0, tblk, scatter_rows, 0)
            return x
        start = tuple(i_scr[:, c * LANE:(c + 1) * LANE] for c in range(2 * nchunk))
        lax.fori_loop(0, nblk, pass_b, start)

    y = y_scr[...]
    gelu = 0.5 * y * (1.0 + jnp.tanh(math.sqrt(2.0 / math.pi) * (y + 0.044715 * (y * y * y))))
    o_ref[...] = gelu.astype(o_ref.dtype)


def _s5_keep_table(seq_lens, lseg):
    starts, ends = set(), set()
    tok = 0
    for length in seq_lens:
        assert length % lseg == 0
        starts.add(tok // lseg)
        tok += length
        ends.add(tok // lseg - 1)
    nseg = tok // lseg
    assert nseg % S5_SEGMENTS == 0
    keep = np.ones((nseg // S5_SEGMENTS, 2, S5_SEGMENTS), np.int32)
    for s in range(nseg):
        if s in starts:
            keep[s // S5_SEGMENTS, 0, s % S5_SEGMENTS] = 0
        if s in ends:
            keep[s // S5_SEGMENTS, 1, s % S5_SEGMENTS] = 0
    return keep.reshape(-1)


def s5_mixer(proj, b_op, c_op, lam, d_op, seq_lens):
    n = proj.shape[1]
    ns = b_op.shape[1]
    nstate2 = b_op.shape[-1]
    nsup = n // SUPER
    lseg = SUPER // S5_SEGMENTS
    tblk = min(S5_TBLK, lseg)
    assert lseg & (lseg - 1) == 0 and lseg % tblk == 0
    keep = _s5_keep_table(seq_lens, lseg)
    rows = SUBLANE * tblk
    grid_spec = pltpu.PrefetchScalarGridSpec(
        num_scalar_prefetch=1,
        grid=(nsup, ns),
        in_specs=[
            pl.BlockSpec((1, SUPER, LANE), lambda b, s, kp: (s, b, 0)),
            pl.BlockSpec((2, 1, LANE, nstate2), lambda b, s, kp: (0, s, 0, 0)),
            pl.BlockSpec((2, 1, nstate2, LANE), lambda b, s, kp: (0, s, 0, 0)),
            pl.BlockSpec((2, 1, SUBLANE, nstate2), lambda b, s, kp: (0, s, 0, 0)),
            pl.BlockSpec((1, 1, LANE), lambda b, s, kp: (s, 0, 0)),
        ],
        out_specs=pl.BlockSpec((SUPER, LANE), lambda b, s, kp: (b, s)),
        scratch_shapes=[
            pltpu.VMEM((SUPER, LANE), F32),
            pltpu.VMEM((rows, LANE), F32),
            pltpu.VMEM((rows, nstate2), F32),
            pltpu.VMEM((rows, nstate2), F32),
            pltpu.VMEM((rows, LANE), F32),
            pltpu.VMEM((SUBLANE, nstate2), F32),
            pltpu.VMEM((SUBLANE, nstate2), F32),
        ],
    )
    return pl.pallas_call(
        functools.partial(_s5_kernel, lseg=lseg, tblk=tblk),
        grid_spec=grid_spec,
        out_shape=jax.ShapeDtypeStruct((n, ns * LANE), BF16),
        compiler_params=_cparams("parallel", "arbitrary"),
        name="s5_mixer",
    )(jnp.asarray(keep), proj, b_op, c_op, lam, d_op)


def _glu_kernel(g_ref, w_ref, b_ref, o_ref):
    tn = o_ref.shape[1]
    col = pl.multiple_of(pl.program_id(1) * tn, tn)
    acc = jnp.dot(g_ref[...], w_ref[...], preferred_element_type=F32) + b_ref[...]
    g = g_ref[:, pl.ds(col, tn)].astype(F32)
    o_ref[...] = (g * _sigmoid(acc)).astype(o_ref.dtype)


def glu(g, w, b, tm=1024, tn=512):
    m, k = g.shape
    assert m % tm == 0 and k % tn == 0
    return pl.pallas_call(
        _glu_kernel,
        grid=(m // tm, k // tn),
        in_specs=[pl.BlockSpec((tm, k), lambda i, j: (i, 0)),
                  pl.BlockSpec((k, tn), lambda i, j: (0, j)),
                  pl.BlockSpec((1, tn), lambda i, j: (0, j))],
        out_specs=pl.BlockSpec((tm, tn), lambda i, j: (i, j)),
        out_shape=jax.ShapeDtypeStruct((m, k), BF16),
        compiler_params=_cparams("parallel", "arbitrary"),
        name="glu",
    )(g, w, b.reshape(1, k).astype(F32))


def _out_proj_kernel(x_ref, tok_ref, mo_ref, w1_ref, w2_ref, o_ref):
    acc = jnp.dot(tok_ref[...], w1_ref[...], preferred_element_type=F32)
    acc = acc + jnp.dot(mo_ref[...], w2_ref[...], preferred_element_type=F32)
    o_ref[...] = x_ref[...] + acc


def out_proj(x, tok, mo, w, tm=1024, tn=512):
    m, d = x.shape
    k1, k2 = tok.shape[1], mo.shape[1]
    assert m % tm == 0 and d % tn == 0 and w.shape[0] == k1 + k2 and k1 % k2 == 0
    return pl.pallas_call(
        _out_proj_kernel,
        grid=(m // tm, d // tn),
        in_specs=[pl.BlockSpec((tm, tn), lambda i, j: (i, j)),
                  pl.BlockSpec((tm, k1), lambda i, j: (i, 0)),
                  pl.BlockSpec((tm, k2), lambda i, j: (i, 0)),
                  pl.BlockSpec((k1, tn), lambda i, j: (0, j)),
                  pl.BlockSpec((k2, tn), lambda i, j: (k1 // k2, j))],
        out_specs=pl.BlockSpec((tm, tn), lambda i, j: (i, j)),
        out_shape=jax.ShapeDtypeStruct((m, d), F32),
        compiler_params=_cparams("parallel", "arbitrary"),
        name="out_proj",
    )(x, tok, mo, w, w)


def _router_kernel(x_ref, g_ref, w_ref, b_ref, h_ref, meta_ref):
    t = _rms_norm(x_ref[...], g_ref[...])
    h_ref[...] = _pack_bf16_pairs(t)
    logits = jnp.dot(t, w_ref[...], preferred_element_type=F32, precision=lax.Precision.HIGHEST) + b_ref[...]
    lane = lax.broadcasted_iota(jnp.int32, logits.shape, 1)
    big = jnp.int32(LANE)
    is_g = lane < N_EXPERT_GROUPS
    gl = jnp.where(is_g, logits, -jnp.inf)
    gmax = jnp.max(gl, axis=-1, keepdims=True)
    gsum = jnp.sum(jnp.where(is_g, jnp.exp(gl - gmax), 0.0), axis=-1, keepdims=True)
    g_val = 1.0 / gsum
    g_idx = jnp.min(jnp.where(gl == gmax, lane, big), axis=-1, keepdims=True)
    e_lane = lane - N_EXPERT_GROUPS
    in_grp = (e_lane >= 0) & (e_lane < N_EXPERTS) & ((e_lane >> 2) == g_idx)
    el = jnp.where(in_grp, logits, -jnp.inf)
    v1 = jnp.max(el, axis=-1, keepdims=True)
    i1 = jnp.min(jnp.where(el == v1, lane, big), axis=-1, keepdims=True)
    el2 = jnp.where(lane == i1, -jnp.inf, el)
    v2 = jnp.max(el2, axis=-1, keepdims=True)
    i2 = jnp.min(jnp.where(el2 == v2, lane, big), axis=-1, keepdims=True)
    z = jnp.exp(v2 - v1)
    w1 = g_val / (1.0 + z)
    w2 = g_val * z / (1.0 + z)
    e1 = (i1 - N_EXPERT_GROUPS).astype(F32)
    e2 = (i2 - N_EXPERT_GROUPS).astype(F32)
    meta = jnp.where(lane == 0, w1, jnp.where(lane == 1, w2, jnp.where(lane == 2, e1, jnp.where(lane == 3, e2, 0.0))))
    meta_ref[...] = meta


def router(x, gain, w_rg, b_rg, w_re, b_re, tm=512):
    m, d = x.shape
    assert EXPERTS_PER_GROUP == 4 and m % tm == 0
    pad = LANE - N_EXPERT_GROUPS - N_EXPERTS
    w = jnp.concatenate([w_rg, w_re, jnp.zeros((d, pad), F32)], axis=1).astype(F32)
    b = jnp.concatenate([b_rg, b_re, jnp.zeros((pad,), F32)]).astype(F32).reshape(1, LANE)
    return pl.pallas_call(
        _router_kernel,
        grid=(m // tm,),
        in_specs=[pl.BlockSpec((tm, d), lambda i: (i, 0)),
                  pl.BlockSpec((1, d), lambda i: (0, 0)),
                  pl.BlockSpec((d, LANE), lambda i: (0, 0)),
                  pl.BlockSpec((1, LANE), lambda i: (0, 0))],
        out_specs=(pl.BlockSpec((tm, d // 2), lambda i: (i, 0)),
                   pl.BlockSpec((tm, LANE), lambda i: (i, 0))),
        out_shape=(jax.ShapeDtypeStruct((m, d // 2), U32), jax.ShapeDtypeStruct((m, LANE), F32)),
        compiler_params=_cparams("parallel"),
        name="moe_router",
    )(x, gain.reshape(1, d), w, b)


def _routing_plan(meta, n_rows):
    n = meta.shape[0]
    e = meta[:, 2:4].astype(jnp.int32)
    onehot = (e[:, :, None] == jnp.arange(N_EXPERTS)[None, None, :]).any(axis=1).astype(jnp.int32)
    counts = onehot.sum(axis=0)
    rank = jnp.cumsum(onehot, axis=0) - onehot
    padded = ((counts + MOE_TILE - 1) // MOE_TILE) * MOE_TILE
    ends = jnp.cumsum(padded)
    offs = ends - padded
    pos = jnp.take_along_axis(offs[None, :] + rank, e, axis=1).astype(jnp.int32)
    tok = jnp.arange(n, dtype=jnp.int32)
    sorted_tok = jnp.zeros((n_rows,), jnp.int32).at[pos[:, 0]].set(tok).at[pos[:, 1]].set(tok)
    tile_start = jnp.arange(n_rows // MOE_TILE, dtype=jnp.int32) * MOE_TILE
    tile_expert = jnp.minimum(jnp.searchsorted(ends, tile_start, side='right'), N_EXPERTS - 1).astype(jnp.int32)
    tile_valid = (tile_start < ends[-1]).astype(jnp.int32)
    return pos[:, 0], pos[:, 1], sorted_tok, tile_expert, tile_valid


def _gather_rows_kernel(idx_ref, src_ref, dst_ref, sem, *, chunk):
    i = pl.program_id(0)
    last = pl.num_programs(0) - 1

    def row_copy(c, r):
        row = c * chunk + r
        return pltpu.make_async_copy(src_ref.at[pl.ds(idx_ref[row], 1), :],
                                     dst_ref.at[pl.ds(row, 1), :], sem.at[c % 2])

    def issue(c):
        def body(r, carry):
            row_copy(c, r).start()
            return carry
        lax.fori_loop(0, chunk, body, 0)

    def drain(c):
        def body(r, carry):
            row_copy(c, r).wait()
            return carry
        lax.fori_loop(0, chunk, body, 0)

    issue(i)

    @pl.when(i > 0)
    def _():
        drain(i - 1)

    @pl.when(i == last)
    def _():
        drain(i)


def gather_rows(idx, src, chunk=GATHER_CHUNK):
    n_rows = idx.shape[0]
    assert n_rows % chunk == 0
    grid_spec = pltpu.PrefetchScalarGridSpec(
        num_scalar_prefetch=1,
        grid=(n_rows // chunk,),
        in_specs=[pl.BlockSpec(memory_space=pl.ANY)],
        out_specs=pl.BlockSpec(memory_space=pl.ANY),
        scratch_shapes=[pltpu.SemaphoreType.DMA((2,))],
    )
    return pl.pallas_call(
        functools.partial(_gather_rows_kernel, chunk=chunk),
        grid_spec=grid_spec,
        out_shape=jax.ShapeDtypeStruct((n_rows, src.shape[1]), src.dtype),
        compiler_params=_cparams("arbitrary"),
        name="moe_gather_rows",
    )(idx, src)


def _moe_up_kernel(te_ref, tv_ref, xs_ref, wg_ref, wu_ref, o_ref):
    del te_ref
    t = pl.program_id(1)

    @pl.when(tv_ref[t] == 1)
    def _():
        lo, hi = _unpack_bf16_pairs(xs_ref[...])
        lo = lo.astype(BF16)
        hi = hi.astype(BF16)
        half = lo.shape[1]
        g = jnp.dot(lo, wg_ref[0, :half], preferred_element_type=F32)
        g = g + jnp.dot(hi, wg_ref[0, half:], preferred_element_type=F32)
        u = jnp.dot(lo, wu_ref[0, :half], preferred_element_type=F32)
        u = u + jnp.dot(hi, wu_ref[0, half:], preferred_element_type=F32)
        o_ref[...] = (g * _sigmoid(g) * u).astype(o_ref.dtype)

    @pl.when(tv_ref[t] == 0)
    def _():
        o_ref[...] = jnp.zeros(o_ref.shape, o_ref.dtype)


def moe_up(xs, w_gate, w_up, tile_expert, tile_valid, tf=512):
    n_rows, half = xs.shape
    d = 2 * half
    f = w_gate.shape[-1]
    assert f % tf == 0 and n_rows % MOE_TILE == 0
    grid_spec = pltpu.PrefetchScalarGridSpec(
        num_scalar_prefetch=2,
        grid=(f // tf, n_rows // MOE_TILE),
        in_specs=[pl.BlockSpec((MOE_TILE, half), lambda j, t, te, tv: (t, 0)),
                  pl.BlockSpec((1, d, tf), lambda j, t, te, tv: (te[t], 0, j)),
                  pl.BlockSpec((1, d, tf), lambda j, t, te, tv: (te[t], 0, j))],
        out_specs=pl.BlockSpec((MOE_TILE, tf), lambda j, t, te, tv: (t, j)),
    )
    return pl.pallas_call(
        _moe_up_kernel,
        grid_spec=grid_spec,
        out_shape=jax.ShapeDtypeStruct((n_rows, f), BF16),
        compiler_params=_cparams("arbitrary", "arbitrary"),
        name="moe_up",
    )(tile_expert, tile_valid, xs, w_gate, w_up)


def _moe_down_kernel(te_ref, tv_ref, h_ref, wd_ref, o_ref):
    del te_ref
    t = pl.program_id(0)

    @pl.when(tv_ref[t] == 1)
    def _():
        y = jnp.dot(h_ref[...], wd_ref[0], preferred_element_type=F32)
        o_ref[...] = _pack_bf16_pairs(y)

    @pl.when(tv_ref[t] == 0)
    def _():
        o_ref[...] = jnp.zeros(o_ref.shape, o_ref.dtype)


def moe_down(hid, w_down, tile_expert, tile_valid):
    n_rows, f = hid.shape
    d = w_down.shape[-1]
    grid_spec = pltpu.PrefetchScalarGridSpec(
        num_scalar_prefetch=2,
        grid=(n_rows // MOE_TILE,),
        in_specs=[pl.BlockSpec((MOE_TILE, f), lambda t, te, tv: (t, 0)),
                  pl.BlockSpec((1, f, d), lambda t, te, tv: (te[t], 0, 0))],
        out_specs=pl.BlockSpec((MOE_TILE, d // 2), lambda t, te, tv: (t, 0)),
    )
    return pl.pallas_call(
        _moe_down_kernel,
        grid_spec=grid_spec,
        out_shape=jax.ShapeDtypeStruct((n_rows, d // 2), U32),
        compiler_params=_cparams("arbitrary"),
        name="moe_down",
    )(tile_expert, tile_valid, hid, w_down)


def _combine_kernel(p0_ref, p1_ref, x_ref, meta_ref, g_ref, ys_ref, o_ref, buf, sem, *, tc, base, final_norm):
    i = pl.program_id(0)

    def row_copies(r):
        n = base + i * tc + r
        return (pltpu.make_async_copy(ys_ref.at[pl.ds(p0_ref[n], 1), :], buf.at[0, pl.ds(r, 1), :], sem.at[0]),
                pltpu.make_async_copy(ys_ref.at[pl.ds(p1_ref[n], 1), :], buf.at[1, pl.ds(r, 1), :], sem.at[1]))

    def issue(r, carry):
        a, b = row_copies(r)
        a.start()
        b.start()
        return carry

    def drain(r, carry):
        a, b = row_copies(r)
        a.wait()
        b.wait()
        return carry

    lax.fori_loop(0, tc, issue, 0)
    lax.fori_loop(0, tc, drain, 0)
    meta = meta_ref[...]
    w0 = meta[:, 0:1]
    w1 = meta[:, 1:2]
    lo0, hi0 = _unpack_bf16_pairs(buf[0])
    lo1, hi1 = _unpack_bf16_pairs(buf[1])
    half = lo0.shape[1]
    y_lo = x_ref[:, :half] + w0 * lo0 + w1 * lo1
    y_hi = x_ref[:, half:] + w0 * hi0 + w1 * hi1
    if final_norm:
        ms = (jnp.sum(y_lo * y_lo, axis=-1, keepdims=True) + jnp.sum(y_hi * y_hi, axis=-1, keepdims=True)) / (2 * half)
        inv = lax.rsqrt(ms + RMS_EPS)
        y_lo = y_lo * inv * g_ref[:, :half]
        y_hi = y_hi * inv * g_ref[:, half:]
    o_ref[:, :half] = y_lo
    o_ref[:, half:] = y_hi


def moe_combine(x, meta, ys, pos0, pos1, gain, *, base, count, final_norm, tc=256):
    d = x.shape[1]
    assert base % tc == 0 and count % tc == 0
    blk0 = base // tc
    grid_spec = pltpu.PrefetchScalarGridSpec(
        num_scalar_prefetch=2,
        grid=(count // tc,),
        in_specs=[pl.BlockSpec((tc, d), lambda i, p0, p1: (blk0 + i, 0)),
                  pl.BlockSpec((tc, LANE), lambda i, p0, p1: (blk0 + i, 0)),
                  pl.BlockSpec((1, d), lambda i, p0, p1: (0, 0)),
                  pl.BlockSpec(memory_space=pl.ANY)],
        out_specs=pl.BlockSpec((tc, d), lambda i, p0, p1: (i, 0)),
        scratch_shapes=[pltpu.VMEM((2, tc, d // 2), U32), pltpu.SemaphoreType.DMA((2,))],
    )
    return pl.pallas_call(
        functools.partial(_combine_kernel, tc=tc, base=base, final_norm=final_norm),
        grid_spec=grid_spec,
        out_shape=jax.ShapeDtypeStruct((count, d), F32),
        compiler_params=_cparams("arbitrary"),
        name="moe_combine",
    )(pos0, pos1, x, meta, gain.reshape(1, d), ys)


def moe_experts(x, gain, w_rg, b_rg, w_re, b_re, w_gate, w_up, w_down):
    n = x.shape[0]
    n_rows = 2 * n + N_EXPERTS * MOE_TILE
    n_rows = ((n_rows + GATHER_CHUNK - 1) // GATHER_CHUNK) * GATHER_CHUNK
    h, meta = router(x, gain, w_rg, b_rg, w_re, b_re)
    pos0, pos1, sorted_tok, tile_expert, tile_valid = _routing_plan(meta, n_rows)
    xs = gather_rows(sorted_tok, h)
    hid = moe_up(xs, w_gate, w_up, tile_expert, tile_valid)
    ys = moe_down(hid, w_down, tile_expert, tile_valid)
    return meta, ys, pos0, pos1


def kernel(x_prompt, x_sample, mem_prompt, mem_sample, norm_mix, norm_mem, norm_ffn, norm_final, na_w_in, na_rpb, s5_w_in, s5_a_re, s5_a_im, s5_log_dt, s5_b_re, s5_b_im, s5_c_re, s5_c_im, s5_d, s5_w_glu, s5_b_glu, mem_w_kv, w_out, moe_w_rg, moe_b_rg, moe_w_re, moe_b_re, moe_w_gate, moe_w_up, moe_w_down):
    d = x_prompt.shape[-1]
    groups = [x_prompt, x_sample]
    mems = [mem_prompt, mem_sample]
    seq_lens = [g.shape[1] for g in groups for _ in range(g.shape[0])]
    n_mem = mem_prompt.shape[1]
    depth = norm_mix.shape[0]
    x = jnp.concatenate([g.reshape(-1, d) for g in groups], axis=0)
    mem = jnp.concatenate([m.reshape(-1, d) for m in mems], axis=0)
    n = x.shape[0]
    tq_mem = 512
    blk_seq, tok = [], 0
    for s, length in enumerate(seq_lens):
        assert length % tq_mem == 0
        blk_seq += [s] * (length // tq_mem)
    blk_seq = jnp.asarray(np.asarray(blk_seq, np.int32))
    na_heads = na_rpb.shape[1]
    na_width = na_heads * NA_HEAD_DIM

    meta = ys = pos0 = pos1 = None
    for i in range(depth):
        j = i // 2
        if i > 0:
            x = moe_combine(x, meta, ys, pos0, pos1, norm_final, base=0, count=n, final_norm=False)
        kv = norm_proj(mem, norm_mem[i], mem_w_kv[i].astype(BF16), BF16, tm=n_mem)
        if i % 2 == 0:
            proj = norm_proj(x, norm_mix[i], na_w_in[j].astype(BF16), BF16, tm=512)
            tok = neighbourhood_attention(proj, na_rpb[j], seq_lens)
            q_slab0 = 3 * na_heads
        else:
            proj = norm_proj(x, norm_mix[i], s5_w_in[j].astype(BF16), F32, tm=512)
            lam_re, lam_im, bb_re, bb_im = s5_discretise(s5_a_re[j], s5_a_im[j], s5_log_dt[j], s5_b_re[j], s5_b_im[j])
            b_op, c_op, lam, d_op = _s5_operands(lam_re, lam_im, bb_re, bb_im, s5_c_re[j], s5_c_im[j], s5_d[j])
            g = s5_mixer(proj, b_op, c_op, lam, d_op, seq_lens)
            tok = glu(g, s5_w_glu[j].astype(BF16), s5_b_glu[j])
            q_slab0 = b_op.shape[1]
        mo = memory_attention(proj, q_slab0, kv, blk_seq, n_mem, tq=tq_mem)
        x = out_proj(x, tok, mo, w_out[i].astype(BF16))
        meta, ys, pos0, pos1 = moe_experts(x, norm_ffn[i], moe_w_rg[i], moe_b_rg[i], moe_w_re[i], moe_b_re[i],
                                           moe_w_gate[i].astype(BF16), moe_w_up[i].astype(BF16),
                                           moe_w_down[i].astype(BF16))
    outs, base = [], 0
    for g in groups:
        count = g.shape[0] * g.shape[1]
        y = moe_combine(x, meta, ys, pos0, pos1, norm_final, base=base, count=count, final_norm=True)
        outs.append(y.reshape(g.shape))
        base += count
    return tuple(outs)
```

```python
import functools
import math

import numpy as np
import jax
import jax.numpy as jnp
from jax import lax
from jax.experimental import pallas as pl
from jax.experimental.pallas import tpu as pltpu

F32 = jnp.float32
BF16 = jnp.bfloat16
U32 = jnp.uint32

LANE = 128
SUBLANE = 8
VMEM_LIMIT_BYTES = 56 * 1024 * 1024

RMS_EPS = 1e-6
NEG_INF = -1e30
GRID_W = 64
ROW_WIN = 8
COL_WIN = 16
NA_HEAD_DIM = 128
MEM_HEADS = 4
S5_GROUP = 16
S5_STATE = 64
N_EXPERT_GROUPS = 4
EXPERTS_PER_GROUP = 4
N_EXPERTS = N_EXPERT_GROUPS * EXPERTS_PER_GROUP

NA_Q_ROWS = 8
NA_K_ROWS = 2 * ROW_WIN
SUPER = 8192
S5_SEGMENTS = SUBLANE
S5_TBLK = 64
MOE_TILE = 256


def _cparams(*sem):
    return pltpu.CompilerParams(dimension_semantics=sem, vmem_limit_bytes=VMEM_LIMIT_BYTES)


def _rms_norm(x, gain):
    ms = jnp.mean(x * x, axis=-1, keepdims=True)
    return x * lax.rsqrt(ms + RMS_EPS) * gain


def _sigmoid(x):
    return 1.0 / (1.0 + jnp.exp(-x))


def _pack_bf16_pairs(x):
    k = x.shape[1] // 2
    bits = lax.bitcast_convert_type(x.astype(BF16).astype(F32), U32)
    return (bits[:, k:] & jnp.uint32(0xFFFF0000)) | (bits[:, :k] >> 16)


def _unpack_bf16_pairs(w):
    lo = lax.bitcast_convert_type(w << 16, F32)
    hi = lax.bitcast_convert_type(w & jnp.uint32(0xFFFF0000), F32)
    return lo, hi


def _norm_proj_kernel(x_ref, g_ref, w_ref, o_ref, h_scr):
    @pl.when(pl.program_id(1) == 0)
    def _():
        h_scr[...] = _rms_norm(x_ref[...], g_ref[...]).astype(BF16)

    acc = jnp.dot(h_scr[...], w_ref[...], preferred_element_type=F32)
    for c in range(o_ref.shape[0]):
        o_ref[c] = acc[:, c * LANE:(c + 1) * LANE].astype(o_ref.dtype)


def norm_proj(x, gain, w, out_dtype, tm, tn=512):
    m, d = x.shape
    n_out = w.shape[1]
    assert m % tm == 0 and n_out % tn == 0 and tn % LANE == 0
    return pl.pallas_call(
        _norm_proj_kernel,
        grid=(m // tm, n_out // tn),
        in_specs=[pl.BlockSpec((tm, d), lambda i, j: (i, 0)),
                  pl.BlockSpec((1, d), lambda i, j: (0, 0)),
                  pl.BlockSpec((d, tn), lambda i, j: (0, j))],
        out_specs=pl.BlockSpec((tn // LANE, tm, LANE), lambda i, j: (j, i, 0)),
        out_shape=jax.ShapeDtypeStruct((n_out // LANE, m, LANE), out_dtype),
        scratch_shapes=[pltpu.VMEM((tm, d), BF16)],
        compiler_params=_cparams("parallel", "arbitrary"),
        name="norm_proj",
    )(x, gain.reshape(1, d), w)


def _na_kernel(ks_ref, ty_ref, q_ref, k_ref, v_ref, b_ref, o_ref, *, nqb, scale):
    step = pl.program_id(1) * nqb + pl.program_id(2)
    start = pl.multiple_of(ks_ref[step] * GRID_W, GRID_W)
    nk = NA_K_ROWS * GRID_W
    q = q_ref[0]
    k = k_ref[0, pl.ds(start, nk), :]
    v = v_ref[0, pl.ds(start, nk), :]
    s = lax.dot_general(q, k, (((1,), (1,)), ((), ())), preferred_element_type=F32)
    s = s * scale + b_ref[0, 0]
    m = jnp.max(s, axis=-1, keepdims=True)
    p = jnp.exp(s - m)
    l = jnp.sum(p, axis=-1, keepdims=True)
    o = jnp.dot(p.astype(BF16), v, preferred_element_type=F32) / l
    o_ref[...] = o.astype(o_ref.dtype)


def _na_tables(seq_lens):
    ks, ty = [], []
    tok = 0
    per_super = SUPER // GRID_W
    for length in seq_lens:
        rows = length // GRID_W
        assert length % (NA_Q_ROWS * GRID_W) == 0 and rows >= 3 * NA_Q_ROWS
        assert tok // SUPER == (tok + length - 1) // SUPER
        row0 = (tok % SUPER) // GRID_W
        for qb in range(rows // NA_Q_ROWS):
            ks.append(row0 + int(np.clip(qb * NA_Q_ROWS - ROW_WIN // 2, 0, rows - NA_K_ROWS)))
            ty.append(0 if qb == 0 else (2 if qb == rows // NA_Q_ROWS - 1 else 1))
        tok += length
    assert tok % SUPER == 0 and len(ks) == (tok // SUPER) * (per_super // NA_Q_ROWS)
    return np.asarray(ks, np.int32), np.asarray(ty, np.int32)


def _na_bias(rpb):
    heads = rpb.shape[0]
    cols = np.arange(GRID_W)
    c_start = np.clip(cols - COL_WIN // 2, 0, GRID_W - COL_WIN)
    in_win = (cols[None, :] >= c_start[:, None]) & (cols[None, :] < c_start[:, None] + COL_WIN)
    dc = np.clip(cols[None, :] - cols[:, None], -(COL_WIN - 1), COL_WIN - 1) + COL_WIN - 1
    col_bias = jnp.where(jnp.asarray(in_win), rpb.astype(F32)[:, :, dc], NEG_INF)
    i = np.arange(NA_Q_ROWS)[:, None]
    j = np.arange(NA_K_ROWS)[None, :]
    half = ROW_WIN // 2
    q_rel = [i, i + half, i + ROW_WIN]
    w_rel = [np.maximum(i - half, 0), i, np.minimum(i + half, ROW_WIN)]
    out = []
    for qr, wr in zip(q_rel, w_rel):
        valid = (j >= wr) & (j < wr + ROW_WIN)
        dr = np.clip(j - qr + ROW_WIN - 1, 0, 2 * ROW_WIN - 2)
        b = jnp.where(jnp.asarray(valid)[None, :, :, None, None], col_bias[:, dr], NEG_INF)
        out.append(b.transpose(0, 1, 3, 2, 4).reshape(heads, NA_Q_ROWS * GRID_W, NA_K_ROWS * GRID_W))
    return jnp.stack(out, axis=1)


def neighbourhood_attention(qkv, rpb, seq_lens):
    heads = rpb.shape[0]
    n = qkv.shape[1]
    nsup = n // SUPER
    tq = NA_Q_ROWS * GRID_W
    nqb = SUPER // tq
    ks_tab, ty_tab = _na_tables(seq_lens)
    bias = _na_bias(rpb)
    grid_spec = pltpu.PrefetchScalarGridSpec(
        num_scalar_prefetch=2,
        grid=(heads, nsup, nqb),
        in_specs=[
            pl.BlockSpec((1, tq, LANE), lambda h, s, b, ks, ty: (h, s * nqb + b, 0)),
            pl.BlockSpec((1, SUPER, LANE), lambda h, s, b, ks, ty: (heads + h, s, 0)),
            pl.BlockSpec((1, SUPER, LANE), lambda h, s, b, ks, ty: (2 * heads + h, s, 0)),
            pl.BlockSpec((1, 1, tq, NA_K_ROWS * GRID_W), lambda h, s, b, ks, ty: (h, ty[s * nqb + b], 0, 0)),
        ],
        out_specs=pl.BlockSpec((tq, LANE), lambda h, s, b, ks, ty: (s * nqb + b, h)),
    )
    return pl.pallas_call(
        functools.partial(_na_kernel, nqb=nqb, scale=NA_HEAD_DIM ** -0.5),
        grid_spec=grid_spec,
        out_shape=jax.ShapeDtypeStruct((n, heads * LANE), BF16),
        compiler_params=_cparams("parallel", "arbitrary", "arbitrary"),
        name="neighbourhood_attention",
    )(jnp.asarray(ks_tab), jnp.asarray(ty_tab), qkv, qkv, qkv, bias)


def _mem_attn_kernel(seq_ref, q_ref, kv_ref, o_ref, *, scale):
    del seq_ref
    hd = 2 * LANE
    for h in range(MEM_HEADS):
        q = jnp.concatenate([q_ref[2 * h], q_ref[2 * h + 1]], axis=-1).astype(BF16)
        k = jnp.concatenate([kv_ref[2 * h], kv_ref[2 * h + 1]], axis=-1)
        v = jnp.concatenate([kv_ref[2 * MEM_HEADS + 2 * h], kv_ref[2 * MEM_HEADS + 2 * h + 1]], axis=-1)
        s = lax.dot_general(q, k, (((1,), (1,)), ((), ())), preferred_element_type=F32) * scale
        m = jnp.max(s, axis=-1, keepdims=True)
        p = jnp.exp(s - m)
        l = jnp.sum(p, axis=-1, keepdims=True)
        o = jnp.dot(p.astype(BF16), v, preferred_element_type=F32) / l
        o_ref[:, h * hd:(h + 1) * hd] = o.astype(o_ref.dtype)


def memory_attention(proj, q_slab0, kv, blk_seq, n_mem, tq=512):
    n = proj.shape[1]
    nq = 2 * MEM_HEADS
    assert q_slab0 % nq == 0
    grid_spec = pltpu.PrefetchScalarGridSpec(
        num_scalar_prefetch=1,
        grid=(n // tq,),
        in_specs=[pl.BlockSpec((nq, tq, LANE), lambda i, sq: (q_slab0 // nq, i, 0)),
                  pl.BlockSpec((2 * nq, n_mem, LANE), lambda i, sq: (0, sq[i], 0))],
        out_specs=pl.BlockSpec((tq, nq * LANE), lambda i, sq: (i, 0)),
    )
    return pl.pallas_call(
        functools.partial(_mem_attn_kernel, scale=(2 * LANE) ** -0.5),
        grid_spec=grid_spec,
        out_shape=jax.ShapeDtypeStruct((n, nq * LANE), BF16),
        compiler_params=_cparams("parallel"),
        name="memory_attention",
    )(blk_seq, proj, kv)


def _s5_param_kernel(are_ref, aim_ref, ldt_ref, bre_ref, bim_ref, lre_ref, lim_ref, bbre_ref, bbim_ref):
    a_re = are_ref[...]
    a_im = aim_ref[...]
    dt = jnp.exp(ldt_ref[...])
    mag = jnp.exp(a_re * dt)
    ang = a_im * dt
    lam_re = mag * jnp.cos(ang)
    lam_im = mag * jnp.sin(ang)
    den = a_re * a_re + a_im * a_im
    n_re = lam_re - 1.0
    coef_re = (n_re * a_re + lam_im * a_im) / den
    coef_im = (lam_im * a_re - n_re * a_im) / den
    lre_ref[...] = lam_re
    lim_ref[...] = lam_im
    for c in range(bre_ref.shape[0]):
        b_re = bre_ref[c]
        b_im = bim_ref[c]
        bbre_ref[c] = coef_re * b_re - coef_im * b_im
        bbim_ref[c] = coef_re * b_im + coef_im * b_re


def s5_discretise(a_re, a_im, log_dt, b_re, b_im):
    two, g, p = a_re.shape
    c = b_re.shape[-1]
    rows = two * g
    ldt = jnp.broadcast_to(log_dt.reshape(rows, 1), (rows, p))
    b_re_t = b_re.reshape(rows, p, c).transpose(2, 0, 1)
    b_im_t = b_im.reshape(rows, p, c).transpose(2, 0, 1)
    return pl.pallas_call(
        _s5_param_kernel,
        out_shape=(jax.ShapeDtypeStruct((rows, p), F32), jax.ShapeDtypeStruct((rows, p), F32),
                   jax.ShapeDtypeStruct((c, rows, p), F32), jax.ShapeDtypeStruct((c, rows, p), F32)),
        name="s5_discretise",
    )(a_re.reshape(rows, p), a_im.reshape(rows, p), ldt, b_re_t, b_im_t)


def _s5_operands(lam_re, lam_im, bb_re, bb_im, c_re, c_im, d):
    c, rows, p = bb_re.shape
    g = rows // 2
    gs = LANE // c
    ns = g // gs
    eye = jnp.eye(gs, dtype=F32)

    def bmat(bb):
        x = bb.reshape(c, 2, ns, gs, p).transpose(1, 2, 3, 0, 4)
        return jnp.einsum('dsgcp,gh->dsgchp', x, eye).reshape(2, ns, gs * c, gs * p)

    def cmat(cc):
        x = cc.astype(F32).reshape(2, ns, gs, c, p)
        return jnp.einsum('dsgcp,gh->dsgphc', x, eye).reshape(2, ns, gs * p, gs * c)

    b_op = jnp.concatenate([bmat(bb_re), bmat(bb_im)], axis=-1).astype(BF16)
    c_op = jnp.concatenate([cmat(c_re), -cmat(c_im)], axis=-2).astype(BF16)
    lam = jnp.concatenate([lam_re.reshape(2, ns, gs * p), lam_im.reshape(2, ns, gs * p)], axis=-1)
    lam = jnp.broadcast_to(lam[:, :, None, :], (2, ns, SUBLANE, 2 * gs * p))
    d_op = d.astype(F32).reshape(ns, 1, gs * c)
    return b_op, c_op, lam, d_op


def _s5_kernel(keep_ref, u_ref, bm_ref, cm_ref, lam_ref, d_ref, o_ref,
               up_scr, y_scr, y2_scr, bu0_scr, bu1_scr, xs0_scr, xs1_scr, e_scr, i_scr, *, lseg, tblk):
    sup = pl.program_id(0)
    nblk = lseg // tblk
    rb = SUBLANE * tblk
    ns = bu0_scr.shape[1] // 2
    nchunk = ns // LANE
    u2 = u_ref.at[0]
    d_row = d_ref[0]

    def permute_in(t2, carry):
        t = t2 * 2
        a = u2[pl.ds(t, SUBLANE, stride=lseg), :]
        b = u2[pl.ds(t + 1, SUBLANE, stride=lseg), :]
        ab = jnp.concatenate([a, b], axis=0)
        rows = pl.ds(pl.multiple_of(t * SUBLANE, 2 * SUBLANE), 2 * SUBLANE)
        up_scr[rows, :] = ab.astype(BF16)
        y_scr[rows, :] = ab * d_row
        return carry
    lax.fori_loop(0, lseg // 2, permute_in, 0, unroll=4)
    xs1_scr[...] = jnp.zeros(xs1_scr.shape, xs1_scr.dtype)

    for direction in range(2):
        rev = direction == 1
        lam_r = [lam_ref[direction, 0, :, c * LANE:(c + 1) * LANE] for c in range(nchunk)]
        lam_i = [lam_ref[direction, 0, :, ns + c * LANE:ns + (c + 1) * LANE] for c in range(nchunk)]
        steps = list(range(tblk))[::-1] if rev else list(range(tblk))

        def block_rows(k, rev=rev):
            k = jnp.clip(k, 0, nblk - 1)
            blk = (nblk - 1 - k) if rev else k
            return pl.ds(pl.multiple_of(blk * rb, rb), rb)

        def bu_dot(k, direction=direction, block_rows=block_rows):
            return jnp.dot(up_scr[block_rows(k), :], bm_ref[direction, 0], preferred_element_type=F32)

        def scan_block(bu, x, xs, lam_r=lam_r, lam_i=lam_i, steps=steps):
            xr, xi = list(x[:nchunk]), list(x[nchunk:])
            held = None
            for n, t in enumerate(steps):
                rows = slice(t * SUBLANE, (t + 1) * SUBLANE)
                for c in range(nchunk):
                    b_r = bu[rows, c * LANE:(c + 1) * LANE]
                    b_i = bu[rows, ns + c * LANE:ns + (c + 1) * LANE]
                    r = lam_r[c] * xr[c] - lam_i[c] * xi[c] + b_r
                    i = lam_r[c] * xi[c] + lam_i[c] * xr[c] + b_i
                    xr[c], xi[c] = r, i
                if xs is None:
                    continue
                if n % 2 == 0:
                    held = (t, list(xr), list(xi))
                    continue
                lo, hi = (held, (t, xr, xi)) if held[0] < t else ((t, xr, xi), held)
                pair = slice(lo[0] * SUBLANE, (lo[0] + 2) * SUBLANE)
                for c in range(nchunk):
                    xs[pair, c * LANE:(c + 1) * LANE] = jnp.concatenate([lo[1][c], hi[1][c]], axis=0).astype(BF16)
                    xs[pair, ns + c * LANE:ns + (c + 1) * LANE] = (
                        jnp.concatenate([lo[2][c], hi[2][c]], axis=0).astype(BF16))
            return tuple(xr + xi)

        def pass_a(k2, x, bu_dot=bu_dot, scan_block=scan_block):
            bu1_scr[...] = bu_dot(2 * k2 + 1)
            x = scan_block(bu0_scr, x, None)
            bu0_scr[...] = bu_dot(2 * k2 + 2)
            return scan_block(bu1_scr, x, None)
        bu0_scr[...] = bu_dot(0)
        zero = tuple(jnp.zeros((SUBLANE, LANE), F32) for _ in range(2 * nchunk))
        ends = lax.fori_loop(0, nblk // 2, pass_a, zero)
        for c in range(2 * nchunk):
            e_scr[:, c * LANE:(c + 1) * LANE] = ends[c]

        p_r = lam_ref[direction, 0, 0:1, 0:ns]
        p_i = lam_ref[direction, 0, 0:1, ns:2 * ns]
        for _ in range(int(math.log2(lseg))):
            p_r, p_i = p_r * p_r - p_i * p_i, 2.0 * p_r * p_i
        order = list(range(S5_SEGMENTS))[::-1] if rev else list(range(S5_SEGMENTS))
        s_r = jnp.zeros((1, ns), F32)
        s_i = jnp.zeros((1, ns), F32)
        i_scr[order[0]:order[0] + 1, :] = jnp.zeros((1, 2 * ns), F32)
        for prev, cur in zip(order[:-1], order[1:]):
            e_r = e_scr[prev:prev + 1, 0:ns]
            e_i = e_scr[prev:prev + 1, ns:2 * ns]
            keep = keep_ref[(sup * 2 + direction) * S5_SEGMENTS + cur].astype(F32)
            n_r = (p_r * s_r - p_i * s_i + e_r) * keep
            n_i = (p_r * s_i + p_i * s_r + e_i) * keep
            i_scr[cur:cur + 1, 0:ns] = n_r
            i_scr[cur:cur + 1, ns:2 * ns] = n_i
            s_r, s_i = n_r, n_i

        def project(k, xs, scale, direction=direction, block_rows=block_rows):
            rows = block_rows(k)
            yb = jnp.dot(xs[...], cm_ref[direction, 0], preferred_element_type=F32)
            y_scr[rows, :] = y_scr[rows, :] + yb * scale

        def pass_b(k2, x, bu_dot=bu_dot, scan_block=scan_block, project=project):
            bu1_scr[...] = bu_dot(2 * k2 + 1)
            x = scan_block(bu0_scr, x, xs0_scr)
            project(2 * k2 - 1, xs1_scr, jnp.where(k2 > 0, 1.0, 0.0))
            bu0_scr[...] = bu_dot(2 * k2 + 2)
            x = scan_block(bu1_scr, x, xs1_scr)
            project(2 * k2, xs0_scr, 1.0)
            return x
        bu0_scr[...] = bu_dot(0)
        start = tuple(i_scr[:, c * LANE:(c + 1) * LANE] for c in range(2 * nchunk))
        lax.fori_loop(0, nblk // 2, pass_b, start)
        project(nblk - 1, xs1_scr, 1.0)

    def permute_out(t, carry):
        y = y_scr[pl.ds(pl.multiple_of(t * SUBLANE, SUBLANE), SUBLANE), :]
        gelu = 0.5 * y * (1.0 + jnp.tanh(math.sqrt(2.0 / math.pi) * (y + 0.044715 * (y * y * y))))
        y2_scr[pl.ds(t, SUBLANE, stride=lseg), :] = gelu
        return carry
    lax.fori_loop(0, lseg, permute_out, 0, unroll=8)
    o_ref[...] = y2_scr[...].astype(o_ref.dtype)


def _s5_keep_table(seq_lens, lseg):
    starts, ends = set(), set()
    tok = 0
    for length in seq_lens:
        assert length % lseg == 0
        starts.add(tok // lseg)
        tok += length
        ends.add(tok // lseg - 1)
    nseg = tok // lseg
    assert nseg % S5_SEGMENTS == 0
    keep = np.ones((nseg // S5_SEGMENTS, 2, S5_SEGMENTS), np.int32)
    for s in range(nseg):
        if s in starts:
            keep[s // S5_SEGMENTS, 0, s % S5_SEGMENTS] = 0
        if s in ends:
            keep[s // S5_SEGMENTS, 1, s % S5_SEGMENTS] = 0
    return keep.reshape(-1)


def s5_mixer(proj, b_op, c_op, lam, d_op, seq_lens):
    n = proj.shape[1]
    ns = b_op.shape[1]
    nstate2 = b_op.shape[-1]
    nsup = n // SUPER
    lseg = SUPER // S5_SEGMENTS
    tblk = min(S5_TBLK, lseg)
    assert lseg & (lseg - 1) == 0 and lseg % (2 * tblk) == 0 and tblk % 2 == 0
    keep = _s5_keep_table(seq_lens, lseg)
    rows = SUBLANE * tblk
    grid_spec = pltpu.PrefetchScalarGridSpec(
        num_scalar_prefetch=1,
        grid=(nsup, ns),
        in_specs=[
            pl.BlockSpec((1, SUPER, LANE), lambda b, s, kp: (s, b, 0)),
            pl.BlockSpec((2, 1, LANE, nstate2), lambda b, s, kp: (0, s, 0, 0)),
            pl.BlockSpec((2, 1, nstate2, LANE), lambda b, s, kp: (0, s, 0, 0)),
            pl.BlockSpec((2, 1, SUBLANE, nstate2), lambda b, s, kp: (0, s, 0, 0)),
            pl.BlockSpec((1, 1, LANE), lambda b, s, kp: (s, 0, 0)),
        ],
        out_specs=pl.BlockSpec((SUPER, LANE), lambda b, s, kp: (b, s)),
        scratch_shapes=[
            pltpu.VMEM((SUPER, LANE), BF16),
            pltpu.VMEM((SUPER, LANE), F32),
            pltpu.VMEM((SUPER, LANE), F32),
            pltpu.VMEM((rows, nstate2), F32),
            pltpu.VMEM((rows, nstate2), F32),
            pltpu.VMEM((rows, nstate2), BF16),
            pltpu.VMEM((rows, nstate2), BF16),
            pltpu.VMEM((SUBLANE, nstate2), F32),
            pltpu.VMEM((SUBLANE, nstate2), F32),
        ],
    )
    return pl.pallas_call(
        functools.partial(_s5_kernel, lseg=lseg, tblk=tblk),
        grid_spec=grid_spec,
        out_shape=jax.ShapeDtypeStruct((n, ns * LANE), BF16),
        compiler_params=_cparams("parallel", "arbitrary"),
        name="s5_mixer",
    )(jnp.asarray(keep), proj, b_op, c_op, lam, d_op)


def _glu_kernel(g_ref, w_ref, b_ref, o_ref):
    tn = o_ref.shape[1]
    col = pl.multiple_of(pl.program_id(1) * tn, tn)
    acc = jnp.dot(g_ref[...], w_ref[...], preferred_element_type=F32) + b_ref[...]
    g = g_ref[:, pl.ds(col, tn)].astype(F32)
    o_ref[...] = (g * _sigmoid(acc)).astype(o_ref.dtype)


def glu(g, w, b, tm=1024, tn=512):
    m, k = g.shape
    assert m % tm == 0 and k % tn == 0
    return pl.pallas_call(
        _glu_kernel,
        grid=(m // tm, k // tn),
        in_specs=[pl.BlockSpec((tm, k), lambda i, j: (i, 0)),
                  pl.BlockSpec((k, tn), lambda i, j: (0, j)),
                  pl.BlockSpec((1, tn), lambda i, j: (0, j))],
        out_specs=pl.BlockSpec((tm, tn), lambda i, j: (i, j)),
        out_shape=jax.ShapeDtypeStruct((m, k), BF16),
        compiler_params=_cparams("parallel", "arbitrary"),
        name="glu",
    )(g, w, b.reshape(1, k).astype(F32))


def _out_proj_kernel(x_ref, tok_ref, mo_ref, w1_ref, w2_ref, o_ref):
    acc = jnp.dot(tok_ref[...], w1_ref[...], preferred_element_type=F32)
    acc = acc + jnp.dot(mo_ref[...], w2_ref[...], preferred_element_type=F32)
    o_ref[...] = x_ref[...] + acc


def out_proj(x, tok, mo, w, tm=1024, tn=512):
    m, d = x.shape
    k1, k2 = tok.shape[1], mo.shape[1]
    assert m % tm == 0 and d % tn == 0 and w.shape[0] == k1 + k2 and k1 % k2 == 0
    return pl.pallas_call(
        _out_proj_kernel,
        grid=(m // tm, d // tn),
        in_specs=[pl.BlockSpec((tm, tn), lambda i, j: (i, j)),
                  pl.BlockSpec((tm, k1), lambda i, j: (i, 0)),
                  pl.BlockSpec((tm, k2), lambda i, j: (i, 0)),
                  pl.BlockSpec((k1, tn), lambda i, j: (0, j)),
                  pl.BlockSpec((k2, tn), lambda i, j: (k1 // k2, j))],
        out_specs=pl.BlockSpec((tm, tn), lambda i, j: (i, j)),
        out_shape=jax.ShapeDtypeStruct((m, d), F32),
        compiler_params=_cparams("parallel", "arbitrary"),
        name="out_proj",
    )(x, tok, mo, w, w)


def _router_kernel(x_ref, g_ref, w_ref, b_ref, h_ref, meta_ref):
    t = _rms_norm(x_ref[...], g_ref[...])
    h_ref[...] = _pack_bf16_pairs(t)
    logits = jnp.dot(t, w_ref[...], preferred_element_type=F32, precision=lax.Precision.HIGHEST) + b_ref[...]
    lane = lax.broadcasted_iota(jnp.int32, logits.shape, 1)
    big = jnp.int32(LANE)
    is_g = lane < N_EXPERT_GROUPS
    gl = jnp.where(is_g, logits, -jnp.inf)
    gmax = jnp.max(gl, axis=-1, keepdims=True)
    gsum = jnp.sum(jnp.where(is_g, jnp.exp(gl - gmax), 0.0), axis=-1, keepdims=True)
    g_val = 1.0 / gsum
    g_idx = jnp.min(jnp.where(gl == gmax, lane, big), axis=-1, keepdims=True)
    e_lane = lane - N_EXPERT_GROUPS
    in_grp = (e_lane >= 0) & (e_lane < N_EXPERTS) & ((e_lane >> 2) == g_idx)
    el = jnp.where(in_grp, logits, -jnp.inf)
    v1 = jnp.max(el, axis=-1, keepdims=True)
    i1 = jnp.min(jnp.where(el == v1, lane, big), axis=-1, keepdims=True)
    el2 = jnp.where(lane == i1, -jnp.inf, el)
    v2 = jnp.max(el2, axis=-1, keepdims=True)
    i2 = jnp.min(jnp.where(el2 == v2, lane, big), axis=-1, keepdims=True)
    z = jnp.exp(v2 - v1)
    w1 = g_val / (1.0 + z)
    w2 = g_val * z / (1.0 + z)
    e1 = (i1 - N_EXPERT_GROUPS).astype(F32)
    e2 = (i2 - N_EXPERT_GROUPS).astype(F32)
    meta = jnp.where(lane == 0, w1, jnp.where(lane == 1, w2, jnp.where(lane == 2, e1, jnp.where(lane == 3, e2, 0.0))))
    meta_ref[...] = meta


def router(x, gain, w_rg, b_rg, w_re, b_re, tm=512):
    m, d = x.shape
    assert EXPERTS_PER_GROUP == 4 and m % tm == 0
    pad = LANE - N_EXPERT_GROUPS - N_EXPERTS
    w = jnp.concatenate([w_rg, w_re, jnp.zeros((d, pad), F32)], axis=1).astype(F32)
    b = jnp.concatenate([b_rg, b_re, jnp.zeros((pad,), F32)]).astype(F32).reshape(1, LANE)
    return pl.pallas_call(
        _router_kernel,
        grid=(m // tm,),
        in_specs=[pl.BlockSpec((tm, d), lambda i: (i, 0)),
                  pl.BlockSpec((1, d), lambda i: (0, 0)),
                  pl.BlockSpec((d, LANE), lambda i: (0, 0)),
                  pl.BlockSpec((1, LANE), lambda i: (0, 0))],
        out_specs=(pl.BlockSpec((tm, d // 2), lambda i: (i, 0)),
                   pl.BlockSpec((tm, LANE), lambda i: (i, 0))),
        out_shape=(jax.ShapeDtypeStruct((m, d // 2), U32), jax.ShapeDtypeStruct((m, LANE), F32)),
        compiler_params=_cparams("parallel"),
        name="moe_router",
    )(x, gain.reshape(1, d), w, b)


def _routing_plan(meta, n_rows):
    n = meta.shape[0]
    e = meta[:, 2:4].astype(jnp.int32)
    onehot = (e[:, :, None] == jnp.arange(N_EXPERTS)[None, None, :]).any(axis=1).astype(jnp.int32)
    counts = onehot.sum(axis=0)
    rank = jnp.cumsum(onehot, axis=0) - onehot
    padded = ((counts + MOE_TILE - 1) // MOE_TILE) * MOE_TILE
    ends = jnp.cumsum(padded)
    offs = ends - padded
    pos = jnp.take_along_axis(offs[None, :] + rank, e, axis=1).astype(jnp.int32)
    tok = jnp.arange(n, dtype=jnp.int32)
    sorted_tok = jnp.zeros((n_rows,), jnp.int32).at[pos[:, 0]].set(tok).at[pos[:, 1]].set(tok)
    tile_start = jnp.arange(n_rows // MOE_TILE, dtype=jnp.int32) * MOE_TILE
    tile_expert = jnp.minimum((ends[None, :] <= tile_start[:, None]).sum(axis=1), N_EXPERTS - 1).astype(jnp.int32)
    tile_valid = (tile_start < ends[-1]).astype(jnp.int32)
    return pos[:, 0], pos[:, 1], sorted_tok, tile_expert, tile_valid


def _row_gather_start(idx_ref, base, src_ref, dst_ref, sem):
    for r in range(dst_ref.shape[0]):
        pltpu.make_async_copy(src_ref.at[pl.ds(idx_ref[base + r], 1), :], dst_ref.at[pl.ds(r, 1), :], sem).start()


def _row_gather_wait(src_ref, dst_ref, sem):
    pltpu.make_async_copy(src_ref.at[pl.ds(0, dst_ref.shape[0]), :], dst_ref, sem).wait()


def _moe_up_kernel(st_ref, te_ref, tv_ref, h_ref, wg_ref, wu_ref, o_ref, xbuf, sem):
    del te_ref
    t = pl.program_id(0)
    last = pl.num_programs(0) - 1
    slot = t % 2
    tile = xbuf.shape[1]

    @pl.when(t == 0)
    def _():
        _row_gather_start(st_ref, 0, h_ref, xbuf.at[0], sem.at[0])

    @pl.when(tv_ref[t] == 1)
    def _():
        _row_gather_wait(h_ref, xbuf.at[slot], sem.at[slot])
        nxt = jnp.minimum(t + 1, last)
        _row_gather_start(st_ref, nxt * tile, h_ref, xbuf.at[1 - slot], sem.at[1 - slot])
        lo, hi = _unpack_bf16_pairs(xbuf[slot])
        lo = lo.astype(BF16)
        hi = hi.astype(BF16)
        half = lo.shape[1]
        g = jnp.dot(lo, wg_ref[0, :half], preferred_element_type=F32)
        g = g + jnp.dot(hi, wg_ref[0, half:], preferred_element_type=F32)
        u = jnp.dot(lo, wu_ref[0, :half], preferred_element_type=F32)
        u = u + jnp.dot(hi, wu_ref[0, half:], preferred_element_type=F32)
        o_ref[...] = (g * _sigmoid(g) * u).astype(o_ref.dtype)

        @pl.when(t == last)
        def _():
            _row_gather_wait(h_ref, xbuf.at[1 - slot], sem.at[1 - slot])

    @pl.when(tv_ref[t] == 0)
    def _():
        @pl.when(tv_ref[jnp.maximum(t - 1, 0)] == 1)
        def _():
            _row_gather_wait(h_ref, xbuf.at[slot], sem.at[slot])
        o_ref[...] = jnp.zeros(o_ref.shape, o_ref.dtype)


def moe_up(h, w_gate, w_up, sorted_tok, tile_expert, tile_valid):
    half = h.shape[1]
    d = 2 * half
    f = w_gate.shape[-1]
    n_tiles = tile_expert.shape[0]
    grid_spec = pltpu.PrefetchScalarGridSpec(
        num_scalar_prefetch=3,
        grid=(n_tiles,),
        in_specs=[pl.BlockSpec(memory_space=pl.ANY),
                  pl.BlockSpec((1, d, f), lambda t, st, te, tv: (te[t], 0, 0)),
                  pl.BlockSpec((1, d, f), lambda t, st, te, tv: (te[t], 0, 0))],
        out_specs=pl.BlockSpec((MOE_TILE, f), lambda t, st, te, tv: (t, 0)),
        scratch_shapes=[pltpu.VMEM((2, MOE_TILE, half), U32), pltpu.SemaphoreType.DMA((2,))],
    )
    return pl.pallas_call(
        _moe_up_kernel,
        grid_spec=grid_spec,
        out_shape=jax.ShapeDtypeStruct((n_tiles * MOE_TILE, f), BF16),
        compiler_params=_cparams("arbitrary"),
        name="moe_up",
    )(sorted_tok, tile_expert, tile_valid, h, w_gate, w_up)


def _moe_down_kernel(te_ref, tv_ref, h_ref, wd_ref, o_ref):
    del te_ref
    t = pl.program_id(0)

    @pl.when(tv_ref[t] == 1)
    def _():
        y = jnp.dot(h_ref[...], wd_ref[0], preferred_element_type=F32)
        o_ref[...] = _pack_bf16_pairs(y)

    @pl.when(tv_ref[t] == 0)
    def _():
        o_ref[...] = jnp.zeros(o_ref.shape, o_ref.dtype)


def moe_down(hid, w_down, tile_expert, tile_valid):
    n_rows, f = hid.shape
    d = w_down.shape[-1]
    grid_spec = pltpu.PrefetchScalarGridSpec(
        num_scalar_prefetch=2,
        grid=(n_rows // MOE_TILE,),
        in_specs=[pl.BlockSpec((MOE_TILE, f), lambda t, te, tv: (t, 0)),
                  pl.BlockSpec((1, f, d), lambda t, te, tv: (te[t], 0, 0))],
        out_specs=pl.BlockSpec((MOE_TILE, d // 2), lambda t, te, tv: (t, 0)),
    )
    return pl.pallas_call(
        _moe_down_kernel,
        grid_spec=grid_spec,
        out_shape=jax.ShapeDtypeStruct((n_rows, d // 2), U32),
        compiler_params=_cparams("arbitrary"),
        name="moe_down",
    )(tile_expert, tile_valid, hid, w_down)


def _combine_kernel(p0_ref, p1_ref, x_ref, meta_ref, g_ref, ys_ref, o_ref, buf, sem, *, base, final_norm):
    i = pl.program_id(0)
    last = pl.num_programs(0) - 1
    slot = i % 2
    tc = buf.shape[2]

    def start(tile, slot):
        n0 = base + tile * tc
        _row_gather_start(p0_ref, n0, ys_ref, buf.at[slot, 0], sem.at[slot, 0])
        _row_gather_start(p1_ref, n0, ys_ref, buf.at[slot, 1], sem.at[slot, 1])

    def wait(slot):
        _row_gather_wait(ys_ref, buf.at[slot, 0], sem.at[slot, 0])
        _row_gather_wait(ys_ref, buf.at[slot, 1], sem.at[slot, 1])

    @pl.when(i == 0)
    def _():
        start(0, 0)

    wait(slot)
    start(jnp.minimum(i + 1, last), 1 - slot)
    meta = meta_ref[...]
    w0 = meta[:, 0:1]
    w1 = meta[:, 1:2]
    lo0, hi0 = _unpack_bf16_pairs(buf[slot, 0])
    lo1, hi1 = _unpack_bf16_pairs(buf[slot, 1])
    half = lo0.shape[1]
    y_lo = x_ref[:, :half] + w0 * lo0 + w1 * lo1
    y_hi = x_ref[:, half:] + w0 * hi0 + w1 * hi1
    if final_norm:
        ms = (jnp.sum(y_lo * y_lo, axis=-1, keepdims=True) + jnp.sum(y_hi * y_hi, axis=-1, keepdims=True)) / (2 * half)
        inv = lax.rsqrt(ms + RMS_EPS)
        y_lo = y_lo * inv * g_ref[:, :half]
        y_hi = y_hi * inv * g_ref[:, half:]
    o_ref[:, :half] = y_lo
    o_ref[:, half:] = y_hi

    @pl.when(i == last)
    def _():
        wait(1 - slot)


def moe_combine(x, meta, ys, pos0, pos1, gain, *, base, count, final_norm, tc=256):
    d = x.shape[1]
    assert base % tc == 0 and count % tc == 0
    blk0 = base // tc
    grid_spec = pltpu.PrefetchScalarGridSpec(
        num_scalar_prefetch=2,
        grid=(count // tc,),
        in_specs=[pl.BlockSpec((tc, d), lambda i, p0, p1: (blk0 + i, 0)),
                  pl.BlockSpec((tc, LANE), lambda i, p0, p1: (blk0 + i, 0)),
                  pl.BlockSpec((1, d), lambda i, p0, p1: (0, 0)),
                  pl.BlockSpec(memory_space=pl.ANY)],
        out_specs=pl.BlockSpec((tc, d), lambda i, p0, p1: (i, 0)),
        scratch_shapes=[pltpu.VMEM((2, 2, tc, d // 2), U32), pltpu.SemaphoreType.DMA((2, 2))],
    )
    return pl.pallas_call(
        functools.partial(_combine_kernel, base=base, final_norm=final_norm),
        grid_spec=grid_spec,
        out_shape=jax.ShapeDtypeStruct((count, d), F32),
        compiler_params=_cparams("arbitrary"),
        name="moe_combine",
    )(pos0, pos1, x, meta, gain.reshape(1, d), ys)


def moe_experts(x, gain, w_rg, b_rg, w_re, b_re, w_gate, w_up, w_down):
    n = x.shape[0]
    n_rows = 2 * n + N_EXPERTS * MOE_TILE
    assert (2 * n) % MOE_TILE == 0
    h, meta = router(x, gain, w_rg, b_rg, w_re, b_re)
    pos0, pos1, sorted_tok, tile_expert, tile_valid = _routing_plan(meta, n_rows)
    hid = moe_up(h, w_gate, w_up, sorted_tok, tile_expert, tile_valid)
    ys = moe_down(hid, w_down, tile_expert, tile_valid)
    return meta, ys, pos0, pos1


def kernel(x_prompt, x_sample, mem_prompt, mem_sample, norm_mix, norm_mem, norm_ffn, norm_final, na_w_in, na_rpb, s5_w_in, s5_a_re, s5_a_im, s5_log_dt, s5_b_re, s5_b_im, s5_c_re, s5_c_im, s5_d, s5_w_glu, s5_b_glu, mem_w_kv, w_out, moe_w_rg, moe_b_rg, moe_w_re, moe_b_re, moe_w_gate, moe_w_up, moe_w_down):
    d = x_prompt.shape[-1]
    groups = [x_prompt, x_sample]
    mems = [mem_prompt, mem_sample]
    seq_lens = [g.shape[1] for g in groups for _ in range(g.shape[0])]
    n_mem = mem_prompt.shape[1]
    depth = norm_mix.shape[0]
    x = jnp.concatenate([g.reshape(-1, d) for g in groups], axis=0)
    mem = jnp.concatenate([m.reshape(-1, d) for m in mems], axis=0)
    n = x.shape[0]
    tq_mem = 512
    blk_seq = []
    for s, length in enumerate(seq_lens):
        assert length % tq_mem == 0
        blk_seq += [s] * (length // tq_mem)
    blk_seq = jnp.asarray(np.asarray(blk_seq, np.int32))
    na_heads = na_rpb.shape[1]

    meta = ys = pos0 = pos1 = None
    for i in range(depth):
        j = i // 2
        if i > 0:
            x = moe_combine(x, meta, ys, pos0, pos1, norm_final, base=0, count=n, final_norm=False)
        kv = norm_proj(mem, norm_mem[i], mem_w_kv[i].astype(BF16), BF16, tm=n_mem)
        if i % 2 == 0:
            proj = norm_proj(x, norm_mix[i], na_w_in[j].astype(BF16), BF16, tm=512)
            tok = neighbourhood_attention(proj, na_rpb[j], seq_lens)
            q_slab0 = 3 * na_heads
        else:
            proj = norm_proj(x, norm_mix[i], s5_w_in[j].astype(BF16), F32, tm=512)
            lam_re, lam_im, bb_re, bb_im = s5_discretise(s5_a_re[j], s5_a_im[j], s5_log_dt[j], s5_b_re[j], s5_b_im[j])
            b_op, c_op, lam, d_op = _s5_operands(lam_re, lam_im, bb_re, bb_im, s5_c_re[j], s5_c_im[j], s5_d[j])
            g = s5_mixer(proj, b_op, c_op, lam, d_op, seq_lens)
            tok = glu(g, s5_w_glu[j].astype(BF16), s5_b_glu[j])
            q_slab0 = b_op.shape[1]
        mo = memory_attention(proj, q_slab0, kv, blk_seq, n_mem, tq=tq_mem)
        x = out_proj(x, tok, mo, w_out[i].astype(BF16))
        meta, ys, pos0, pos1 = moe_experts(x, norm_ffn[i], moe_w_rg[i], moe_b_rg[i], moe_w_re[i], moe_b_re[i],
                                           moe_w_gate[i].astype(BF16), moe_w_up[i].astype(BF16),
                                           moe_w_down[i].astype(BF16))
    outs, base = [], 0
    for g in groups:
        count = g.shape[0] * g.shape[1]
        y = moe_combine(x, meta, ys, pos0, pos1, norm_final, base=base, count=count, final_norm=True)
        outs.append(y.reshape(g.shape))
        base += count
    return tuple(outs)
```

```python
import functools
import math

import numpy as np
import jax
import jax.numpy as jnp
from jax import lax
from jax.experimental import pallas as pl
from jax.experimental.pallas import tpu as pltpu

F32 = jnp.float32
BF16 = jnp.bfloat16
U32 = jnp.uint32

LANE = 128
SUBLANE = 8
VMEM_LIMIT_BYTES = 56 * 1024 * 1024

RMS_EPS = 1e-6
NEG_INF = -1e30
GRID_W = 64
ROW_WIN = 8
COL_WIN = 16
NA_HEAD_DIM = 128
MEM_HEADS = 4
S5_GROUP = 16
S5_STATE = 64
N_EXPERT_GROUPS = 4
EXPERTS_PER_GROUP = 4
N_EXPERTS = N_EXPERT_GROUPS * EXPERTS_PER_GROUP

NA_Q_ROWS = 8
NA_K_ROWS = 2 * ROW_WIN
NA_HEADS_PER_STEP = 2
SUPER = 8192
S5_SEGMENTS = SUBLANE
S5_TBLK = 64
MOE_TILE = 256


def _cparams(*sem):
    return pltpu.CompilerParams(dimension_semantics=sem, vmem_limit_bytes=VMEM_LIMIT_BYTES)


def _rms_norm(x, gain):
    ms = jnp.mean(x * x, axis=-1, keepdims=True)
    return x * lax.rsqrt(ms + RMS_EPS) * gain


def _sigmoid(x):
    return 1.0 / (1.0 + jnp.exp(-x))


def _pack_bf16_pairs(x):
    k = x.shape[1] // 2
    bits = lax.bitcast_convert_type(x.astype(BF16).astype(F32), U32)
    return (bits[:, k:] & jnp.uint32(0xFFFF0000)) | (bits[:, :k] >> 16)


def _unpack_bf16_pairs(w):
    lo = lax.bitcast_convert_type(w << 16, F32)
    hi = lax.bitcast_convert_type(w & jnp.uint32(0xFFFF0000), F32)
    return lo, hi


def _norm_proj_kernel(x_ref, g_ref, w_ref, o_ref, h_scr):
    @pl.when(pl.program_id(1) == 0)
    def _():
        h_scr[...] = _rms_norm(x_ref[...], g_ref[...]).astype(BF16)

    acc = jnp.dot(h_scr[...], w_ref[...], preferred_element_type=F32)
    for c in range(o_ref.shape[0]):
        o_ref[c] = acc[:, c * LANE:(c + 1) * LANE].astype(o_ref.dtype)


def norm_proj(x, gain, w, out_dtype, tm, tn=1024):
    m, d = x.shape
    n_out = w.shape[1]
    assert m % tm == 0 and n_out % tn == 0 and tn % LANE == 0
    return pl.pallas_call(
        _norm_proj_kernel,
        grid=(m // tm, n_out // tn),
        in_specs=[pl.BlockSpec((tm, d), lambda i, j: (i, 0)),
                  pl.BlockSpec((1, d), lambda i, j: (0, 0)),
                  pl.BlockSpec((d, tn), lambda i, j: (0, j))],
        out_specs=pl.BlockSpec((tn // LANE, tm, LANE), lambda i, j: (j, i, 0)),
        out_shape=jax.ShapeDtypeStruct((n_out // LANE, m, LANE), out_dtype),
        scratch_shapes=[pltpu.VMEM((tm, d), BF16)],
        compiler_params=_cparams("parallel", "arbitrary"),
        name="norm_proj",
    )(x, gain.reshape(1, d), w)


def _na_kernel(ks_ref, ty_ref, q_ref, k_ref, v_ref, b_ref, o_ref, *, nqb, scale):
    step = pl.program_id(1) * nqb + pl.program_id(2)
    start = pl.multiple_of(ks_ref[step] * GRID_W, GRID_W)
    nk = NA_K_ROWS * GRID_W
    for h in range(q_ref.shape[0]):
        q = q_ref[h]
        k = k_ref[h, pl.ds(start, nk), :]
        v = v_ref[h, pl.ds(start, nk), :]
        s = lax.dot_general(q, k, (((1,), (1,)), ((), ())), preferred_element_type=F32)
        s = s * scale + b_ref[h, 0]
        m = jnp.max(s, axis=-1, keepdims=True)
        p = jnp.exp(s - m)
        l = jnp.sum(p, axis=-1, keepdims=True)
        o = jnp.dot(p.astype(BF16), v, preferred_element_type=F32) / l
        o_ref[:, h * LANE:(h + 1) * LANE] = o.astype(o_ref.dtype)


def _na_tables(seq_lens):
    ks, ty = [], []
    tok = 0
    per_super = SUPER // GRID_W
    for length in seq_lens:
        rows = length // GRID_W
        assert length % (NA_Q_ROWS * GRID_W) == 0 and rows >= 3 * NA_Q_ROWS
        assert tok // SUPER == (tok + length - 1) // SUPER
        row0 = (tok % SUPER) // GRID_W
        for qb in range(rows // NA_Q_ROWS):
            ks.append(row0 + int(np.clip(qb * NA_Q_ROWS - ROW_WIN // 2, 0, rows - NA_K_ROWS)))
            ty.append(0 if qb == 0 else (2 if qb == rows // NA_Q_ROWS - 1 else 1))
        tok += length
    assert tok % SUPER == 0 and len(ks) == (tok // SUPER) * (per_super // NA_Q_ROWS)
    return np.asarray(ks, np.int32), np.asarray(ty, np.int32)


def _na_bias(rpb):
    heads = rpb.shape[0]
    cols = np.arange(GRID_W)
    c_start = np.clip(cols - COL_WIN // 2, 0, GRID_W - COL_WIN)
    in_win = (cols[None, :] >= c_start[:, None]) & (cols[None, :] < c_start[:, None] + COL_WIN)
    dc = np.clip(cols[None, :] - cols[:, None], -(COL_WIN - 1), COL_WIN - 1) + COL_WIN - 1
    col_bias = jnp.where(jnp.asarray(in_win), rpb.astype(F32)[:, :, dc], NEG_INF)
    n_dr = 2 * ROW_WIN - 1
    masked = jnp.full((heads, 1, GRID_W, GRID_W), NEG_INF, F32)
    table = jnp.concatenate([col_bias, masked], axis=1).transpose(0, 2, 1, 3)
    i = np.arange(NA_Q_ROWS)[:, None]
    j = np.arange(NA_K_ROWS)[None, :]
    half = ROW_WIN // 2
    q_rel = [i, i + half, i + ROW_WIN]
    w_rel = [np.maximum(i - half, 0), i, np.minimum(i + half, ROW_WIN)]
    out = []
    for qr, wr in zip(q_rel, w_rel):
        valid = (j >= wr) & (j < wr + ROW_WIN)
        dr = np.where(valid, np.clip(j - qr + ROW_WIN - 1, 0, n_dr - 1), n_dr)
        rows = [jnp.take(table, dr[r], axis=2).reshape(heads, GRID_W, NA_K_ROWS * GRID_W) for r in range(NA_Q_ROWS)]
        out.append(jnp.concatenate(rows, axis=1))
    return jnp.stack(out, axis=1)


def neighbourhood_attention(qkv, rpb, seq_lens):
    heads = rpb.shape[0]
    n = qkv.shape[1]
    nsup = n // SUPER
    tq = NA_Q_ROWS * GRID_W
    nqb = SUPER // tq
    ks_tab, ty_tab = _na_tables(seq_lens)
    bias = _na_bias(rpb)
    hp = NA_HEADS_PER_STEP
    assert heads % hp == 0
    ng = heads // hp
    grid_spec = pltpu.PrefetchScalarGridSpec(
        num_scalar_prefetch=2,
        grid=(ng, nsup, nqb),
        in_specs=[
            pl.BlockSpec((hp, tq, LANE), lambda h, s, b, ks, ty: (h, s * nqb + b, 0)),
            pl.BlockSpec((hp, SUPER, LANE), lambda h, s, b, ks, ty: (ng + h, s, 0)),
            pl.BlockSpec((hp, SUPER, LANE), lambda h, s, b, ks, ty: (2 * ng + h, s, 0)),
            pl.BlockSpec((hp, 1, tq, NA_K_ROWS * GRID_W), lambda h, s, b, ks, ty: (h, ty[s * nqb + b], 0, 0)),
        ],
        out_specs=pl.BlockSpec((tq, hp * LANE), lambda h, s, b, ks, ty: (s * nqb + b, h)),
    )
    return pl.pallas_call(
        functools.partial(_na_kernel, nqb=nqb, scale=NA_HEAD_DIM ** -0.5),
        grid_spec=grid_spec,
        out_shape=jax.ShapeDtypeStruct((n, heads * LANE), BF16),
        compiler_params=_cparams("parallel", "arbitrary", "arbitrary"),
        name="neighbourhood_attention",
    )(jnp.asarray(ks_tab), jnp.asarray(ty_tab), qkv, qkv, qkv, bias)


def _mem_attn_kernel(seq_ref, q_ref, kv_ref, o_ref, *, scale):
    del seq_ref
    hd = 2 * LANE
    for h in range(MEM_HEADS):
        q = jnp.concatenate([q_ref[2 * h], q_ref[2 * h + 1]], axis=-1).astype(BF16)
        k = jnp.concatenate([kv_ref[2 * h], kv_ref[2 * h + 1]], axis=-1)
        v = jnp.concatenate([kv_ref[2 * MEM_HEADS + 2 * h], kv_ref[2 * MEM_HEADS + 2 * h + 1]], axis=-1)
        s = lax.dot_general(q, k, (((1,), (1,)), ((), ())), preferred_element_type=F32) * scale
        m = jnp.max(s, axis=-1, keepdims=True)
        p = jnp.exp(s - m)
        l = jnp.sum(p, axis=-1, keepdims=True)
        o = jnp.dot(p.astype(BF16), v, preferred_element_type=F32) / l
        o_ref[:, h * hd:(h + 1) * hd] = o.astype(o_ref.dtype)


def memory_attention(proj, q_slab0, kv, blk_seq, n_mem, tq=512):
    n = proj.shape[1]
    nq = 2 * MEM_HEADS
    assert q_slab0 % nq == 0
    grid_spec = pltpu.PrefetchScalarGridSpec(
        num_scalar_prefetch=1,
        grid=(n // tq,),
        in_specs=[pl.BlockSpec((nq, tq, LANE), lambda i, sq: (q_slab0 // nq, i, 0)),
                  pl.BlockSpec((2 * nq, n_mem, LANE), lambda i, sq: (0, sq[i], 0))],
        out_specs=pl.BlockSpec((tq, nq * LANE), lambda i, sq: (i, 0)),
    )
    return pl.pallas_call(
        functools.partial(_mem_attn_kernel, scale=(2 * LANE) ** -0.5),
        grid_spec=grid_spec,
        out_shape=jax.ShapeDtypeStruct((n, nq * LANE), BF16),
        compiler_params=_cparams("parallel"),
        name="memory_attention",
    )(blk_seq, proj, kv)


def _s5_param_kernel(are_ref, aim_ref, ldt_ref, bre_ref, bim_ref, lre_ref, lim_ref, bbre_ref, bbim_ref):
    a_re = are_ref[...]
    a_im = aim_ref[...]
    dt = jnp.exp(ldt_ref[...])
    mag = jnp.exp(a_re * dt)
    ang = a_im * dt
    lam_re = mag * jnp.cos(ang)
    lam_im = mag * jnp.sin(ang)
    den = a_re * a_re + a_im * a_im
    n_re = lam_re - 1.0
    coef_re = (n_re * a_re + lam_im * a_im) / den
    coef_im = (lam_im * a_re - n_re * a_im) / den
    lre_ref[...] = lam_re
    lim_ref[...] = lam_im
    for c in range(bre_ref.shape[0]):
        b_re = bre_ref[c]
        b_im = bim_ref[c]
        bbre_ref[c] = coef_re * b_re - coef_im * b_im
        bbim_ref[c] = coef_re * b_im + coef_im * b_re


def s5_discretise(a_re, a_im, log_dt, b_re, b_im):
    two, g, p = a_re.shape
    c = b_re.shape[-1]
    rows = two * g
    ldt = jnp.broadcast_to(log_dt.reshape(rows, 1), (rows, p))
    b_re_t = b_re.reshape(rows, p, c).transpose(2, 0, 1)
    b_im_t = b_im.reshape(rows, p, c).transpose(2, 0, 1)
    return pl.pallas_call(
        _s5_param_kernel,
        out_shape=(jax.ShapeDtypeStruct((rows, p), F32), jax.ShapeDtypeStruct((rows, p), F32),
                   jax.ShapeDtypeStruct((c, rows, p), F32), jax.ShapeDtypeStruct((c, rows, p), F32)),
        name="s5_discretise",
    )(a_re.reshape(rows, p), a_im.reshape(rows, p), ldt, b_re_t, b_im_t)


def _s5_operands(lam_re, lam_im, bb_re, bb_im, c_re, c_im, d):
    c, rows, p = bb_re.shape
    g = rows // 2
    gs = LANE // c
    ns = g // gs
    eye = jnp.eye(gs, dtype=F32)

    def bmat(bb):
        x = bb.reshape(c, 2, ns, gs, p).transpose(1, 2, 3, 0, 4)
        return jnp.einsum('dsgcp,gh->dsgchp', x, eye).reshape(2, ns, gs * c, gs * p)

    def cmat(cc):
        x = cc.astype(F32).reshape(2, ns, gs, c, p)
        return jnp.einsum('dsgcp,gh->dsgphc', x, eye).reshape(2, ns, gs * p, gs * c)

    b_op = jnp.concatenate([bmat(bb_re), bmat(bb_im)], axis=-1).astype(BF16)
    c_op = jnp.concatenate([cmat(c_re), -cmat(c_im)], axis=-2).astype(BF16)
    lam = jnp.concatenate([lam_re.reshape(2, ns, gs * p), lam_im.reshape(2, ns, gs * p)], axis=-1)
    lam = jnp.broadcast_to(lam[:, :, None, :], (2, ns, SUBLANE, 2 * gs * p))
    d_op = d.astype(F32).reshape(ns, 1, gs * c)
    return b_op, c_op, lam, d_op


def _s5_kernel(keep_ref, u_hbm, bm_ref, cm_ref, lam_ref, d_ref, o_ref,
               u_scr, u_sem, up_scr, y_scr, y2_scr, bu0_scr, bu1_scr, xs0_scr, xs1_scr, e_scr, i_scr, *, lseg, tblk):
    sup = pl.program_id(0)
    nblk = lseg // tblk
    rb = SUBLANE * tblk
    ns = bu0_scr.shape[1] // 2
    nchunk = ns // LANE
    d_row = d_ref[0]
    pitch = u_scr.shape[1] // S5_SEGMENTS
    n_slab = pl.num_programs(1)
    lin = sup * n_slab + pl.program_id(1)
    slot = lin % 2

    def segment_copies(step, slot):
        b = step // n_slab
        s = step % n_slab
        return [pltpu.make_async_copy(u_hbm.at[s, pl.ds(b * (S5_SEGMENTS * lseg) + r * lseg, lseg), :],
                                      u_scr.at[slot, pl.ds(r * pitch, lseg), :], u_sem.at[slot])
                for r in range(S5_SEGMENTS)]

    @pl.when(lin == 0)
    def _():
        for cp in segment_copies(lin, slot):
            cp.start()

    for cp in segment_copies(lin, slot):
        cp.wait()

    @pl.when(lin + 1 < pl.num_programs(0) * n_slab)
    def _():
        for cp in segment_copies(lin + 1, 1 - slot):
            cp.start()

    u2 = u_scr.at[slot]

    def permute_in(t2, carry):
        t = t2 * 2
        a = u2[pl.ds(t, SUBLANE, stride=pitch), :]
        b = u2[pl.ds(t + 1, SUBLANE, stride=pitch), :]
        ab = jnp.concatenate([a, b], axis=0)
        rows = pl.ds(pl.multiple_of(t * SUBLANE, 2 * SUBLANE), 2 * SUBLANE)
        up_scr[rows, :] = ab.astype(BF16)
        y_scr[rows, :] = ab * d_row
        return carry
    lax.fori_loop(0, lseg // 2, permute_in, 0, unroll=4)
    xs1_scr[...] = jnp.zeros(xs1_scr.shape, xs1_scr.dtype)

    for direction in range(2):
        rev = direction == 1
        lam_r = [lam_ref[direction, 0, :, c * LANE:(c + 1) * LANE] for c in range(nchunk)]
        lam_i = [lam_ref[direction, 0, :, ns + c * LANE:ns + (c + 1) * LANE] for c in range(nchunk)]
        steps = list(range(tblk))[::-1] if rev else list(range(tblk))

        def block_rows(k, rev=rev):
            k = jnp.clip(k, 0, nblk - 1)
            blk = (nblk - 1 - k) if rev else k
            return pl.ds(pl.multiple_of(blk * rb, rb), rb)

        def bu_dot(k, direction=direction, block_rows=block_rows):
            return jnp.dot(up_scr[block_rows(k), :], bm_ref[direction, 0], preferred_element_type=F32)

        def scan_block(bu, x, xs, lam_r=lam_r, lam_i=lam_i, steps=steps):
            xr, xi = list(x[:nchunk]), list(x[nchunk:])
            held = None
            for n, t in enumerate(steps):
                rows = slice(t * SUBLANE, (t + 1) * SUBLANE)
                for c in range(nchunk):
                    b_r = bu[rows, c * LANE:(c + 1) * LANE]
                    b_i = bu[rows, ns + c * LANE:ns + (c + 1) * LANE]
                    r = lam_r[c] * xr[c] - lam_i[c] * xi[c] + b_r
                    i = lam_r[c] * xi[c] + lam_i[c] * xr[c] + b_i
                    xr[c], xi[c] = r, i
                if xs is None:
                    continue
                if n % 2 == 0:
                    held = (t, list(xr), list(xi))
                    continue
                lo, hi = (held, (t, xr, xi)) if held[0] < t else ((t, xr, xi), held)
                pair = slice(lo[0] * SUBLANE, (lo[0] + 2) * SUBLANE)
                for c in range(nchunk):
                    xs[pair, c * LANE:(c + 1) * LANE] = jnp.concatenate([lo[1][c], hi[1][c]], axis=0).astype(BF16)
                    xs[pair, ns + c * LANE:ns + (c + 1) * LANE] = (
                        jnp.concatenate([lo[2][c], hi[2][c]], axis=0).astype(BF16))
            return tuple(xr + xi)

        def pass_a(k2, x, bu_dot=bu_dot, scan_block=scan_block):
            bu1_scr[...] = bu_dot(2 * k2 + 1)
            x = scan_block(bu0_scr, x, None)
            bu0_scr[...] = bu_dot(2 * k2 + 2)
            return scan_block(bu1_scr, x, None)
        bu0_scr[...] = bu_dot(0)
        zero = tuple(jnp.zeros((SUBLANE, LANE), F32) for _ in range(2 * nchunk))
        ends = lax.fori_loop(0, nblk // 2, pass_a, zero)
        for c in range(2 * nchunk):
            e_scr[:, c * LANE:(c + 1) * LANE] = ends[c]

        p_r = lam_ref[direction, 0, 0:1, 0:ns]
        p_i = lam_ref[direction, 0, 0:1, ns:2 * ns]
        for _ in range(int(math.log2(lseg))):
            p_r, p_i = p_r * p_r - p_i * p_i, 2.0 * p_r * p_i
        order = list(range(S5_SEGMENTS))[::-1] if rev else list(range(S5_SEGMENTS))
        s_r = jnp.zeros((1, ns), F32)
        s_i = jnp.zeros((1, ns), F32)
        i_scr[order[0]:order[0] + 1, :] = jnp.zeros((1, 2 * ns), F32)
        for prev, cur in zip(order[:-1], order[1:]):
            e_r = e_scr[prev:prev + 1, 0:ns]
            e_i = e_scr[prev:prev + 1, ns:2 * ns]
            keep = keep_ref[(sup * 2 + direction) * S5_SEGMENTS + cur].astype(F32)
            n_r = (p_r * s_r - p_i * s_i + e_r) * keep
            n_i = (p_r * s_i + p_i * s_r + e_i) * keep
            i_scr[cur:cur + 1, 0:ns] = n_r
            i_scr[cur:cur + 1, ns:2 * ns] = n_i
            s_r, s_i = n_r, n_i

        def project(k, xs, scale, direction=direction, block_rows=block_rows):
            rows = block_rows(k)
            yb = jnp.dot(xs[...], cm_ref[direction, 0], preferred_element_type=F32)
            y_scr[rows, :] = y_scr[rows, :] + yb * scale

        def pass_b(k2, x, bu_dot=bu_dot, scan_block=scan_block, project=project):
            bu1_scr[...] = bu_dot(2 * k2 + 1)
            x = scan_block(bu0_scr, x, xs0_scr)
            project(2 * k2 - 1, xs1_scr, jnp.where(k2 > 0, 1.0, 0.0))
            bu0_scr[...] = bu_dot(2 * k2 + 2)
            x = scan_block(bu1_scr, x, xs1_scr)
            project(2 * k2, xs0_scr, 1.0)
            return x
        bu0_scr[...] = bu_dot(0)
        start = tuple(i_scr[:, c * LANE:(c + 1) * LANE] for c in range(2 * nchunk))
        lax.fori_loop(0, nblk // 2, pass_b, start)
        project(nblk - 1, xs1_scr, 1.0)

    def permute_out(t, carry):
        y = y_scr[pl.ds(pl.multiple_of(t * SUBLANE, SUBLANE), SUBLANE), :]
        gelu = 0.5 * y * (1.0 + jnp.tanh(math.sqrt(2.0 / math.pi) * (y + 0.044715 * (y * y * y))))
        y2_scr[pl.ds(t, SUBLANE, stride=pitch), :] = gelu
        return carry
    lax.fori_loop(0, lseg, permute_out, 0, unroll=8)
    for r in range(S5_SEGMENTS):
        o_ref[r * lseg:(r + 1) * lseg, :] = y2_scr[r * pitch:r * pitch + lseg, :].astype(o_ref.dtype)


def _s5_keep_table(seq_lens, lseg):
    starts, ends = set(), set()
    tok = 0
    for length in seq_lens:
        assert length % lseg == 0
        starts.add(tok // lseg)
        tok += length
        ends.add(tok // lseg - 1)
    nseg = tok // lseg
    assert nseg % S5_SEGMENTS == 0
    keep = np.ones((nseg // S5_SEGMENTS, 2, S5_SEGMENTS), np.int32)
    for s in range(nseg):
        if s in starts:
            keep[s // S5_SEGMENTS, 0, s % S5_SEGMENTS] = 0
        if s in ends:
            keep[s // S5_SEGMENTS, 1, s % S5_SEGMENTS] = 0
    return keep.reshape(-1)


def s5_mixer(proj, b_op, c_op, lam, d_op, seq_lens):
    n = proj.shape[1]
    ns = b_op.shape[1]
    nstate2 = b_op.shape[-1]
    nsup = n // SUPER
    lseg = SUPER // S5_SEGMENTS
    tblk = min(S5_TBLK, lseg)
    assert lseg & (lseg - 1) == 0 and lseg % (2 * tblk) == 0 and tblk % 2 == 0
    keep = _s5_keep_table(seq_lens, lseg)
    rows = SUBLANE * tblk
    pitch = lseg + SUBLANE
    grid_spec = pltpu.PrefetchScalarGridSpec(
        num_scalar_prefetch=1,
        grid=(nsup, ns),
        in_specs=[
            pl.BlockSpec(memory_space=pl.ANY),
            pl.BlockSpec((2, 1, LANE, nstate2), lambda b, s, kp: (0, s, 0, 0)),
            pl.BlockSpec((2, 1, nstate2, LANE), lambda b, s, kp: (0, s, 0, 0)),
            pl.BlockSpec((2, 1, SUBLANE, nstate2), lambda b, s, kp: (0, s, 0, 0)),
            pl.BlockSpec((1, 1, LANE), lambda b, s, kp: (s, 0, 0)),
        ],
        out_specs=pl.BlockSpec((SUPER, LANE), lambda b, s, kp: (b, s)),
        scratch_shapes=[
            pltpu.VMEM((2, S5_SEGMENTS * pitch, LANE), F32),
            pltpu.SemaphoreType.DMA((2,)),
            pltpu.VMEM((SUPER, LANE), BF16),
            pltpu.VMEM((SUPER, LANE), F32),
            pltpu.VMEM((S5_SEGMENTS * pitch, LANE), F32),
            pltpu.VMEM((rows, nstate2), F32),
            pltpu.VMEM((rows, nstate2), F32),
            pltpu.VMEM((rows, nstate2), BF16),
            pltpu.VMEM((rows, nstate2), BF16),
            pltpu.VMEM((SUBLANE, nstate2), F32),
            pltpu.VMEM((SUBLANE, nstate2), F32),
        ],
    )
    return pl.pallas_call(
        functools.partial(_s5_kernel, lseg=lseg, tblk=tblk),
        grid_spec=grid_spec,
        out_shape=jax.ShapeDtypeStruct((n, ns * LANE), BF16),
        compiler_params=_cparams("arbitrary", "arbitrary"),
        name="s5_mixer",
    )(jnp.asarray(keep), proj, b_op, c_op, lam, d_op)


def _glu_kernel(g_ref, w_ref, b_ref, o_ref):
    tn = o_ref.shape[1]
    col = pl.multiple_of(pl.program_id(1) * tn, tn)
    acc = jnp.dot(g_ref[...], w_ref[...], preferred_element_type=F32) + b_ref[...]
    g = g_ref[:, pl.ds(col, tn)].astype(F32)
    o_ref[...] = (g * _sigmoid(acc)).astype(o_ref.dtype)


def glu(g, w, b, tm=1024, tn=1024):
    m, k = g.shape
    assert m % tm == 0 and k % tn == 0
    return pl.pallas_call(
        _glu_kernel,
        grid=(m // tm, k // tn),
        in_specs=[pl.BlockSpec((tm, k), lambda i, j: (i, 0)),
                  pl.BlockSpec((k, tn), lambda i, j: (0, j)),
                  pl.BlockSpec((1, tn), lambda i, j: (0, j))],
        out_specs=pl.BlockSpec((tm, tn), lambda i, j: (i, j)),
        out_shape=jax.ShapeDtypeStruct((m, k), BF16),
        compiler_params=_cparams("parallel", "arbitrary"),
        name="glu",
    )(g, w, b.reshape(1, k).astype(F32))


def _out_proj_kernel(x_ref, tok_ref, mo_ref, w1_ref, w2_ref, o_ref):
    acc = jnp.dot(tok_ref[...], w1_ref[...], preferred_element_type=F32)
    acc = acc + jnp.dot(mo_ref[...], w2_ref[...], preferred_element_type=F32)
    o_ref[...] = x_ref[...] + acc


def out_proj(x, tok, mo, w, tm=1024, tn=512):
    m, d = x.shape
    k1, k2 = tok.shape[1], mo.shape[1]
    assert m % tm == 0 and d % tn == 0 and w.shape[0] == k1 + k2 and k1 % k2 == 0
    return pl.pallas_call(
        _out_proj_kernel,
        grid=(m // tm, d // tn),
        in_specs=[pl.BlockSpec((tm, tn), lambda i, j: (i, j)),
                  pl.BlockSpec((tm, k1), lambda i, j: (i, 0)),
                  pl.BlockSpec((tm, k2), lambda i, j: (i, 0)),
                  pl.BlockSpec((k1, tn), lambda i, j: (0, j)),
                  pl.BlockSpec((k2, tn), lambda i, j: (k1 // k2, j))],
        out_specs=pl.BlockSpec((tm, tn), lambda i, j: (i, j)),
        out_shape=jax.ShapeDtypeStruct((m, d), F32),
        compiler_params=_cparams("parallel", "arbitrary"),
        name="out_proj",
    )(x, tok, mo, w, w)


def _router_kernel(x_ref, g_ref, w_ref, b_ref, h_ref, meta_ref):
    t = _rms_norm(x_ref[...], g_ref[...])
    h_ref[...] = _pack_bf16_pairs(t)
    logits = jnp.dot(t, w_ref[...], preferred_element_type=F32, precision=lax.Precision.HIGHEST) + b_ref[...]
    lane = lax.broadcasted_iota(jnp.int32, logits.shape, 1)
    big = jnp.int32(LANE)
    is_g = lane < N_EXPERT_GROUPS
    gl = jnp.where(is_g, logits, -jnp.inf)
    gmax = jnp.max(gl, axis=-1, keepdims=True)
    gsum = jnp.sum(jnp.where(is_g, jnp.exp(gl - gmax), 0.0), axis=-1, keepdims=True)
    g_val = 1.0 / gsum
    g_idx = jnp.min(jnp.where(gl == gmax, lane, big), axis=-1, keepdims=True)
    e_lane = lane - N_EXPERT_GROUPS
    in_grp = (e_lane >= 0) & (e_lane < N_EXPERTS) & ((e_lane >> 2) == g_idx)
    el = jnp.where(in_grp, logits, -jnp.inf)
    v1 = jnp.max(el, axis=-1, keepdims=True)
    i1 = jnp.min(jnp.where(el == v1, lane, big), axis=-1, keepdims=True)
    el2 = jnp.where(lane == i1, -jnp.inf, el)
    v2 = jnp.max(el2, axis=-1, keepdims=True)
    i2 = jnp.min(jnp.where(el2 == v2, lane, big), axis=-1, keepdims=True)
    z = jnp.exp(v2 - v1)
    w1 = g_val / (1.0 + z)
    w2 = g_val * z / (1.0 + z)
    e1 = (i1 - N_EXPERT_GROUPS).astype(F32)
    e2 = (i2 - N_EXPERT_GROUPS).astype(F32)
    meta = jnp.where(lane == 0, w1, jnp.where(lane == 1, w2, jnp.where(lane == 2, e1, jnp.where(lane == 3, e2, 0.0))))
    meta_ref[...] = meta


def router(x, gain, w_rg, b_rg, w_re, b_re, tm=512):
    m, d = x.shape
    assert EXPERTS_PER_GROUP == 4 and m % tm == 0
    pad = LANE - N_EXPERT_GROUPS - N_EXPERTS
    w = jnp.concatenate([w_rg, w_re, jnp.zeros((d, pad), F32)], axis=1).astype(F32)
    b = jnp.concatenate([b_rg, b_re, jnp.zeros((pad,), F32)]).astype(F32).reshape(1, LANE)
    return pl.pallas_call(
        _router_kernel,
        grid=(m // tm,),
        in_specs=[pl.BlockSpec((tm, d), lambda i: (i, 0)),
                  pl.BlockSpec((1, d), lambda i: (0, 0)),
                  pl.BlockSpec((d, LANE), lambda i: (0, 0)),
                  pl.BlockSpec((1, LANE), lambda i: (0, 0))],
        out_specs=(pl.BlockSpec((tm, d // 2), lambda i: (i, 0)),
                   pl.BlockSpec((tm, LANE), lambda i: (i, 0))),
        out_shape=(jax.ShapeDtypeStruct((m, d // 2), U32), jax.ShapeDtypeStruct((m, LANE), F32)),
        compiler_params=_cparams("parallel"),
        name="moe_router",
    )(x, gain.reshape(1, d), w, b)


def _routing_plan(meta, n_rows):
    n = meta.shape[0]
    e = meta[:, 2:4].astype(jnp.int32)
    onehot = (e[:, :, None] == jnp.arange(N_EXPERTS)[None, None, :]).any(axis=1).astype(jnp.int32)
    counts = onehot.sum(axis=0)
    rank = jnp.cumsum(onehot, axis=0) - onehot
    padded = ((counts + MOE_TILE - 1) // MOE_TILE) * MOE_TILE
    ends = jnp.cumsum(padded)
    offs = ends - padded
    pos = jnp.take_along_axis(offs[None, :] + rank, e, axis=1).astype(jnp.int32)
    tok = jnp.arange(n, dtype=jnp.int32)
    sorted_tok = jnp.zeros((n_rows,), jnp.int32).at[pos[:, 0]].set(tok).at[pos[:, 1]].set(tok)
    tile_start = jnp.arange(n_rows // MOE_TILE, dtype=jnp.int32) * MOE_TILE
    tile_expert = jnp.minimum((ends[None, :] <= tile_start[:, None]).sum(axis=1), N_EXPERTS - 1).astype(jnp.int32)
    tile_valid = (tile_start < ends[-1]).astype(jnp.int32)
    return pos[:, 0], pos[:, 1], sorted_tok, tile_expert, tile_valid


def _row_gather_start(idx_ref, base, src_ref, dst_ref, sem):
    for r in range(dst_ref.shape[0]):
        pltpu.make_async_copy(src_ref.at[pl.ds(idx_ref[base + r], 1), :], dst_ref.at[pl.ds(r, 1), :], sem).start()


def _row_gather_wait(src_ref, dst_ref, sem):
    pltpu.make_async_copy(src_ref.at[pl.ds(0, dst_ref.shape[0]), :], dst_ref, sem).wait()


def _moe_up_kernel(st_ref, te_ref, tv_ref, h_ref, wg_ref, wu_ref, o_ref, xbuf, sem):
    del te_ref
    t = pl.program_id(0)
    last = pl.num_programs(0) - 1
    slot = t % 2
    tile = xbuf.shape[1]

    @pl.when(t == 0)
    def _():
        _row_gather_start(st_ref, 0, h_ref, xbuf.at[0], sem.at[0])

    @pl.when(tv_ref[t] == 1)
    def _():
        _row_gather_wait(h_ref, xbuf.at[slot], sem.at[slot])
        nxt = jnp.minimum(t + 1, last)
        _row_gather_start(st_ref, nxt * tile, h_ref, xbuf.at[1 - slot], sem.at[1 - slot])
        lo, hi = _unpack_bf16_pairs(xbuf[slot])
        lo = lo.astype(BF16)
        hi = hi.astype(BF16)
        half = lo.shape[1]
        g = jnp.dot(lo, wg_ref[0, :half], preferred_element_type=F32)
        g = g + jnp.dot(hi, wg_ref[0, half:], preferred_element_type=F32)
        u = jnp.dot(lo, wu_ref[0, :half], preferred_element_type=F32)
        u = u + jnp.dot(hi, wu_ref[0, half:], preferred_element_type=F32)
        o_ref[...] = (g * _sigmoid(g) * u).astype(o_ref.dtype)

        @pl.when(t == last)
        def _():
            _row_gather_wait(h_ref, xbuf.at[1 - slot], sem.at[1 - slot])

    @pl.when(tv_ref[t] == 0)
    def _():
        @pl.when(tv_ref[jnp.maximum(t - 1, 0)] == 1)
        def _():
            _row_gather_wait(h_ref, xbuf.at[slot], sem.at[slot])
        o_ref[...] = jnp.zeros(o_ref.shape, o_ref.dtype)


def moe_up(h, w_gate, w_up, sorted_tok, tile_expert, tile_valid):
    half = h.shape[1]
    d = 2 * half
    f = w_gate.shape[-1]
    n_tiles = tile_expert.shape[0]
    grid_spec = pltpu.PrefetchScalarGridSpec(
        num_scalar_prefetch=3,
        grid=(n_tiles,),
        in_specs=[pl.BlockSpec(memory_space=pl.ANY),
                  pl.BlockSpec((1, d, f), lambda t, st, te, tv: (te[t], 0, 0)),
                  pl.BlockSpec((1, d, f), lambda t, st, te, tv: (te[t], 0, 0))],
        out_specs=pl.BlockSpec((MOE_TILE, f), lambda t, st, te, tv: (t, 0)),
        scratch_shapes=[pltpu.VMEM((2, MOE_TILE, half), U32), pltpu.SemaphoreType.DMA((2,))],
    )
    return pl.pallas_call(
        _moe_up_kernel,
        grid_spec=grid_spec,
        out_shape=jax.ShapeDtypeStruct((n_tiles * MOE_TILE, f), BF16),
        compiler_params=_cparams("arbitrary"),
        name="moe_up",
    )(sorted_tok, tile_expert, tile_valid, h, w_gate, w_up)


def _moe_down_kernel(te_ref, tv_ref, h_ref, wd_ref, o_ref):
    del te_ref
    t = pl.program_id(0)

    @pl.when(tv_ref[t] == 1)
    def _():
        y = jnp.dot(h_ref[...], wd_ref[0], preferred_element_type=F32)
        o_ref[...] = _pack_bf16_pairs(y)

    @pl.when(tv_ref[t] == 0)
    def _():
        o_ref[...] = jnp.zeros(o_ref.shape, o_ref.dtype)


def moe_down(hid, w_down, tile_expert, tile_valid):
    n_rows, f = hid.shape
    d = w_down.shape[-1]
    grid_spec = pltpu.PrefetchScalarGridSpec(
        num_scalar_prefetch=2,
        grid=(n_rows // MOE_TILE,),
        in_specs=[pl.BlockSpec((MOE_TILE, f), lambda t, te, tv: (t, 0)),
                  pl.BlockSpec((1, f, d), lambda t, te, tv: (te[t], 0, 0))],
        out_specs=pl.BlockSpec((MOE_TILE, d // 2), lambda t, te, tv: (t, 0)),
    )
    return pl.pallas_call(
        _moe_down_kernel,
        grid_spec=grid_spec,
        out_shape=jax.ShapeDtypeStruct((n_rows, d // 2), U32),
        compiler_params=_cparams("arbitrary"),
        name="moe_down",
    )(tile_expert, tile_valid, hid, w_down)


def _combine_kernel(p0_ref, p1_ref, x_ref, meta_ref, g_ref, ys_ref, o_ref, buf, sem, *, base, final_norm):
    i = pl.program_id(0)
    last = pl.num_programs(0) - 1
    slot = i % 2
    tc = buf.shape[2]

    def start(tile, slot):
        n0 = base + tile * tc
        _row_gather_start(p0_ref, n0, ys_ref, buf.at[slot, 0], sem.at[slot, 0])
        _row_gather_start(p1_ref, n0, ys_ref, buf.at[slot, 1], sem.at[slot, 1])

    def wait(slot):
        _row_gather_wait(ys_ref, buf.at[slot, 0], sem.at[slot, 0])
        _row_gather_wait(ys_ref, buf.at[slot, 1], sem.at[slot, 1])

    @pl.when(i == 0)
    def _():
        start(0, 0)

    wait(slot)
    start(jnp.minimum(i + 1, last), 1 - slot)
    meta = meta_ref[...]
    w0 = meta[:, 0:1]
    w1 = meta[:, 1:2]
    lo0, hi0 = _unpack_bf16_pairs(buf[slot, 0])
    lo1, hi1 = _unpack_bf16_pairs(buf[slot, 1])
    half = lo0.shape[1]
    y_lo = x_ref[:, :half] + w0 * lo0 + w1 * lo1
    y_hi = x_ref[:, half:] + w0 * hi0 + w1 * hi1
    if final_norm:
        ms = (jnp.sum(y_lo * y_lo, axis=-1, keepdims=True) + jnp.sum(y_hi * y_hi, axis=-1, keepdims=True)) / (2 * half)
        inv = lax.rsqrt(ms + RMS_EPS)
        y_lo = y_lo * inv * g_ref[:, :half]
        y_hi = y_hi * inv * g_ref[:, half:]
    o_ref[:, :half] = y_lo
    o_ref[:, half:] = y_hi

    @pl.when(i == last)
    def _():
        wait(1 - slot)


def moe_combine(x, meta, ys, pos0, pos1, gain, *, base, count, final_norm, tc=256):
    d = x.shape[1]
    assert base % tc == 0 and count % tc == 0
    blk0 = base // tc
    grid_spec = pltpu.PrefetchScalarGridSpec(
        num_scalar_prefetch=2,
        grid=(count // tc,),
        in_specs=[pl.BlockSpec((tc, d), lambda i, p0, p1: (blk0 + i, 0)),
                  pl.BlockSpec((tc, LANE), lambda i, p0, p1: (blk0 + i, 0)),
                  pl.BlockSpec((1, d), lambda i, p0, p1: (0, 0)),
                  pl.BlockSpec(memory_space=pl.ANY)],
        out_specs=pl.BlockSpec((tc, d), lambda i, p0, p1: (i, 0)),
        scratch_shapes=[pltpu.VMEM((2, 2, tc, d // 2), U32), pltpu.SemaphoreType.DMA((2, 2))],
    )
    return pl.pallas_call(
        functools.partial(_combine_kernel, base=base, final_norm=final_norm),
        grid_spec=grid_spec,
        out_shape=jax.ShapeDtypeStruct((count, d), F32),
        compiler_params=_cparams("arbitrary"),
        name="moe_combine",
    )(pos0, pos1, x, meta, gain.reshape(1, d), ys)


def moe_experts(x, gain, w_rg, b_rg, w_re, b_re, w_gate, w_up, w_down, layer):
    n = x.shape[0]
    n_rows = 2 * n + N_EXPERTS * MOE_TILE
    assert (2 * n) % MOE_TILE == 0
    h, meta = router(x, gain, w_rg, b_rg, w_re, b_re)
    pos0, pos1, sorted_tok, tile_expert, tile_valid = _routing_plan(meta, n_rows)
    tile_expert = tile_expert + layer * N_EXPERTS
    hid = moe_up(h, w_gate, w_up, sorted_tok, tile_expert, tile_valid)
    ys = moe_down(hid, w_down, tile_expert, tile_valid)
    return meta, ys, pos0, pos1


def kernel(x_prompt, x_sample, mem_prompt, mem_sample, norm_mix, norm_mem, norm_ffn, norm_final, na_w_in, na_rpb, s5_w_in, s5_a_re, s5_a_im, s5_log_dt, s5_b_re, s5_b_im, s5_c_re, s5_c_im, s5_d, s5_w_glu, s5_b_glu, mem_w_kv, w_out, moe_w_rg, moe_b_rg, moe_w_re, moe_b_re, moe_w_gate, moe_w_up, moe_w_down):
    d = x_prompt.shape[-1]
    groups = [x_prompt, x_sample]
    mems = [mem_prompt, mem_sample]
    seq_lens = [g.shape[1] for g in groups for _ in range(g.shape[0])]
    n_mem = mem_prompt.shape[1]
    depth = norm_mix.shape[0]
    x = jnp.concatenate([g.reshape(-1, d) for g in groups], axis=0)
    mem = jnp.concatenate([m.reshape(-1, d) for m in mems], axis=0)
    n = x.shape[0]
    tq_mem = 512
    blk_seq = []
    for s, length in enumerate(seq_lens):
        assert length % tq_mem == 0
        blk_seq += [s] * (length // tq_mem)
    blk_seq = jnp.asarray(np.asarray(blk_seq, np.int32))
    na_heads = na_rpb.shape[1]
    w_gate_all = moe_w_gate.astype(BF16).reshape((-1,) + moe_w_gate.shape[2:])
    w_up_all = moe_w_up.astype(BF16).reshape((-1,) + moe_w_up.shape[2:])
    w_down_all = moe_w_down.astype(BF16).reshape((-1,) + moe_w_down.shape[2:])

    meta = ys = pos0 = pos1 = None
    for i in range(depth):
        j = i // 2
        if i > 0:
            x = moe_combine(x, meta, ys, pos0, pos1, norm_final, base=0, count=n, final_norm=False)
        kv = norm_proj(mem, norm_mem[i], mem_w_kv[i].astype(BF16), BF16, tm=n_mem)
        if i % 2 == 0:
            proj = norm_proj(x, norm_mix[i], na_w_in[j].astype(BF16), BF16, tm=512)
            tok = neighbourhood_attention(proj, na_rpb[j], seq_lens)
            q_slab0 = 3 * na_heads
        else:
            proj = norm_proj(x, norm_mix[i], s5_w_in[j].astype(BF16), F32, tm=512)
            lam_re, lam_im, bb_re, bb_im = s5_discretise(s5_a_re[j], s5_a_im[j], s5_log_dt[j], s5_b_re[j], s5_b_im[j])
            b_op, c_op, lam, d_op = _s5_operands(lam_re, lam_im, bb_re, bb_im, s5_c_re[j], s5_c_im[j], s5_d[j])
            g = s5_mixer(proj, b_op, c_op, lam, d_op, seq_lens)
            tok = glu(g, s5_w_glu[j].astype(BF16), s5_b_glu[j])
            q_slab0 = b_op.shape[1]
        mo = memory_attention(proj, q_slab0, kv, blk_seq, n_mem, tq=tq_mem)
        x = out_proj(x, tok, mo, w_out[i].astype(BF16))
        meta, ys, pos0, pos1 = moe_experts(x, norm_ffn[i], moe_w_rg[i], moe_b_rg[i], moe_w_re[i], moe_b_re[i],
                                           w_gate_all, w_up_all, w_down_all, i)
    outs, base = [], 0
    for g in groups:
        count = g.shape[0] * g.shape[1]
        y = moe_combine(x, meta, ys, pos0, pos1, norm_final, base=base, count=count, final_norm=True)
        outs.append(y.reshape(g.shape))
        base += count
    return tuple(outs)
```

```python
import functools
import math

import numpy as np
import jax
import jax.numpy as jnp
from jax import lax
from jax.experimental import pallas as pl
from jax.experimental.pallas import tpu as pltpu

F32 = jnp.float32
BF16 = jnp.bfloat16
U32 = jnp.uint32

LANE = 128
SUBLANE = 8
VMEM_LIMIT_BYTES = 56 * 1024 * 1024

RMS_EPS = 1e-6
NEG_INF = -1e30
GRID_W = 64
ROW_WIN = 8
COL_WIN = 16
NA_HEAD_DIM = 128
MEM_HEADS = 4
S5_GROUP = 16
S5_STATE = 64
N_EXPERT_GROUPS = 4
EXPERTS_PER_GROUP = 4
N_EXPERTS = N_EXPERT_GROUPS * EXPERTS_PER_GROUP

NA_Q_ROWS = 8
NA_K_ROWS = 2 * ROW_WIN
NA_HEADS_PER_STEP = 2
SUPER = 8192
S5_SEGMENTS = SUBLANE
S5_TBLK = 64
MOE_TILE = 256
MOE_DOWN_CHUNKS = 2


def _cparams(*sem):
    return pltpu.CompilerParams(dimension_semantics=sem, vmem_limit_bytes=VMEM_LIMIT_BYTES)


def _rms_norm(x, gain):
    ms = jnp.mean(x * x, axis=-1, keepdims=True)
    return x * lax.rsqrt(ms + RMS_EPS) * gain


def _sigmoid(x):
    return 1.0 / (1.0 + jnp.exp(-x))


def _pack_bf16_pairs(x):
    k = x.shape[1] // 2
    bits = lax.bitcast_convert_type(x.astype(BF16).astype(F32), U32)
    return (bits[:, k:] & jnp.uint32(0xFFFF0000)) | (bits[:, :k] >> 16)


def _unpack_bf16_pairs(w):
    lo = lax.bitcast_convert_type(w << 16, F32)
    hi = lax.bitcast_convert_type(w & jnp.uint32(0xFFFF0000), F32)
    return lo, hi


def _norm_proj_kernel(x_ref, g_ref, w_ref, o_ref, h_scr):
    @pl.when(pl.program_id(1) == 0)
    def _():
        h_scr[...] = _rms_norm(x_ref[...], g_ref[...]).astype(BF16)

    acc = jnp.dot(h_scr[...], w_ref[...], preferred_element_type=F32)
    for c in range(o_ref.shape[0]):
        o_ref[c] = acc[:, c * LANE:(c + 1) * LANE].astype(o_ref.dtype)


def norm_proj(x, gain, w, out_dtype, tm, tn=1024):
    m, d = x.shape
    n_out = w.shape[1]
    assert m % tm == 0 and n_out % tn == 0 and tn % LANE == 0
    return pl.pallas_call(
        _norm_proj_kernel,
        grid=(m // tm, n_out // tn),
        in_specs=[pl.BlockSpec((tm, d), lambda i, j: (i, 0)),
                  pl.BlockSpec((1, d), lambda i, j: (0, 0)),
                  pl.BlockSpec((d, tn), lambda i, j: (0, j))],
        out_specs=pl.BlockSpec((tn // LANE, tm, LANE), lambda i, j: (j, i, 0)),
        out_shape=jax.ShapeDtypeStruct((n_out // LANE, m, LANE), out_dtype),
        scratch_shapes=[pltpu.VMEM((tm, d), BF16)],
        compiler_params=_cparams("parallel", "arbitrary"),
        name="norm_proj",
    )(x, gain.reshape(1, d), w)


def _na_kernel(ks_ref, ty_ref, q_ref, k_ref, v_ref, b_ref, o_ref, *, nqb, scale):
    step = pl.program_id(1) * nqb + pl.program_id(2)
    start = pl.multiple_of(ks_ref[step] * GRID_W, GRID_W)
    nk = NA_K_ROWS * GRID_W
    for h in range(q_ref.shape[0]):
        q = q_ref[h]
        k = k_ref[h, pl.ds(start, nk), :]
        v = v_ref[h, pl.ds(start, nk), :]
        s = lax.dot_general(q, k, (((1,), (1,)), ((), ())), preferred_element_type=F32)
        s = s * scale + b_ref[h, 0]
        m = jnp.max(s, axis=-1, keepdims=True)
        p = jnp.exp(s - m)
        l = jnp.sum(p, axis=-1, keepdims=True)
        o = jnp.dot(p.astype(BF16), v, preferred_element_type=F32) / l
        o_ref[:, h * LANE:(h + 1) * LANE] = o.astype(o_ref.dtype)


def _na_tables(seq_lens):
    ks, ty = [], []
    tok = 0
    per_super = SUPER // GRID_W
    for length in seq_lens:
        rows = length // GRID_W
        assert length % (NA_Q_ROWS * GRID_W) == 0 and rows >= 3 * NA_Q_ROWS
        assert tok // SUPER == (tok + length - 1) // SUPER
        row0 = (tok % SUPER) // GRID_W
        for qb in range(rows // NA_Q_ROWS):
            ks.append(row0 + int(np.clip(qb * NA_Q_ROWS - ROW_WIN // 2, 0, rows - NA_K_ROWS)))
            ty.append(0 if qb == 0 else (2 if qb == rows // NA_Q_ROWS - 1 else 1))
        tok += length
    assert tok % SUPER == 0 and len(ks) == (tok // SUPER) * (per_super // NA_Q_ROWS)
    return np.asarray(ks, np.int32), np.asarray(ty, np.int32)


def _na_bias(rpb):
    heads = rpb.shape[0]
    cols = np.arange(GRID_W)
    c_start = np.clip(cols - COL_WIN // 2, 0, GRID_W - COL_WIN)
    in_win = (cols[None, :] >= c_start[:, None]) & (cols[None, :] < c_start[:, None] + COL_WIN)
    dc = np.clip(cols[None, :] - cols[:, None], -(COL_WIN - 1), COL_WIN - 1) + COL_WIN - 1
    col_bias = jnp.where(jnp.asarray(in_win), rpb.astype(F32)[:, :, dc], NEG_INF)
    n_dr = 2 * ROW_WIN - 1
    masked = jnp.full((heads, 1, GRID_W, GRID_W), NEG_INF, F32)
    table = jnp.concatenate([col_bias, masked], axis=1).transpose(0, 2, 1, 3)
    i = np.arange(NA_Q_ROWS)[:, None]
    j = np.arange(NA_K_ROWS)[None, :]
    half = ROW_WIN // 2
    q_rel = [i, i + half, i + ROW_WIN]
    w_rel = [np.maximum(i - half, 0), i, np.minimum(i + half, ROW_WIN)]
    out = []
    for qr, wr in zip(q_rel, w_rel):
        valid = (j >= wr) & (j < wr + ROW_WIN)
        dr = np.where(valid, np.clip(j - qr + ROW_WIN - 1, 0, n_dr - 1), n_dr)
        rows = [jnp.take(table, dr[r], axis=2).reshape(heads, GRID_W, NA_K_ROWS * GRID_W) for r in range(NA_Q_ROWS)]
        out.append(jnp.concatenate(rows, axis=1))
    return jnp.stack(out, axis=1)


def neighbourhood_attention(qkv, rpb, seq_lens):
    heads = rpb.shape[0]
    n = qkv.shape[1]
    nsup = n // SUPER
    tq = NA_Q_ROWS * GRID_W
    nqb = SUPER // tq
    ks_tab, ty_tab = _na_tables(seq_lens)
    bias = _na_bias(rpb)
    hp = NA_HEADS_PER_STEP
    assert heads % hp == 0
    ng = heads // hp
    grid_spec = pltpu.PrefetchScalarGridSpec(
        num_scalar_prefetch=2,
        grid=(ng, nsup, nqb),
        in_specs=[
            pl.BlockSpec((hp, tq, LANE), lambda h, s, b, ks, ty: (h, s * nqb + b, 0)),
            pl.BlockSpec((hp, SUPER, LANE), lambda h, s, b, ks, ty: (ng + h, s, 0)),
            pl.BlockSpec((hp, SUPER, LANE), lambda h, s, b, ks, ty: (2 * ng + h, s, 0)),
            pl.BlockSpec((hp, 1, tq, NA_K_ROWS * GRID_W), lambda h, s, b, ks, ty: (h, ty[s * nqb + b], 0, 0)),
        ],
        out_specs=pl.BlockSpec((tq, hp * LANE), lambda h, s, b, ks, ty: (s * nqb + b, h)),
    )
    return pl.pallas_call(
        functools.partial(_na_kernel, nqb=nqb, scale=NA_HEAD_DIM ** -0.5),
        grid_spec=grid_spec,
        out_shape=jax.ShapeDtypeStruct((n, heads * LANE), BF16),
        compiler_params=_cparams("parallel", "arbitrary", "arbitrary"),
        name="neighbourhood_attention",
    )(jnp.asarray(ks_tab), jnp.asarray(ty_tab), qkv, qkv, qkv, bias)


def _mem_attn_kernel(seq_ref, q_ref, kv_ref, o_ref, *, scale):
    del seq_ref
    hd = 2 * LANE
    for h in range(MEM_HEADS):
        q = jnp.concatenate([q_ref[2 * h], q_ref[2 * h + 1]], axis=-1).astype(BF16)
        k = jnp.concatenate([kv_ref[2 * h], kv_ref[2 * h + 1]], axis=-1)
        v = jnp.concatenate([kv_ref[2 * MEM_HEADS + 2 * h], kv_ref[2 * MEM_HEADS + 2 * h + 1]], axis=-1)
        s = lax.dot_general(q, k, (((1,), (1,)), ((), ())), preferred_element_type=F32) * scale
        m = jnp.max(s, axis=-1, keepdims=True)
        p = jnp.exp(s - m)
        l = jnp.sum(p, axis=-1, keepdims=True)
        o = jnp.dot(p.astype(BF16), v, preferred_element_type=F32) / l
        o_ref[:, h * hd:(h + 1) * hd] = o.astype(o_ref.dtype)


def memory_attention(proj, q_slab0, kv, blk_seq, n_mem, tq=512):
    n = proj.shape[1]
    nq = 2 * MEM_HEADS
    assert q_slab0 % nq == 0
    grid_spec = pltpu.PrefetchScalarGridSpec(
        num_scalar_prefetch=1,
        grid=(n // tq,),
        in_specs=[pl.BlockSpec((nq, tq, LANE), lambda i, sq: (q_slab0 // nq, i, 0)),
                  pl.BlockSpec((2 * nq, n_mem, LANE), lambda i, sq: (0, sq[i], 0))],
        out_specs=pl.BlockSpec((tq, nq * LANE), lambda i, sq: (i, 0)),
    )
    return pl.pallas_call(
        functools.partial(_mem_attn_kernel, scale=(2 * LANE) ** -0.5),
        grid_spec=grid_spec,
        out_shape=jax.ShapeDtypeStruct((n, nq * LANE), BF16),
        compiler_params=_cparams("parallel"),
        name="memory_attention",
    )(blk_seq, proj, kv)


def _s5_param_kernel(are_ref, aim_ref, ldt_ref, bre_ref, bim_ref, lre_ref, lim_ref, bbre_ref, bbim_ref):
    a_re = are_ref[...]
    a_im = aim_ref[...]
    dt = jnp.exp(ldt_ref[...])
    mag = jnp.exp(a_re * dt)
    ang = a_im * dt
    lam_re = mag * jnp.cos(ang)
    lam_im = mag * jnp.sin(ang)
    den = a_re * a_re + a_im * a_im
    n_re = lam_re - 1.0
    coef_re = (n_re * a_re + lam_im * a_im) / den
    coef_im = (lam_im * a_re - n_re * a_im) / den
    lre_ref[...] = lam_re
    lim_ref[...] = lam_im
    for c in range(bre_ref.shape[0]):
        b_re = bre_ref[c]
        b_im = bim_ref[c]
        bbre_ref[c] = coef_re * b_re - coef_im * b_im
        bbim_ref[c] = coef_re * b_im + coef_im * b_re


def s5_discretise(a_re, a_im, log_dt, b_re, b_im):
    two, g, p = a_re.shape
    c = b_re.shape[-1]
    rows = two * g
    ldt = jnp.broadcast_to(log_dt.reshape(rows, 1), (rows, p))
    b_re_t = b_re.reshape(rows, p, c).transpose(2, 0, 1)
    b_im_t = b_im.reshape(rows, p, c).transpose(2, 0, 1)
    return pl.pallas_call(
        _s5_param_kernel,
        out_shape=(jax.ShapeDtypeStruct((rows, p), F32), jax.ShapeDtypeStruct((rows, p), F32),
                   jax.ShapeDtypeStruct((c, rows, p), F32), jax.ShapeDtypeStruct((c, rows, p), F32)),
        name="s5_discretise",
    )(a_re.reshape(rows, p), a_im.reshape(rows, p), ldt, b_re_t, b_im_t)


def _s5_operands(lam_re, lam_im, bb_re, bb_im, c_re, c_im, d):
    c, rows, p = bb_re.shape
    g = rows // 2
    gs = LANE // c
    ns = g // gs
    eye = jnp.eye(gs, dtype=F32)

    def bmat(bb):
        x = bb.reshape(c, 2, ns, gs, p).transpose(1, 2, 3, 0, 4)
        return jnp.einsum('dsgcp,gh->dsgchp', x, eye).reshape(2, ns, gs * c, gs * p)

    def cmat(cc):
        x = cc.astype(F32).reshape(2, ns, gs, c, p)
        return jnp.einsum('dsgcp,gh->dsgphc', x, eye).reshape(2, ns, gs * p, gs * c)

    b_op = jnp.concatenate([bmat(bb_re), bmat(bb_im)], axis=-1).astype(BF16)
    c_op = jnp.concatenate([cmat(c_re), -cmat(c_im)], axis=-2).astype(BF16)
    lam = jnp.concatenate([lam_re.reshape(2, ns, gs * p), lam_im.reshape(2, ns, gs * p)], axis=-1)
    lam = jnp.broadcast_to(lam[:, :, None, :], (2, ns, SUBLANE, 2 * gs * p))
    d_op = d.astype(F32).reshape(ns, 1, gs * c)
    return b_op, c_op, lam, d_op


def _s5_kernel(keep_ref, u_hbm, bm_ref, cm_ref, lam_ref, d_ref, o_ref,
               u_scr, u_sem, up_scr, y_scr, y2_scr, bu0_scr, bu1_scr, xs0_scr, xs1_scr, e_scr, i_scr, *, lseg, tblk):
    sup = pl.program_id(0)
    nblk = lseg // tblk
    rb = SUBLANE * tblk
    ns = bu0_scr.shape[1] // 2
    nchunk = ns // LANE
    d_row = d_ref[0]
    pitch = u_scr.shape[1] // S5_SEGMENTS
    n_slab = pl.num_programs(1)
    lin = sup * n_slab + pl.program_id(1)
    slot = lin % 2

    def segment_copies(step, slot):
        b = step // n_slab
        s = step % n_slab
        return [pltpu.make_async_copy(u_hbm.at[s, pl.ds(b * (S5_SEGMENTS * lseg) + r * lseg, lseg), :],
                                      u_scr.at[slot, pl.ds(r * pitch, lseg), :], u_sem.at[slot])
                for r in range(S5_SEGMENTS)]

    @pl.when(lin == 0)
    def _():
        for cp in segment_copies(lin, slot):
            cp.start()

    for cp in segment_copies(lin, slot):
        cp.wait()

    @pl.when(lin + 1 < pl.num_programs(0) * n_slab)
    def _():
        for cp in segment_copies(lin + 1, 1 - slot):
            cp.start()

    u2 = u_scr.at[slot]

    def permute_in(t2, carry):
        t = t2 * 2
        a = u2[pl.ds(t, SUBLANE, stride=pitch), :]
        b = u2[pl.ds(t + 1, SUBLANE, stride=pitch), :]
        ab = jnp.concatenate([a, b], axis=0)
        rows = pl.ds(pl.multiple_of(t * SUBLANE, 2 * SUBLANE), 2 * SUBLANE)
        up_scr[rows, :] = ab.astype(BF16)
        y_scr[rows, :] = ab * d_row
        return carry
    lax.fori_loop(0, lseg // 2, permute_in, 0, unroll=4)
    xs1_scr[...] = jnp.zeros(xs1_scr.shape, xs1_scr.dtype)

    for direction in range(2):
        rev = direction == 1
        lam_r = [lam_ref[direction, 0, :, c * LANE:(c + 1) * LANE] for c in range(nchunk)]
        lam_i = [lam_ref[direction, 0, :, ns + c * LANE:ns + (c + 1) * LANE] for c in range(nchunk)]
        steps = list(range(tblk))[::-1] if rev else list(range(tblk))

        def block_rows(k, rev=rev):
            k = jnp.clip(k, 0, nblk - 1)
            blk = (nblk - 1 - k) if rev else k
            return pl.ds(pl.multiple_of(blk * rb, rb), rb)

        def bu_dot(k, direction=direction, block_rows=block_rows):
            return jnp.dot(up_scr[block_rows(k), :], bm_ref[direction, 0], preferred_element_type=F32)

        def scan_block(bu, x, xs, lam_r=lam_r, lam_i=lam_i, steps=steps):
            xr, xi = list(x[:nchunk]), list(x[nchunk:])
            held = None
            for n, t in enumerate(steps):
                rows = slice(t * SUBLANE, (t + 1) * SUBLANE)
                for c in range(nchunk):
                    b_r = bu[rows, c * LANE:(c + 1) * LANE]
                    b_i = bu[rows, ns + c * LANE:ns + (c + 1) * LANE]
                    r = lam_r[c] * xr[c] - lam_i[c] * xi[c] + b_r
                    i = lam_r[c] * xi[c] + lam_i[c] * xr[c] + b_i
                    xr[c], xi[c] = r, i
                if xs is None:
                    continue
                if n % 2 == 0:
                    held = (t, list(xr), list(xi))
                    continue
                lo, hi = (held, (t, xr, xi)) if held[0] < t else ((t, xr, xi), held)
                pair = slice(lo[0] * SUBLANE, (lo[0] + 2) * SUBLANE)
                for c in range(nchunk):
                    xs[pair, c * LANE:(c + 1) * LANE] = jnp.concatenate([lo[1][c], hi[1][c]], axis=0).astype(BF16)
                    xs[pair, ns + c * LANE:ns + (c + 1) * LANE] = (
                        jnp.concatenate([lo[2][c], hi[2][c]], axis=0).astype(BF16))
            return tuple(xr + xi)

        def pass_a(k2, x, bu_dot=bu_dot, scan_block=scan_block):
            bu1_scr[...] = bu_dot(2 * k2 + 1)
            x = scan_block(bu0_scr, x, None)
            bu0_scr[...] = bu_dot(2 * k2 + 2)
            return scan_block(bu1_scr, x, None)
        bu0_scr[...] = bu_dot(0)
        zero = tuple(jnp.zeros((SUBLANE, LANE), F32) for _ in range(2 * nchunk))
        ends = lax.fori_loop(0, nblk // 2, pass_a, zero)
        for c in range(2 * nchunk):
            e_scr[:, c * LANE:(c + 1) * LANE] = ends[c]

        p_r = lam_ref[direction, 0, 0:1, 0:ns]
        p_i = lam_ref[direction, 0, 0:1, ns:2 * ns]
        for _ in range(int(math.log2(lseg))):
            p_r, p_i = p_r * p_r - p_i * p_i, 2.0 * p_r * p_i
        order = list(range(S5_SEGMENTS))[::-1] if rev else list(range(S5_SEGMENTS))
        s_r = jnp.zeros((1, ns), F32)
        s_i = jnp.zeros((1, ns), F32)
        i_scr[order[0]:order[0] + 1, :] = jnp.zeros((1, 2 * ns), F32)
        for prev, cur in zip(order[:-1], order[1:]):
            e_r = e_scr[prev:prev + 1, 0:ns]
            e_i = e_scr[prev:prev + 1, ns:2 * ns]
            keep = keep_ref[(sup * 2 + direction) * S5_SEGMENTS + cur].astype(F32)
            n_r = (p_r * s_r - p_i * s_i + e_r) * keep
            n_i = (p_r * s_i + p_i * s_r + e_i) * keep
            i_scr[cur:cur + 1, 0:ns] = n_r
            i_scr[cur:cur + 1, ns:2 * ns] = n_i
            s_r, s_i = n_r, n_i

        def project(k, xs, scale, direction=direction, block_rows=block_rows):
            rows = block_rows(k)
            yb = jnp.dot(xs[...], cm_ref[direction, 0], preferred_element_type=F32)
            y_scr[rows, :] = y_scr[rows, :] + yb * scale

        def pass_b(k2, x, bu_dot=bu_dot, scan_block=scan_block, project=project):
            bu1_scr[...] = bu_dot(2 * k2 + 1)
            x = scan_block(bu0_scr, x, xs0_scr)
            project(2 * k2 - 1, xs1_scr, jnp.where(k2 > 0, 1.0, 0.0))
            bu0_scr[...] = bu_dot(2 * k2 + 2)
            x = scan_block(bu1_scr, x, xs1_scr)
            project(2 * k2, xs0_scr, 1.0)
            return x
        bu0_scr[...] = bu_dot(0)
        start = tuple(i_scr[:, c * LANE:(c + 1) * LANE] for c in range(2 * nchunk))
        lax.fori_loop(0, nblk // 2, pass_b, start)
        project(nblk - 1, xs1_scr, 1.0)

    def permute_out(t, carry):
        y = y_scr[pl.ds(pl.multiple_of(t * SUBLANE, SUBLANE), SUBLANE), :]
        gelu = 0.5 * y * (1.0 + jnp.tanh(math.sqrt(2.0 / math.pi) * (y + 0.044715 * (y * y * y))))
        y2_scr[pl.ds(t, SUBLANE, stride=pitch), :] = gelu
        return carry
    lax.fori_loop(0, lseg, permute_out, 0, unroll=8)
    for r in range(S5_SEGMENTS):
        o_ref[r * lseg:(r + 1) * lseg, :] = y2_scr[r * pitch:r * pitch + lseg, :].astype(o_ref.dtype)


def _s5_keep_table(seq_lens, lseg):
    starts, ends = set(), set()
    tok = 0
    for length in seq_lens:
        assert length % lseg == 0
        starts.add(tok // lseg)
        tok += length
        ends.add(tok // lseg - 1)
    nseg = tok // lseg
    assert nseg % S5_SEGMENTS == 0
    keep = np.ones((nseg // S5_SEGMENTS, 2, S5_SEGMENTS), np.int32)
    for s in range(nseg):
        if s in starts:
            keep[s // S5_SEGMENTS, 0, s % S5_SEGMENTS] = 0
        if s in ends:
            keep[s // S5_SEGMENTS, 1, s % S5_SEGMENTS] = 0
    return keep.reshape(-1)


def s5_mixer(proj, b_op, c_op, lam, d_op, seq_lens):
    n = proj.shape[1]
    ns = b_op.shape[1]
    nstate2 = b_op.shape[-1]
    nsup = n // SUPER
    lseg = SUPER // S5_SEGMENTS
    tblk = min(S5_TBLK, lseg)
    assert lseg & (lseg - 1) == 0 and lseg % (2 * tblk) == 0 and tblk % 2 == 0
    keep = _s5_keep_table(seq_lens, lseg)
    rows = SUBLANE * tblk
    pitch = lseg + SUBLANE
    grid_spec = pltpu.PrefetchScalarGridSpec(
        num_scalar_prefetch=1,
        grid=(nsup, ns),
        in_specs=[
            pl.BlockSpec(memory_space=pl.ANY),
            pl.BlockSpec((2, 1, LANE, nstate2), lambda b, s, kp: (0, s, 0, 0)),
            pl.BlockSpec((2, 1, nstate2, LANE), lambda b, s, kp: (0, s, 0, 0)),
            pl.BlockSpec((2, 1, SUBLANE, nstate2), lambda b, s, kp: (0, s, 0, 0)),
            pl.BlockSpec((1, 1, LANE), lambda b, s, kp: (s, 0, 0)),
        ],
        out_specs=pl.BlockSpec((SUPER, LANE), lambda b, s, kp: (b, s)),
        scratch_shapes=[
            pltpu.VMEM((2, S5_SEGMENTS * pitch, LANE), F32),
            pltpu.SemaphoreType.DMA((2,)),
            pltpu.VMEM((SUPER, LANE), BF16),
            pltpu.VMEM((SUPER, LANE), F32),
            pltpu.VMEM((S5_SEGMENTS * pitch, LANE), F32),
            pltpu.VMEM((rows, nstate2), F32),
            pltpu.VMEM((rows, nstate2), F32),
            pltpu.VMEM((rows, nstate2), BF16),
            pltpu.VMEM((rows, nstate2), BF16),
            pltpu.VMEM((SUBLANE, nstate2), F32),
            pltpu.VMEM((SUBLANE, nstate2), F32),
        ],
    )
    return pl.pallas_call(
        functools.partial(_s5_kernel, lseg=lseg, tblk=tblk),
        grid_spec=grid_spec,
        out_shape=jax.ShapeDtypeStruct((n, ns * LANE), BF16),
        compiler_params=_cparams("arbitrary", "arbitrary"),
        name="s5_mixer",
    )(jnp.asarray(keep), proj, b_op, c_op, lam, d_op)


def _glu_kernel(g_ref, w_ref, b_ref, o_ref):
    tn = o_ref.shape[1]
    col = pl.multiple_of(pl.program_id(1) * tn, tn)
    acc = jnp.dot(g_ref[...], w_ref[...], preferred_element_type=F32) + b_ref[...]
    g = g_ref[:, pl.ds(col, tn)].astype(F32)
    o_ref[...] = (g * _sigmoid(acc)).astype(o_ref.dtype)


def glu(g, w, b, tm=1024, tn=1024):
    m, k = g.shape
    assert m % tm == 0 and k % tn == 0
    return pl.pallas_call(
        _glu_kernel,
        grid=(m // tm, k // tn),
        in_specs=[pl.BlockSpec((tm, k), lambda i, j: (i, 0)),
                  pl.BlockSpec((k, tn), lambda i, j: (0, j)),
                  pl.BlockSpec((1, tn), lambda i, j: (0, j))],
        out_specs=pl.BlockSpec((tm, tn), lambda i, j: (i, j)),
        out_shape=jax.ShapeDtypeStruct((m, k), BF16),
        compiler_params=_cparams("parallel", "arbitrary"),
        name="glu",
    )(g, w, b.reshape(1, k).astype(F32))


def _out_proj_kernel(x_ref, tok_ref, mo_ref, w1_ref, w2_ref, o_ref):
    acc = jnp.dot(tok_ref[...], w1_ref[...], preferred_element_type=F32)
    acc = acc + jnp.dot(mo_ref[...], w2_ref[...], preferred_element_type=F32)
    o_ref[...] = x_ref[...] + acc


def out_proj(x, tok, mo, w, tm=1024, tn=512):
    m, d = x.shape
    k1, k2 = tok.shape[1], mo.shape[1]
    assert m % tm == 0 and d % tn == 0 and w.shape[0] == k1 + k2 and k1 % k2 == 0
    return pl.pallas_call(
        _out_proj_kernel,
        grid=(m // tm, d // tn),
        in_specs=[pl.BlockSpec((tm, tn), lambda i, j: (i, j)),
                  pl.BlockSpec((tm, k1), lambda i, j: (i, 0)),
                  pl.BlockSpec((tm, k2), lambda i, j: (i, 0)),
                  pl.BlockSpec((k1, tn), lambda i, j: (0, j)),
                  pl.BlockSpec((k2, tn), lambda i, j: (k1 // k2, j))],
        out_specs=pl.BlockSpec((tm, tn), lambda i, j: (i, j)),
        out_shape=jax.ShapeDtypeStruct((m, d), F32),
        compiler_params=_cparams("parallel", "arbitrary"),
        name="out_proj",
    )(x, tok, mo, w, w)


def _router_kernel(x_ref, g_ref, w_ref, b_ref, h_ref, meta_ref):
    t = _rms_norm(x_ref[...], g_ref[...])
    h_ref[...] = _pack_bf16_pairs(t)
    logits = jnp.dot(t, w_ref[...], preferred_element_type=F32, precision=lax.Precision.HIGHEST) + b_ref[...]
    lane = lax.broadcasted_iota(jnp.int32, logits.shape, 1)
    big = jnp.int32(LANE)
    is_g = lane < N_EXPERT_GROUPS
    gl = jnp.where(is_g, logits, -jnp.inf)
    gmax = jnp.max(gl, axis=-1, keepdims=True)
    gsum = jnp.sum(jnp.where(is_g, jnp.exp(gl - gmax), 0.0), axis=-1, keepdims=True)
    g_val = 1.0 / gsum
    g_idx = jnp.min(jnp.where(gl == gmax, lane, big), axis=-1, keepdims=True)
    e_lane = lane - N_EXPERT_GROUPS
    in_grp = (e_lane >= 0) & (e_lane < N_EXPERTS) & ((e_lane >> 2) == g_idx)
    el = jnp.where(in_grp, logits, -jnp.inf)
    v1 = jnp.max(el, axis=-1, keepdims=True)
    i1 = jnp.min(jnp.where(el == v1, lane, big), axis=-1, keepdims=True)
    el2 = jnp.where(lane == i1, -jnp.inf, el)
    v2 = jnp.max(el2, axis=-1, keepdims=True)
    i2 = jnp.min(jnp.where(el2 == v2, lane, big), axis=-1, keepdims=True)
    z = jnp.exp(v2 - v1)
    w1 = g_val / (1.0 + z)
    w2 = g_val * z / (1.0 + z)
    e1 = (i1 - N_EXPERT_GROUPS).astype(F32)
    e2 = (i2 - N_EXPERT_GROUPS).astype(F32)
    meta = jnp.where(lane == 0, w1, jnp.where(lane == 1, w2, jnp.where(lane == 2, e1, jnp.where(lane == 3, e2, 0.0))))
    meta_ref[...] = meta


def router(x, gain, w_rg, b_rg, w_re, b_re, tm=512):
    m, d = x.shape
    assert EXPERTS_PER_GROUP == 4 and m % tm == 0
    pad = LANE - N_EXPERT_GROUPS - N_EXPERTS
    w = jnp.concatenate([w_rg, w_re, jnp.zeros((d, pad), F32)], axis=1).astype(F32)
    b = jnp.concatenate([b_rg, b_re, jnp.zeros((pad,), F32)]).astype(F32).reshape(1, LANE)
    return pl.pallas_call(
        _router_kernel,
        grid=(m // tm,),
        in_specs=[pl.BlockSpec((tm, d), lambda i: (i, 0)),
                  pl.BlockSpec((1, d), lambda i: (0, 0)),
                  pl.BlockSpec((d, LANE), lambda i: (0, 0)),
                  pl.BlockSpec((1, LANE), lambda i: (0, 0))],
        out_specs=(pl.BlockSpec((tm, d // 2), lambda i: (i, 0)),
                   pl.BlockSpec((tm, LANE), lambda i: (i, 0))),
        out_shape=(jax.ShapeDtypeStruct((m, d // 2), U32), jax.ShapeDtypeStruct((m, LANE), F32)),
        compiler_params=_cparams("parallel"),
        name="moe_router",
    )(x, gain.reshape(1, d), w, b)


def _routing_plan(meta, n_rows):
    n = meta.shape[0]
    e = meta[:, 2:4].astype(jnp.int32)
    onehot = (e[:, :, None] == jnp.arange(N_EXPERTS)[None, None, :]).any(axis=1).astype(jnp.int32)
    counts = onehot.sum(axis=0)
    rank = jnp.cumsum(onehot, axis=0) - onehot
    padded = ((counts + MOE_TILE - 1) // MOE_TILE) * MOE_TILE
    ends = jnp.cumsum(padded)
    offs = ends - padded
    pos = jnp.take_along_axis(offs[None, :] + rank, e, axis=1).astype(jnp.int32)
    tok = jnp.arange(n, dtype=jnp.int32)
    sorted_tok = jnp.zeros((n_rows,), jnp.int32).at[pos[:, 0]].set(tok).at[pos[:, 1]].set(tok)
    tile_start = jnp.arange(n_rows // MOE_TILE, dtype=jnp.int32) * MOE_TILE
    tile_expert = jnp.minimum((ends[None, :] <= tile_start[:, None]).sum(axis=1), N_EXPERTS - 1).astype(jnp.int32)
    tile_valid = (tile_start < ends[-1]).astype(jnp.int32)
    return pos[:, 0], pos[:, 1], sorted_tok, tile_expert, tile_valid


def _row_gather_start(idx_ref, base, src_ref, dst_ref, sem):
    for r in range(dst_ref.shape[0]):
        pltpu.make_async_copy(src_ref.at[pl.ds(idx_ref[base + r], 1), :], dst_ref.at[pl.ds(r, 1), :], sem).start()


def _row_gather_wait(src_ref, dst_ref, sem):
    pltpu.make_async_copy(src_ref.at[pl.ds(0, dst_ref.shape[0]), :], dst_ref, sem).wait()


def _expert_changed(te_ref, t):
    return (t == 0) | (te_ref[t] != te_ref[jnp.maximum(t - 1, 0)])


def _moe_up_kernel(st_ref, te_ref, nv_ref, h_ref, wg_ref, wu_ref, o_ref, xbuf, sem, wg_s, wu_s):
    j = pl.program_id(0)
    t = pl.program_id(1)
    n_valid = nv_ref[0]
    tile = xbuf.shape[1]
    slot = (j * n_valid + t) % 2

    @pl.when((j == 0) & (t == 0))
    def _():
        _row_gather_start(st_ref, 0, h_ref, xbuf.at[0], sem.at[0])

    @pl.when(t < n_valid)
    def _():
        @pl.when(_expert_changed(te_ref, t))
        def _():
            wg_s[...] = wg_ref[0].astype(BF16)
            wu_s[...] = wu_ref[0].astype(BF16)

        _row_gather_wait(h_ref, xbuf.at[slot], sem.at[slot])
        nxt = jnp.where(t + 1 < n_valid, t + 1, 0)
        _row_gather_start(st_ref, nxt * tile, h_ref, xbuf.at[1 - slot], sem.at[1 - slot])
        lo, hi = _unpack_bf16_pairs(xbuf[slot])
        lo = lo.astype(BF16)
        hi = hi.astype(BF16)
        half = lo.shape[1]
        g = jnp.dot(lo, wg_s[:half], preferred_element_type=F32)
        g = g + jnp.dot(hi, wg_s[half:], preferred_element_type=F32)
        u = jnp.dot(lo, wu_s[:half], preferred_element_type=F32)
        u = u + jnp.dot(hi, wu_s[half:], preferred_element_type=F32)
        o_ref[...] = (g * _sigmoid(g) * u).astype(o_ref.dtype)

        @pl.when((j == pl.num_programs(0) - 1) & (t == n_valid - 1))
        def _():
            _row_gather_wait(h_ref, xbuf.at[1 - slot], sem.at[1 - slot])

    @pl.when(t >= n_valid)
    def _():
        o_ref[...] = jnp.zeros(o_ref.shape, o_ref.dtype)


def moe_up(h, w_gate, w_up, sorted_tok, tile_expert, n_valid, tf=512):
    half = h.shape[1]
    d = 2 * half
    f = w_gate.shape[-1]
    n_tiles = tile_expert.shape[0]
    assert f % tf == 0
    grid_spec = pltpu.PrefetchScalarGridSpec(
        num_scalar_prefetch=3,
        grid=(f // tf, n_tiles),
        in_specs=[pl.BlockSpec(memory_space=pl.ANY),
                  pl.BlockSpec((1, d, tf), lambda j, t, st, te, nv: (te[t], 0, j)),
                  pl.BlockSpec((1, d, tf), lambda j, t, st, te, nv: (te[t], 0, j))],
        out_specs=pl.BlockSpec((MOE_TILE, tf), lambda j, t, st, te, nv: (t, j)),
        scratch_shapes=[pltpu.VMEM((2, MOE_TILE, half), U32), pltpu.SemaphoreType.DMA((2,)),
                        pltpu.VMEM((d, tf), BF16), pltpu.VMEM((d, tf), BF16)],
    )
    return pl.pallas_call(
        _moe_up_kernel,
        grid_spec=grid_spec,
        out_shape=jax.ShapeDtypeStruct((n_tiles * MOE_TILE, f), BF16),
        compiler_params=_cparams("arbitrary", "arbitrary"),
        name="moe_up",
    )(sorted_tok, tile_expert, n_valid, h, w_gate, w_up)


def _moe_down_kernel(te_ref, nv_ref, h_ref, wd_ref, o_ref, wd_s):
    t = pl.program_id(1)

    @pl.when(t < nv_ref[0])
    def _():
        @pl.when(_expert_changed(te_ref, t))
        def _():
            wd_s[...] = wd_ref[0].astype(BF16)

        y = jnp.dot(h_ref[...], wd_s[...], preferred_element_type=F32)
        o_ref[...] = _pack_bf16_pairs(y)

    @pl.when(t >= nv_ref[0])
    def _():
        o_ref[...] = jnp.zeros(o_ref.shape, o_ref.dtype)


def moe_down(hid, w_down, tile_expert, n_valid):
    n_rows, f = hid.shape
    d = w_down.shape[-1]
    dc = d // MOE_DOWN_CHUNKS
    grid_spec = pltpu.PrefetchScalarGridSpec(
        num_scalar_prefetch=2,
        grid=(MOE_DOWN_CHUNKS, n_rows // MOE_TILE),
        in_specs=[pl.BlockSpec((MOE_TILE, f), lambda j, t, te, nv: (t, 0)),
                  pl.BlockSpec((1, f, dc), lambda j, t, te, nv: (te[t], 0, j))],
        out_specs=pl.BlockSpec((MOE_TILE, dc // 2), lambda j, t, te, nv: (t, j)),
        scratch_shapes=[pltpu.VMEM((f, dc), BF16)],
    )
    return pl.pallas_call(
        _moe_down_kernel,
        grid_spec=grid_spec,
        out_shape=jax.ShapeDtypeStruct((n_rows, d // 2), U32),
        compiler_params=_cparams("arbitrary", "arbitrary"),
        name="moe_down",
    )(tile_expert, n_valid, hid, w_down)


def _combine_kernel(p0_ref, p1_ref, x_ref, meta_ref, g_ref, ys_ref, o_ref, buf, sem, *, base, final_norm):
    i = pl.program_id(0)
    last = pl.num_programs(0) - 1
    slot = i % 2
    tc = buf.shape[2]

    def start(tile, slot):
        n0 = base + tile * tc
        _row_gather_start(p0_ref, n0, ys_ref, buf.at[slot, 0], sem.at[slot, 0])
        _row_gather_start(p1_ref, n0, ys_ref, buf.at[slot, 1], sem.at[slot, 1])

    def wait(slot):
        _row_gather_wait(ys_ref, buf.at[slot, 0], sem.at[slot, 0])
        _row_gather_wait(ys_ref, buf.at[slot, 1], sem.at[slot, 1])

    @pl.when(i == 0)
    def _():
        start(0, 0)

    wait(slot)
    start(jnp.minimum(i + 1, last), 1 - slot)
    meta = meta_ref[...]
    w0 = meta[:, 0:1]
    w1 = meta[:, 1:2]
    q = buf.shape[3] // MOE_DOWN_CHUNKS
    pieces = []
    for c in range(MOE_DOWN_CHUNKS):
        lo0, hi0 = _unpack_bf16_pairs(buf[slot, 0, :, c * q:(c + 1) * q])
        lo1, hi1 = _unpack_bf16_pairs(buf[slot, 1, :, c * q:(c + 1) * q])
        col = 2 * c * q
        pieces.append((col, x_ref[:, col:col + q] + w0 * lo0 + w1 * lo1))
        pieces.append((col + q, x_ref[:, col + q:col + 2 * q] + w0 * hi0 + w1 * hi1))
    if final_norm:
        ms = sum(jnp.sum(y * y, axis=-1, keepdims=True) for _, y in pieces) / x_ref.shape[1]
        inv = lax.rsqrt(ms + RMS_EPS)
        pieces = [(col, y * inv * g_ref[:, col:col + q]) for col, y in pieces]
    for col, y in pieces:
        o_ref[:, col:col + q] = y

    @pl.when(i == last)
    def _():
        wait(1 - slot)


def moe_combine(x, meta, ys, pos0, pos1, gain, *, base, count, final_norm, tc=256):
    d = x.shape[1]
    assert base % tc == 0 and count % tc == 0
    blk0 = base // tc
    grid_spec = pltpu.PrefetchScalarGridSpec(
        num_scalar_prefetch=2,
        grid=(count // tc,),
        in_specs=[pl.BlockSpec((tc, d), lambda i, p0, p1: (blk0 + i, 0)),
                  pl.BlockSpec((tc, LANE), lambda i, p0, p1: (blk0 + i, 0)),
                  pl.BlockSpec((1, d), lambda i, p0, p1: (0, 0)),
                  pl.BlockSpec(memory_space=pl.ANY)],
        out_specs=pl.BlockSpec((tc, d), lambda i, p0, p1: (i, 0)),
        scratch_shapes=[pltpu.VMEM((2, 2, tc, d // 2), U32), pltpu.SemaphoreType.DMA((2, 2))],
    )
    return pl.pallas_call(
        functools.partial(_combine_kernel, base=base, final_norm=final_norm),
        grid_spec=grid_spec,
        out_shape=jax.ShapeDtypeStruct((count, d), F32),
        compiler_params=_cparams("arbitrary"),
        name="moe_combine",
    )(pos0, pos1, x, meta, gain.reshape(1, d), ys)


def moe_experts(x, gain, w_rg, b_rg, w_re, b_re, w_gate, w_up, w_down, layer):
    n = x.shape[0]
    n_rows = 2 * n + N_EXPERTS * MOE_TILE
    assert (2 * n) % MOE_TILE == 0
    h, meta = router(x, gain, w_rg, b_rg, w_re, b_re)
    pos0, pos1, sorted_tok, tile_expert, tile_valid = _routing_plan(meta, n_rows)
    tile_expert = tile_expert + layer * N_EXPERTS
    n_valid = jnp.sum(tile_valid).astype(jnp.int32).reshape(1)
    hid = moe_up(h, w_gate, w_up, sorted_tok, tile_expert, n_valid)
    ys = moe_down(hid, w_down, tile_expert, n_valid)
    return meta, ys, pos0, pos1


def kernel(x_prompt, x_sample, mem_prompt, mem_sample, norm_mix, norm_mem, norm_ffn, norm_final, na_w_in, na_rpb, s5_w_in, s5_a_re, s5_a_im, s5_log_dt, s5_b_re, s5_b_im, s5_c_re, s5_c_im, s5_d, s5_w_glu, s5_b_glu, mem_w_kv, w_out, moe_w_rg, moe_b_rg, moe_w_re, moe_b_re, moe_w_gate, moe_w_up, moe_w_down):
    d = x_prompt.shape[-1]
    groups = [x_prompt, x_sample]
    mems = [mem_prompt, mem_sample]
    seq_lens = [g.shape[1] for g in groups for _ in range(g.shape[0])]
    n_mem = mem_prompt.shape[1]
    depth = norm_mix.shape[0]
    x = jnp.concatenate([g.reshape(-1, d) for g in groups], axis=0)
    mem = jnp.concatenate([m.reshape(-1, d) for m in mems], axis=0)
    n = x.shape[0]
    tq_mem = 512
    blk_seq = []
    for s, length in enumerate(seq_lens):
        assert length % tq_mem == 0
        blk_seq += [s] * (length // tq_mem)
    blk_seq = jnp.asarray(np.asarray(blk_seq, np.int32))
    na_heads = na_rpb.shape[1]
    w_gate_all = moe_w_gate.reshape((-1,) + moe_w_gate.shape[2:])
    w_up_all = moe_w_up.reshape((-1,) + moe_w_up.shape[2:])
    w_down_all = moe_w_down.reshape((-1,) + moe_w_down.shape[2:])

    meta = ys = pos0 = pos1 = None
    for i in range(depth):
        j = i // 2
        if i > 0:
            x = moe_combine(x, meta, ys, pos0, pos1, norm_final, base=0, count=n, final_norm=False)
        kv = norm_proj(mem, norm_mem[i], mem_w_kv[i].astype(BF16), BF16, tm=n_mem)
        if i % 2 == 0:
            proj = norm_proj(x, norm_mix[i], na_w_in[j].astype(BF16), BF16, tm=512)
            tok = neighbourhood_attention(proj, na_rpb[j], seq_lens)
            q_slab0 = 3 * na_heads
        else:
            proj = norm_proj(x, norm_mix[i], s5_w_in[j].astype(BF16), F32, tm=512)
            lam_re, lam_im, bb_re, bb_im = s5_discretise(s5_a_re[j], s5_a_im[j], s5_log_dt[j], s5_b_re[j], s5_b_im[j])
            b_op, c_op, lam, d_op = _s5_operands(lam_re, lam_im, bb_re, bb_im, s5_c_re[j], s5_c_im[j], s5_d[j])
            g = s5_mixer(proj, b_op, c_op, lam, d_op, seq_lens)
            tok = glu(g, s5_w_glu[j].astype(BF16), s5_b_glu[j])
            q_slab0 = b_op.shape[1]
        mo = memory_attention(proj, q_slab0, kv, blk_seq, n_mem, tq=tq_mem)
        x = out_proj(x, tok, mo, w_out[i].astype(BF16))
        meta, ys, pos0, pos1 = moe_experts(x, norm_ffn[i], moe_w_rg[i], moe_b_rg[i], moe_w_re[i], moe_b_re[i],
                                           w_gate_all, w_up_all, w_down_all, i)
    outs, base = [], 0
    for g in groups:
        count = g.shape[0] * g.shape[1]
        y = moe_combine(x, meta, ys, pos0, pos1, norm_final, base=base, count=count, final_norm=True)
        outs.append(y.reshape(g.shape))
        base += count
    return tuple(outs)
```

```python
import functools
import math

import numpy as np
import jax
import jax.numpy as jnp
from jax import lax
from jax.experimental import pallas as pl
from jax.experimental.pallas import tpu as pltpu

F32 = jnp.float32
BF16 = jnp.bfloat16
U32 = jnp.uint32

LANE = 128
SUBLANE = 8
VMEM_LIMIT_BYTES = 56 * 1024 * 1024

RMS_EPS = 1e-6
NEG_INF = -1e30
GRID_W = 64
ROW_WIN = 8
COL_WIN = 16
NA_HEAD_DIM = 128
MEM_HEADS = 4
S5_GROUP = 16
S5_STATE = 64
N_EXPERT_GROUPS = 4
EXPERTS_PER_GROUP = 4
N_EXPERTS = N_EXPERT_GROUPS * EXPERTS_PER_GROUP

NA_Q_ROWS = 8
NA_K_ROWS = 2 * ROW_WIN
NA_HEADS_PER_STEP = 2
SUPER = 8192
S5_SEGMENTS = SUBLANE
S5_TBLK = 64
MOE_TILE = 256


def _cparams(*sem):
    return pltpu.CompilerParams(dimension_semantics=sem, vmem_limit_bytes=VMEM_LIMIT_BYTES)


def _rms_norm(x, gain):
    ms = jnp.mean(x * x, axis=-1, keepdims=True)
    return x * lax.rsqrt(ms + RMS_EPS) * gain


def _sigmoid(x):
    return 1.0 / (1.0 + jnp.exp(-x))


def _split_hi_lo(x):
    hi = lax.bitcast_convert_type(lax.bitcast_convert_type(x, U32) & jnp.uint32(0xFFFF0000), F32)
    return hi.astype(BF16), (x - hi).astype(BF16)


def _pack_bf16_pairs(x):
    k = x.shape[1] // 2
    bits = lax.bitcast_convert_type(x, U32)
    bits = bits + jnp.uint32(0x7FFF) + ((bits >> 16) & jnp.uint32(1))
    return (bits[:, k:] & jnp.uint32(0xFFFF0000)) | (bits[:, :k] >> 16)


def _unpack_bf16_pairs(w):
    lo = lax.bitcast_convert_type(w << 16, F32)
    hi = lax.bitcast_convert_type(w & jnp.uint32(0xFFFF0000), F32)
    return lo, hi


def _norm_proj_kernel(x_ref, g_ref, w_ref, o_ref, h_scr):
    @pl.when(pl.program_id(1) == 0)
    def _():
        h_scr[...] = _rms_norm(x_ref[...], g_ref[...]).astype(BF16)

    acc = jnp.dot(h_scr[...], w_ref[...], preferred_element_type=F32)
    for c in range(o_ref.shape[0]):
        o_ref[c] = acc[:, c * LANE:(c + 1) * LANE].astype(o_ref.dtype)


def norm_proj(x, gain, w, out_dtype, tm, tn=1024):
    m, d = x.shape
    n_out = w.shape[1]
    assert m % tm == 0 and n_out % tn == 0 and tn % LANE == 0
    return pl.pallas_call(
        _norm_proj_kernel,
        grid=(m // tm, n_out // tn),
        in_specs=[pl.BlockSpec((tm, d), lambda i, j: (i, 0)),
                  pl.BlockSpec((1, d), lambda i, j: (0, 0)),
                  pl.BlockSpec((d, tn), lambda i, j: (0, j))],
        out_specs=pl.BlockSpec((tn // LANE, tm, LANE), lambda i, j: (j, i, 0)),
        out_shape=jax.ShapeDtypeStruct((n_out // LANE, m, LANE), out_dtype),
        scratch_shapes=[pltpu.VMEM((tm, d), BF16)],
        compiler_params=_cparams("parallel", "arbitrary"),
        name="norm_proj",
    )(x, gain.reshape(1, d), w)


def _na_kernel(ks_ref, ty_ref, pi_ref, q_ref, k_ref, v_ref, b_ref, o_ref, *, nqb, scale):
    step = pl.program_id(1) * nqb + pl.program_id(2)
    start = pl.multiple_of(ks_ref[step] * GRID_W, GRID_W)
    nk = NA_K_ROWS * GRID_W
    npair = NA_K_ROWS // 2
    ty = ty_ref[step]
    for h in range(q_ref.shape[0]):
        q = q_ref[h]
        k = k_ref[h, pl.ds(start, nk), :]
        v = v_ref[h, pl.ds(start, nk), :]
        s = lax.dot_general(q, k, (((1,), (1,)), ((), ())), preferred_element_type=F32)
        bias = jnp.concatenate(
            [jnp.concatenate([b_ref[h, pi_ref[(ty * NA_Q_ROWS + i) * npair + jp]] for jp in range(npair)], axis=1)
             for i in range(NA_Q_ROWS)], axis=0)
        s = s * scale + bias
        m = jnp.max(s, axis=-1, keepdims=True)
        p = jnp.exp(s - m)
        l = jnp.sum(p, axis=-1, keepdims=True)
        o = jnp.dot(p.astype(BF16), v, preferred_element_type=F32) / l
        o_ref[:, h * LANE:(h + 1) * LANE] = o.astype(o_ref.dtype)


def _na_tables(seq_lens):
    ks, ty = [], []
    tok = 0
    per_super = SUPER // GRID_W
    for length in seq_lens:
        rows = length // GRID_W
        assert length % (NA_Q_ROWS * GRID_W) == 0 and rows >= 3 * NA_Q_ROWS
        assert tok // SUPER == (tok + length - 1) // SUPER
        row0 = (tok % SUPER) // GRID_W
        for qb in range(rows // NA_Q_ROWS):
            ks.append(row0 + int(np.clip(qb * NA_Q_ROWS - ROW_WIN // 2, 0, rows - NA_K_ROWS)))
            ty.append(0 if qb == 0 else (2 if qb == rows // NA_Q_ROWS - 1 else 1))
        tok += length
    assert tok % SUPER == 0 and len(ks) == (tok // SUPER) * (per_super // NA_Q_ROWS)
    return np.asarray(ks, np.int32), np.asarray(ty, np.int32)


def _na_bias(rpb):
    heads = rpb.shape[0]
    cols = np.arange(GRID_W)
    c_start = np.clip(cols - COL_WIN // 2, 0, GRID_W - COL_WIN)
    in_win = (cols[None, :] >= c_start[:, None]) & (cols[None, :] < c_start[:, None] + COL_WIN)
    dc = np.clip(cols[None, :] - cols[:, None], -(COL_WIN - 1), COL_WIN - 1) + COL_WIN - 1
    col_bias = jnp.where(jnp.asarray(in_win), rpb.astype(F32)[:, :, dc], NEG_INF)
    n_dr = 2 * ROW_WIN - 1
    masked = jnp.full((heads, n_dr, GRID_W, GRID_W), NEG_INF, F32)
    left = jnp.concatenate([col_bias[:, :n_dr - 1], masked, col_bias, masked[:, :1]], axis=1)
    right = jnp.concatenate([col_bias[:, 1:], col_bias, masked, masked[:, :1]], axis=1)
    pairs = jnp.concatenate([left, right], axis=-1)
    both, right_only, left_only, none = 0, n_dr - 1, 2 * n_dr - 1, 3 * n_dr - 1
    half = ROW_WIN // 2
    index = []
    for ty in range(3):
        for i in range(NA_Q_ROWS):
            q_rel = (i, i + half, i + ROW_WIN)[ty]
            w_rel = (max(i - half, 0), i, min(i + half, ROW_WIN))[ty]
            for jp in range(NA_K_ROWS // 2):
                j0, j1 = 2 * jp, 2 * jp + 1
                v0, v1 = w_rel <= j0 < w_rel + ROW_WIN, w_rel <= j1 < w_rel + ROW_WIN
                d0, d1 = j0 - q_rel + ROW_WIN - 1, j1 - q_rel + ROW_WIN - 1
                assert (not v0 or 0 <= d0 < n_dr) and (not v1 or 0 <= d1 < n_dr)
                index.append(both + d0 if v0 and v1 else right_only + d1 if v1 else left_only + d0 if v0 else none)
    return pairs, np.asarray(index, np.int32)


def neighbourhood_attention(qkv, rpb, seq_lens):
    heads = rpb.shape[0]
    n = qkv.shape[1]
    nsup = n // SUPER
    tq = NA_Q_ROWS * GRID_W
    nqb = SUPER // tq
    ks_tab, ty_tab = _na_tables(seq_lens)
    pairs, pair_index = _na_bias(rpb)
    hp = NA_HEADS_PER_STEP
    assert heads % hp == 0
    ng = heads // hp
    grid_spec = pltpu.PrefetchScalarGridSpec(
        num_scalar_prefetch=3,
        grid=(ng, nsup, nqb),
        in_specs=[
            pl.BlockSpec((hp, tq, LANE), lambda h, s, b, ks, ty, pi: (h, s * nqb + b, 0)),
            pl.BlockSpec((hp, SUPER, LANE), lambda h, s, b, ks, ty, pi: (ng + h, s, 0)),
            pl.BlockSpec((hp, SUPER, LANE), lambda h, s, b, ks, ty, pi: (2 * ng + h, s, 0)),
            pl.BlockSpec((hp,) + pairs.shape[1:], lambda h, s, b, ks, ty, pi: (h, 0, 0, 0)),
        ],
        out_specs=pl.BlockSpec((tq, hp * LANE), lambda h, s, b, ks, ty, pi: (s * nqb + b, h)),
    )
    return pl.pallas_call(
        functools.partial(_na_kernel, nqb=nqb, scale=NA_HEAD_DIM ** -0.5),
        grid_spec=grid_spec,
        out_shape=jax.ShapeDtypeStruct((n, heads * LANE), BF16),
        compiler_params=_cparams("parallel", "arbitrary", "arbitrary"),
        name="neighbourhood_attention",
    )(jnp.asarray(ks_tab), jnp.asarray(ty_tab), jnp.asarray(pair_index), qkv, qkv, qkv, pairs)


def _mem_attn_kernel(seq_ref, q_ref, kv_ref, o_ref, *, scale):
    del seq_ref
    hd = 2 * LANE
    for h in range(MEM_HEADS):
        q = jnp.concatenate([q_ref[2 * h], q_ref[2 * h + 1]], axis=-1).astype(BF16)
        k = jnp.concatenate([kv_ref[2 * h], kv_ref[2 * h + 1]], axis=-1)
        v = jnp.concatenate([kv_ref[2 * MEM_HEADS + 2 * h], kv_ref[2 * MEM_HEADS + 2 * h + 1]], axis=-1)
        s = lax.dot_general(q, k, (((1,), (1,)), ((), ())), preferred_element_type=F32) * scale
        m = jnp.max(s, axis=-1, keepdims=True)
        p = jnp.exp(s - m)
        l = jnp.sum(p, axis=-1, keepdims=True)
        o = jnp.dot(p.astype(BF16), v, preferred_element_type=F32) / l
        o_ref[:, h * hd:(h + 1) * hd] = o.astype(o_ref.dtype)


def memory_attention(proj, q_slab0, kv, blk_seq, n_mem, tq=512):
    n = proj.shape[1]
    nq = 2 * MEM_HEADS
    assert q_slab0 % nq == 0
    grid_spec = pltpu.PrefetchScalarGridSpec(
        num_scalar_prefetch=1,
        grid=(n // tq,),
        in_specs=[pl.BlockSpec((nq, tq, LANE), lambda i, sq: (q_slab0 // nq, i, 0)),
                  pl.BlockSpec((2 * nq, n_mem, LANE), lambda i, sq: (0, sq[i], 0))],
        out_specs=pl.BlockSpec((tq, nq * LANE), lambda i, sq: (i, 0)),
    )
    return pl.pallas_call(
        functools.partial(_mem_attn_kernel, scale=(2 * LANE) ** -0.5),
        grid_spec=grid_spec,
        out_shape=jax.ShapeDtypeStruct((n, nq * LANE), BF16),
        compiler_params=_cparams("parallel"),
        name="memory_attention",
    )(blk_seq, proj, kv)


def _s5_param_kernel(are_ref, aim_ref, ldt_ref, bre_ref, bim_ref, lre_ref, lim_ref, bbre_ref, bbim_ref):
    a_re = are_ref[...]
    a_im = aim_ref[...]
    dt = jnp.exp(ldt_ref[...])
    mag = jnp.exp(a_re * dt)
    ang = a_im * dt
    lam_re = mag * jnp.cos(ang)
    lam_im = mag * jnp.sin(ang)
    den = a_re * a_re + a_im * a_im
    n_re = lam_re - 1.0
    coef_re = (n_re * a_re + lam_im * a_im) / den
    coef_im = (lam_im * a_re - n_re * a_im) / den
    lre_ref[...] = lam_re
    lim_ref[...] = lam_im
    for c in range(bre_ref.shape[0]):
        b_re = bre_ref[c]
        b_im = bim_ref[c]
        bbre_ref[c] = coef_re * b_re - coef_im * b_im
        bbim_ref[c] = coef_re * b_im + coef_im * b_re


def s5_discretise(a_re, a_im, log_dt, b_re, b_im):
    two, g, p = a_re.shape
    c = b_re.shape[-1]
    rows = two * g
    ldt = jnp.broadcast_to(log_dt.reshape(rows, 1), (rows, p))
    b_re_t = b_re.reshape(rows, p, c).transpose(2, 0, 1)
    b_im_t = b_im.reshape(rows, p, c).transpose(2, 0, 1)
    return pl.pallas_call(
        _s5_param_kernel,
        out_shape=(jax.ShapeDtypeStruct((rows, p), F32), jax.ShapeDtypeStruct((rows, p), F32),
                   jax.ShapeDtypeStruct((c, rows, p), F32), jax.ShapeDtypeStruct((c, rows, p), F32)),
        name="s5_discretise",
    )(a_re.reshape(rows, p), a_im.reshape(rows, p), ldt, b_re_t, b_im_t)


def _s5_operands(lam_re, lam_im, bb_re, bb_im, c_re, c_im, d):
    c, rows, p = bb_re.shape
    g = rows // 2
    gs = LANE // c
    ns = g // gs
    eye = jnp.eye(gs, dtype=F32)

    def bmat(bb):
        x = bb.reshape(c, 2, ns, gs, p).transpose(1, 2, 3, 0, 4)
        return jnp.einsum('dsgcp,gh->dsgchp', x, eye).reshape(2, ns, gs * c, gs * p)

    def cmat(cc):
        x = cc.astype(F32).reshape(2, ns, gs, c, p)
        return jnp.einsum('dsgcp,gh->dsgphc', x, eye).reshape(2, ns, gs * p, gs * c)

    b_op = jnp.concatenate([bmat(bb_re), bmat(bb_im)], axis=-1).astype(BF16)
    c_op = jnp.concatenate([cmat(c_re), -cmat(c_im)], axis=-2).astype(BF16)
    lam = jnp.concatenate([lam_re.reshape(2, ns, gs * p), lam_im.reshape(2, ns, gs * p)], axis=-1)
    lam = jnp.broadcast_to(lam[:, :, None, :], (2, ns, SUBLANE, 2 * gs * p))
    d_op = d.astype(F32).reshape(ns, 1, gs * c)
    return b_op, c_op, lam, d_op


def _s5_kernel(keep_ref, u_hbm, bm_ref, cm_ref, lam_ref, d_ref, o_ref,
               u_scr, u_sem, up_scr, y_scr, y2_scr, bu0_scr, bu1_scr, xs0_scr, xs1_scr, e_scr, i_scr, *, lseg, tblk):
    sup = pl.program_id(0)
    nblk = lseg // tblk
    rb = SUBLANE * tblk
    ns = bu0_scr.shape[1] // 2
    nchunk = ns // LANE
    d_row = d_ref[0]
    pitch = u_scr.shape[1] // S5_SEGMENTS
    n_slab = pl.num_programs(1)
    lin = sup * n_slab + pl.program_id(1)
    slot = lin % 2

    def segment_copies(step, slot):
        b = step // n_slab
        s = step % n_slab
        return [pltpu.make_async_copy(u_hbm.at[s, pl.ds(b * (S5_SEGMENTS * lseg) + r * lseg, lseg), :],
                                      u_scr.at[slot, pl.ds(r * pitch, lseg), :], u_sem.at[slot])
                for r in range(S5_SEGMENTS)]

    @pl.when(lin == 0)
    def _():
        for cp in segment_copies(lin, slot):
            cp.start()

    for cp in segment_copies(lin, slot):
        cp.wait()

    @pl.when(lin + 1 < pl.num_programs(0) * n_slab)
    def _():
        for cp in segment_copies(lin + 1, 1 - slot):
            cp.start()

    u2 = u_scr.at[slot]

    def permute_in(t2, carry):
        t = t2 * 2
        a = u2[pl.ds(t, SUBLANE, stride=pitch), :]
        b = u2[pl.ds(t + 1, SUBLANE, stride=pitch), :]
        ab = jnp.concatenate([a, b], axis=0)
        rows = pl.ds(pl.multiple_of(t * SUBLANE, 2 * SUBLANE), 2 * SUBLANE)
        up_scr[rows, :] = ab.astype(BF16)
        y_scr[rows, :] = ab * d_row
        return carry
    lax.fori_loop(0, lseg // 2, permute_in, 0, unroll=4)
    xs1_scr[...] = jnp.zeros(xs1_scr.shape, xs1_scr.dtype)

    for direction in range(2):
        rev = direction == 1
        lam_r = [lam_ref[direction, 0, :, c * LANE:(c + 1) * LANE] for c in range(nchunk)]
        lam_i = [lam_ref[direction, 0, :, ns + c * LANE:ns + (c + 1) * LANE] for c in range(nchunk)]
        steps = list(range(tblk))[::-1] if rev else list(range(tblk))

        def block_rows(k, rev=rev):
            k = jnp.clip(k, 0, nblk - 1)
            blk = (nblk - 1 - k) if rev else k
            return pl.ds(pl.multiple_of(blk * rb, rb), rb)

        def bu_dot(k, direction=direction, block_rows=block_rows):
            return jnp.dot(up_scr[block_rows(k), :], bm_ref[direction, 0], preferred_element_type=F32)

        def scan_block(bu, x, xs, lam_r=lam_r, lam_i=lam_i, steps=steps):
            xr, xi = list(x[:nchunk]), list(x[nchunk:])
            held = None
            for n, t in enumerate(steps):
                rows = slice(t * SUBLANE, (t + 1) * SUBLANE)
                for c in range(nchunk):
                    b_r = bu[rows, c * LANE:(c + 1) * LANE]
                    b_i = bu[rows, ns + c * LANE:ns + (c + 1) * LANE]
                    r = lam_r[c] * xr[c] - lam_i[c] * xi[c] + b_r
                    i = lam_r[c] * xi[c] + lam_i[c] * xr[c] + b_i
                    xr[c], xi[c] = r, i
                if xs is None:
                    continue
                if n % 2 == 0:
                    held = (t, list(xr), list(xi))
                    continue
                lo, hi = (held, (t, xr, xi)) if held[0] < t else ((t, xr, xi), held)
                pair = slice(lo[0] * SUBLANE, (lo[0] + 2) * SUBLANE)
                for c in range(nchunk):
                    xs[pair, c * LANE:(c + 1) * LANE] = jnp.concatenate([lo[1][c], hi[1][c]], axis=0).astype(BF16)
                    xs[pair, ns + c * LANE:ns + (c + 1) * LANE] = (
                        jnp.concatenate([lo[2][c], hi[2][c]], axis=0).astype(BF16))
            return tuple(xr + xi)

        def pass_a(k2, x, bu_dot=bu_dot, scan_block=scan_block):
            bu1_scr[...] = bu_dot(2 * k2 + 1)
            x = scan_block(bu0_scr, x, None)
            bu0_scr[...] = bu_dot(2 * k2 + 2)
            return scan_block(bu1_scr, x, None)
        bu0_scr[...] = bu_dot(0)
        zero = tuple(jnp.zeros((SUBLANE, LANE), F32) for _ in range(2 * nchunk))
        ends = lax.fori_loop(0, nblk // 2, pass_a, zero)
        for c in range(2 * nchunk):
            e_scr[:, c * LANE:(c + 1) * LANE] = ends[c]

        p_r = lam_ref[direction, 0, 0:1, 0:ns]
        p_i = lam_ref[direction, 0, 0:1, ns:2 * ns]
        for _ in range(int(math.log2(lseg))):
            p_r, p_i = p_r * p_r - p_i * p_i, 2.0 * p_r * p_i
        order = list(range(S5_SEGMENTS))[::-1] if rev else list(range(S5_SEGMENTS))
        s_r = jnp.zeros((1, ns), F32)
        s_i = jnp.zeros((1, ns), F32)
        i_scr[order[0]:order[0] + 1, :] = jnp.zeros((1, 2 * ns), F32)
        for prev, cur in zip(order[:-1], order[1:]):
            e_r = e_scr[prev:prev + 1, 0:ns]
            e_i = e_scr[prev:prev + 1, ns:2 * ns]
            keep = keep_ref[(sup * 2 + direction) * S5_SEGMENTS + cur].astype(F32)
            n_r = (p_r * s_r - p_i * s_i + e_r) * keep
            n_i = (p_r * s_i + p_i * s_r + e_i) * keep
            i_scr[cur:cur + 1, 0:ns] = n_r
            i_scr[cur:cur + 1, ns:2 * ns] = n_i
            s_r, s_i = n_r, n_i

        def project(k, xs, scale, direction=direction, block_rows=block_rows):
            rows = block_rows(k)
            yb = jnp.dot(xs[...], cm_ref[direction, 0], preferred_element_type=F32)
            y_scr[rows, :] = y_scr[rows, :] + yb * scale

        def pass_b(k2, x, bu_dot=bu_dot, scan_block=scan_block, project=project):
            bu1_scr[...] = bu_dot(2 * k2 + 1)
            x = scan_block(bu0_scr, x, xs0_scr)
            project(2 * k2 - 1, xs1_scr, jnp.where(k2 > 0, 1.0, 0.0))
            bu0_scr[...] = bu_dot(2 * k2 + 2)
            x = scan_block(bu1_scr, x, xs1_scr)
            project(2 * k2, xs0_scr, 1.0)
            return x
        bu0_scr[...] = bu_dot(0)
        start = tuple(i_scr[:, c * LANE:(c + 1) * LANE] for c in range(2 * nchunk))
        lax.fori_loop(0, nblk // 2, pass_b, start)
        project(nblk - 1, xs1_scr, 1.0)

    def permute_out(t, carry):
        y = y_scr[pl.ds(pl.multiple_of(t * SUBLANE, SUBLANE), SUBLANE), :]
        gelu = 0.5 * y * (1.0 + jnp.tanh(math.sqrt(2.0 / math.pi) * (y + 0.044715 * (y * y * y))))
        y2_scr[pl.ds(t, SUBLANE, stride=pitch), :] = gelu
        return carry
    lax.fori_loop(0, lseg, permute_out, 0, unroll=8)
    for r in range(S5_SEGMENTS):
        o_ref[r * lseg:(r + 1) * lseg, :] = y2_scr[r * pitch:r * pitch + lseg, :].astype(o_ref.dtype)


def _s5_keep_table(seq_lens, lseg):
    starts, ends = set(), set()
    tok = 0
    for length in seq_lens:
        assert length % lseg == 0
        starts.add(tok // lseg)
        tok += length
        ends.add(tok // lseg - 1)
    nseg = tok // lseg
    assert nseg % S5_SEGMENTS == 0
    keep = np.ones((nseg // S5_SEGMENTS, 2, S5_SEGMENTS), np.int32)
    for s in range(nseg):
        if s in starts:
            keep[s // S5_SEGMENTS, 0, s % S5_SEGMENTS] = 0
        if s in ends:
            keep[s // S5_SEGMENTS, 1, s % S5_SEGMENTS] = 0
    return keep.reshape(-1)


def s5_mixer(proj, b_op, c_op, lam, d_op, seq_lens):
    n = proj.shape[1]
    ns = b_op.shape[1]
    nstate2 = b_op.shape[-1]
    nsup = n // SUPER
    lseg = SUPER // S5_SEGMENTS
    tblk = min(S5_TBLK, lseg)
    assert lseg & (lseg - 1) == 0 and lseg % (2 * tblk) == 0 and tblk % 2 == 0
    keep = _s5_keep_table(seq_lens, lseg)
    rows = SUBLANE * tblk
    pitch = lseg + SUBLANE
    grid_spec = pltpu.PrefetchScalarGridSpec(
        num_scalar_prefetch=1,
        grid=(nsup, ns),
        in_specs=[
            pl.BlockSpec(memory_space=pl.ANY),
            pl.BlockSpec((2, 1, LANE, nstate2), lambda b, s, kp: (0, s, 0, 0)),
            pl.BlockSpec((2, 1, nstate2, LANE), lambda b, s, kp: (0, s, 0, 0)),
            pl.BlockSpec((2, 1, SUBLANE, nstate2), lambda b, s, kp: (0, s, 0, 0)),
            pl.BlockSpec((1, 1, LANE), lambda b, s, kp: (s, 0, 0)),
        ],
        out_specs=pl.BlockSpec((SUPER, LANE), lambda b, s, kp: (b, s)),
        scratch_shapes=[
            pltpu.VMEM((2, S5_SEGMENTS * pitch, LANE), F32),
            pltpu.SemaphoreType.DMA((2,)),
            pltpu.VMEM((SUPER, LANE), BF16),
            pltpu.VMEM((SUPER, LANE), F32),
            pltpu.VMEM((S5_SEGMENTS * pitch, LANE), F32),
            pltpu.VMEM((rows, nstate2), F32),
            pltpu.VMEM((rows, nstate2), F32),
            pltpu.VMEM((rows, nstate2), BF16),
            pltpu.VMEM((rows, nstate2), BF16),
            pltpu.VMEM((SUBLANE, nstate2), F32),
            pltpu.VMEM((SUBLANE, nstate2), F32),
        ],
    )
    return pl.pallas_call(
        functools.partial(_s5_kernel, lseg=lseg, tblk=tblk),
        grid_spec=grid_spec,
        out_shape=jax.ShapeDtypeStruct((n, ns * LANE), BF16),
        compiler_params=_cparams("arbitrary", "arbitrary"),
        name="s5_mixer",
    )(jnp.asarray(keep), proj, b_op, c_op, lam, d_op)


def _glu_kernel(g_ref, w_ref, b_ref, o_ref):
    tn = o_ref.shape[1]
    col = pl.multiple_of(pl.program_id(1) * tn, tn)
    acc = jnp.dot(g_ref[...], w_ref[...], preferred_element_type=F32) + b_ref[...]
    g = g_ref[:, pl.ds(col, tn)].astype(F32)
    o_ref[...] = (g * _sigmoid(acc)).astype(o_ref.dtype)


def glu(g, w, b, tm=1024, tn=1024):
    m, k = g.shape
    assert m % tm == 0 and k % tn == 0
    return pl.pallas_call(
        _glu_kernel,
        grid=(m // tm, k // tn),
        in_specs=[pl.BlockSpec((tm, k), lambda i, j: (i, 0)),
                  pl.BlockSpec((k, tn), lambda i, j: (0, j)),
                  pl.BlockSpec((1, tn), lambda i, j: (0, j))],
        out_specs=pl.BlockSpec((tm, tn), lambda i, j: (i, j)),
        out_shape=jax.ShapeDtypeStruct((m, k), BF16),
        compiler_params=_cparams("parallel", "arbitrary"),
        name="glu",
    )(g, w, b.reshape(1, k).astype(F32))


def _out_proj_kernel(x_ref, tok_ref, mo_ref, w1_ref, w2_ref, o_ref):
    acc = jnp.dot(tok_ref[...], w1_ref[...], preferred_element_type=F32)
    acc = acc + jnp.dot(mo_ref[...], w2_ref[...], preferred_element_type=F32)
    o_ref[...] = x_ref[...] + acc


def out_proj(x, tok, mo, w, tm=1024, tn=512):
    m, d = x.shape
    k1, k2 = tok.shape[1], mo.shape[1]
    assert m % tm == 0 and d % tn == 0 and w.shape[0] == k1 + k2 and k1 % k2 == 0
    return pl.pallas_call(
        _out_proj_kernel,
        grid=(m // tm, d // tn),
        in_specs=[pl.BlockSpec((tm, tn), lambda i, j: (i, j)),
                  pl.BlockSpec((tm, k1), lambda i, j: (i, 0)),
                  pl.BlockSpec((tm, k2), lambda i, j: (i, 0)),
                  pl.BlockSpec((k1, tn), lambda i, j: (0, j)),
                  pl.BlockSpec((k2, tn), lambda i, j: (k1 // k2, j))],
        out_specs=pl.BlockSpec((tm, tn), lambda i, j: (i, j)),
        out_shape=jax.ShapeDtypeStruct((m, d), F32),
        compiler_params=_cparams("parallel", "arbitrary"),
        name="out_proj",
    )(x, tok, mo, w, w)


def _router_kernel(x_ref, g_ref, whi_ref, wlo_ref, b_ref, h_ref, meta_ref):
    t = _rms_norm(x_ref[...], g_ref[...])
    h_ref[...] = _pack_bf16_pairs(t)
    t_hi, t_lo = _split_hi_lo(t)
    logits = (jnp.dot(t_hi, whi_ref[...], preferred_element_type=F32)
              + jnp.dot(t_lo, whi_ref[...], preferred_element_type=F32)
              + jnp.dot(t_hi, wlo_ref[...], preferred_element_type=F32)) + b_ref[...]
    lane = lax.broadcasted_iota(jnp.int32, logits.shape, 1)
    big = jnp.int32(LANE)
    is_g = lane < N_EXPERT_GROUPS
    gl = jnp.where(is_g, logits, -jnp.inf)
    gmax = jnp.max(gl, axis=-1, keepdims=True)
    gsum = jnp.sum(jnp.where(is_g, jnp.exp(gl - gmax), 0.0), axis=-1, keepdims=True)
    g_val = 1.0 / gsum
    g_idx = jnp.min(jnp.where(gl == gmax, lane, big), axis=-1, keepdims=True)
    e_lane = lane - N_EXPERT_GROUPS
    in_grp = (e_lane >= 0) & (e_lane < N_EXPERTS) & ((e_lane >> 2) == g_idx)
    el = jnp.where(in_grp, logits, -jnp.inf)
    v1 = jnp.max(el, axis=-1, keepdims=True)
    i1 = jnp.min(jnp.where(el == v1, lane, big), axis=-1, keepdims=True)
    el2 = jnp.where(lane == i1, -jnp.inf, el)
    v2 = jnp.max(el2, axis=-1, keepdims=True)
    i2 = jnp.min(jnp.where(el2 == v2, lane, big), axis=-1, keepdims=True)
    z = jnp.exp(v2 - v1)
    w1 = g_val / (1.0 + z)
    w2 = g_val * z / (1.0 + z)
    e1 = (i1 - N_EXPERT_GROUPS).astype(F32)
    e2 = (i2 - N_EXPERT_GROUPS).astype(F32)
    meta = jnp.where(lane == 0, w1, jnp.where(lane == 1, w2, jnp.where(lane == 2, e1, jnp.where(lane == 3, e2, 0.0))))
    meta_ref[...] = meta


def router(x, gain, w_rg, b_rg, w_re, b_re, tm=512):
    m, d = x.shape
    assert EXPERTS_PER_GROUP == 4 and m % tm == 0
    pad = LANE - N_EXPERT_GROUPS - N_EXPERTS
    w = jnp.concatenate([w_rg, w_re, jnp.zeros((d, pad), F32)], axis=1).astype(F32)
    w_hi, w_lo = _split_hi_lo(w)
    b = jnp.concatenate([b_rg, b_re, jnp.zeros((pad,), F32)]).astype(F32).reshape(1, LANE)
    return pl.pallas_call(
        _router_kernel,
        grid=(m // tm,),
        in_specs=[pl.BlockSpec((tm, d), lambda i: (i, 0)),
                  pl.BlockSpec((1, d), lambda i: (0, 0)),
                  pl.BlockSpec((d, LANE), lambda i: (0, 0)),
                  pl.BlockSpec((d, LANE), lambda i: (0, 0)),
                  pl.BlockSpec((1, LANE), lambda i: (0, 0))],
        out_specs=(pl.BlockSpec((tm, d // 2), lambda i: (i, 0)),
                   pl.BlockSpec((tm, LANE), lambda i: (i, 0))),
        out_shape=(jax.ShapeDtypeStruct((m, d // 2), U32), jax.ShapeDtypeStruct((m, LANE), F32)),
        compiler_params=_cparams("parallel"),
        name="moe_router",
    )(x, gain.reshape(1, d), w_hi, w_lo, b)


def _routing_plan(meta, n_rows):
    n = meta.shape[0]
    e = meta[:, 2:4].astype(jnp.int32)
    onehot = (e[:, :, None] == jnp.arange(N_EXPERTS)[None, None, :]).any(axis=1).astype(jnp.int32)
    counts = onehot.sum(axis=0)
    rank = jnp.cumsum(onehot, axis=0) - onehot
    padded = ((counts + MOE_TILE - 1) // MOE_TILE) * MOE_TILE
    ends = jnp.cumsum(padded)
    offs = ends - padded
    pos = jnp.take_along_axis(offs[None, :] + rank, e, axis=1).astype(jnp.int32)
    tok = jnp.arange(n, dtype=jnp.int32)
    sorted_tok = jnp.zeros((n_rows,), jnp.int32).at[pos.T.reshape(-1)].set(jnp.concatenate([tok, tok]))
    tile_start = jnp.arange(n_rows // MOE_TILE, dtype=jnp.int32) * MOE_TILE
    tile_expert = jnp.minimum((ends[None, :] <= tile_start[:, None]).sum(axis=1), N_EXPERTS - 1).astype(jnp.int32)
    tile_valid = (tile_start < ends[-1]).astype(jnp.int32)
    return pos[:, 0], pos[:, 1], sorted_tok, tile_expert, tile_valid


def _row_gather_start(idx_ref, base, src_ref, dst_ref, sem):
    for r in range(dst_ref.shape[0]):
        pltpu.make_async_copy(src_ref.at[pl.ds(idx_ref[base + r], 1), :], dst_ref.at[pl.ds(r, 1), :], sem).start()


def _row_gather_wait(src_ref, dst_ref, sem):
    pltpu.make_async_copy(src_ref.at[pl.ds(0, dst_ref.shape[0]), :], dst_ref, sem).wait()


def _moe_up_kernel(st_ref, te_ref, tv_ref, h_ref, wg_ref, wu_ref, o_ref, xbuf0, xbuf1, sem):
    del te_ref
    t = pl.program_id(0)
    last = pl.num_programs(0) - 1
    tile = xbuf0.shape[0]

    @pl.when(t == 0)
    def _():
        _row_gather_start(st_ref, 0, h_ref, xbuf0, sem.at[0])

    def run(cur, cur_sem, nxt_buf, nxt_sem):
        _row_gather_wait(h_ref, cur, cur_sem)
        lo, hi = _unpack_bf16_pairs(cur[...])
        lo = lo.astype(BF16)
        hi = hi.astype(BF16)
        _row_gather_start(st_ref, jnp.minimum(t + 1, last) * tile, h_ref, nxt_buf, nxt_sem)
        half = lo.shape[1]
        g = jnp.dot(lo, wg_ref[0, :half], preferred_element_type=F32)
        g = g + jnp.dot(hi, wg_ref[0, half:], preferred_element_type=F32)
        u = jnp.dot(lo, wu_ref[0, :half], preferred_element_type=F32)
        u = u + jnp.dot(hi, wu_ref[0, half:], preferred_element_type=F32)
        o_ref[...] = (g * _sigmoid(g) * u).astype(o_ref.dtype)

        @pl.when(t == last)
        def _():
            _row_gather_wait(h_ref, nxt_buf, nxt_sem)

    valid = tv_ref[t] == 1

    @pl.when(valid & (t % 2 == 0))
    def _():
        run(xbuf0, sem.at[0], xbuf1, sem.at[1])

    @pl.when(valid & (t % 2 == 1))
    def _():
        run(xbuf1, sem.at[1], xbuf0, sem.at[0])

    @pl.when(tv_ref[t] == 0)
    def _():
        @pl.when((tv_ref[jnp.maximum(t - 1, 0)] == 1) & (t % 2 == 0))
        def _():
            _row_gather_wait(h_ref, xbuf0, sem.at[0])

        @pl.when((tv_ref[jnp.maximum(t - 1, 0)] == 1) & (t % 2 == 1))
        def _():
            _row_gather_wait(h_ref, xbuf1, sem.at[1])
        o_ref[...] = jnp.zeros(o_ref.shape, o_ref.dtype)


def moe_up(h, w_gate, w_up, sorted_tok, tile_expert, tile_valid):
    half = h.shape[1]
    d = 2 * half
    f = w_gate.shape[-1]
    n_tiles = tile_expert.shape[0]
    grid_spec = pltpu.PrefetchScalarGridSpec(
        num_scalar_prefetch=3,
        grid=(n_tiles,),
        in_specs=[pl.BlockSpec(memory_space=pl.ANY),
                  pl.BlockSpec((1, d, f), lambda t, st, te, tv: (te[t], 0, 0)),
                  pl.BlockSpec((1, d, f), lambda t, st, te, tv: (te[t], 0, 0))],
        out_specs=pl.BlockSpec((MOE_TILE, f), lambda t, st, te, tv: (t, 0)),
        scratch_shapes=[pltpu.VMEM((MOE_TILE, half), U32), pltpu.VMEM((MOE_TILE, half), U32),
                        pltpu.SemaphoreType.DMA((2,))],
    )
    return pl.pallas_call(
        _moe_up_kernel,
        grid_spec=grid_spec,
        out_shape=jax.ShapeDtypeStruct((n_tiles * MOE_TILE, f), BF16),
        compiler_params=_cparams("arbitrary"),
        name="moe_up",
    )(sorted_tok, tile_expert, tile_valid, h, w_gate, w_up)


def _moe_down_kernel(te_ref, tv_ref, h_ref, wd_ref, o_ref):
    del te_ref
    t = pl.program_id(0)

    @pl.when(tv_ref[t] == 1)
    def _():
        y = jnp.dot(h_ref[...], wd_ref[0], preferred_element_type=F32)
        o_ref[...] = _pack_bf16_pairs(y)

    @pl.when(tv_ref[t] == 0)
    def _():
        o_ref[...] = jnp.zeros(o_ref.shape, o_ref.dtype)


def moe_down(hid, w_down, tile_expert, tile_valid):
    n_rows, f = hid.shape
    d = w_down.shape[-1]
    grid_spec = pltpu.PrefetchScalarGridSpec(
        num_scalar_prefetch=2,
        grid=(n_rows // MOE_TILE,),
        in_specs=[pl.BlockSpec((MOE_TILE, f), lambda t, te, tv: (t, 0)),
                  pl.BlockSpec((1, f, d), lambda t, te, tv: (te[t], 0, 0))],
        out_specs=pl.BlockSpec((MOE_TILE, d // 2), lambda t, te, tv: (t, 0)),
    )
    return pl.pallas_call(
        _moe_down_kernel,
        grid_spec=grid_spec,
        out_shape=jax.ShapeDtypeStruct((n_rows, d // 2), U32),
        compiler_params=_cparams("arbitrary"),
        name="moe_down",
    )(tile_expert, tile_valid, hid, w_down)


def _combine_kernel(p0_ref, p1_ref, x_ref, meta_ref, g_ref, ys_ref, o_ref, buf0, buf1, sem, *, base, final_norm):
    i = pl.program_id(0)
    last = pl.num_programs(0) - 1
    tc = buf0.shape[1]

    def start(tile, buf, sems):
        n0 = base + tile * tc
        _row_gather_start(p0_ref, n0, ys_ref, buf.at[0], sems.at[0])
        _row_gather_start(p1_ref, n0, ys_ref, buf.at[1], sems.at[1])

    def wait(buf, sems):
        _row_gather_wait(ys_ref, buf.at[0], sems.at[0])
        _row_gather_wait(ys_ref, buf.at[1], sems.at[1])

    @pl.when(i == 0)
    def _():
        start(0, buf0, sem.at[0])

    def run(cur, cur_sems, nxt, nxt_sems):
        wait(cur, cur_sems)
        lo0, hi0 = _unpack_bf16_pairs(cur[0])
        lo1, hi1 = _unpack_bf16_pairs(cur[1])
        start(jnp.minimum(i + 1, last), nxt, nxt_sems)
        meta = meta_ref[...]
        w0 = meta[:, 0:1]
        w1 = meta[:, 1:2]
        half = lo0.shape[1]
        y_lo = x_ref[:, :half] + w0 * lo0 + w1 * lo1
        y_hi = x_ref[:, half:] + w0 * hi0 + w1 * hi1
        if final_norm:
            ms = (jnp.sum(y_lo * y_lo, axis=-1, keepdims=True)
                  + jnp.sum(y_hi * y_hi, axis=-1, keepdims=True)) / (2 * half)
            inv = lax.rsqrt(ms + RMS_EPS)
            y_lo = y_lo * inv * g_ref[:, :half]
            y_hi = y_hi * inv * g_ref[:, half:]
        o_ref[:, :half] = y_lo
        o_ref[:, half:] = y_hi

        @pl.when(i == last)
        def _():
            wait(nxt, nxt_sems)

    @pl.when(i % 2 == 0)
    def _():
        run(buf0, sem.at[0], buf1, sem.at[1])

    @pl.when(i % 2 == 1)
    def _():
        run(buf1, sem.at[1], buf0, sem.at[0])


def moe_combine(x, meta, ys, pos0, pos1, gain, *, base, count, final_norm, tc=256):
    d = x.shape[1]
    assert base % tc == 0 and count % tc == 0
    blk0 = base // tc
    grid_spec = pltpu.PrefetchScalarGridSpec(
        num_scalar_prefetch=2,
        grid=(count // tc,),
        in_specs=[pl.BlockSpec((tc, d), lambda i, p0, p1: (blk0 + i, 0)),
                  pl.BlockSpec((tc, LANE), lambda i, p0, p1: (blk0 + i, 0)),
                  pl.BlockSpec((1, d), lambda i, p0, p1: (0, 0)),
                  pl.BlockSpec(memory_space=pl.ANY)],
        out_specs=pl.BlockSpec((tc, d), lambda i, p0, p1: (i, 0)),
        scratch_shapes=[pltpu.VMEM((2, tc, d // 2), U32), pltpu.VMEM((2, tc, d // 2), U32),
                        pltpu.SemaphoreType.DMA((2, 2))],
    )
    return pl.pallas_call(
        functools.partial(_combine_kernel, base=base, final_norm=final_norm),
        grid_spec=grid_spec,
        out_shape=jax.ShapeDtypeStruct((count, d), F32),
        compiler_params=_cparams("arbitrary"),
        name="moe_combine",
    )(pos0, pos1, x, meta, gain.reshape(1, d), ys)


def moe_experts(x, gain, w_rg, b_rg, w_re, b_re, w_gate, w_up, w_down, layer):
    n = x.shape[0]
    n_rows = 2 * n + N_EXPERTS * MOE_TILE
    assert (2 * n) % MOE_TILE == 0
    h, meta = router(x, gain, w_rg, b_rg, w_re, b_re)
    pos0, pos1, sorted_tok, tile_expert, tile_valid = _routing_plan(meta, n_rows)
    tile_expert = tile_expert + layer * N_EXPERTS
    hid = moe_up(h, w_gate, w_up, sorted_tok, tile_expert, tile_valid)
    ys = moe_down(hid, w_down, tile_expert, tile_valid)
    return meta, ys, pos0, pos1


def kernel(x_prompt, x_sample, mem_prompt, mem_sample, norm_mix, norm_mem, norm_ffn, norm_final, na_w_in, na_rpb, s5_w_in, s5_a_re, s5_a_im, s5_log_dt, s5_b_re, s5_b_im, s5_c_re, s5_c_im, s5_d, s5_w_glu, s5_b_glu, mem_w_kv, w_out, moe_w_rg, moe_b_rg, moe_w_re, moe_b_re, moe_w_gate, moe_w_up, moe_w_down):
    d = x_prompt.shape[-1]
    groups = [x_prompt, x_sample]
    mems = [mem_prompt, mem_sample]
    seq_lens = [g.shape[1] for g in groups for _ in range(g.shape[0])]
    n_mem = mem_prompt.shape[1]
    depth = norm_mix.shape[0]
    x = jnp.concatenate([g.reshape(-1, d) for g in groups], axis=0)
    mem = jnp.concatenate([m.reshape(-1, d) for m in mems], axis=0)
    n = x.shape[0]
    tq_mem = 512
    blk_seq = []
    for s, length in enumerate(seq_lens):
        assert length % tq_mem == 0
        blk_seq += [s] * (length // tq_mem)
    blk_seq = jnp.asarray(np.asarray(blk_seq, np.int32))
    na_heads = na_rpb.shape[1]
    w_gate_all = moe_w_gate.astype(BF16).reshape((-1,) + moe_w_gate.shape[2:])
    w_up_all = moe_w_up.astype(BF16).reshape((-1,) + moe_w_up.shape[2:])
    w_down_all = moe_w_down.astype(BF16).reshape((-1,) + moe_w_down.shape[2:])

    meta = ys = pos0 = pos1 = None
    for i in range(depth):
        j = i // 2
        if i > 0:
            x = moe_combine(x, meta, ys, pos0, pos1, norm_final, base=0, count=n, final_norm=False)
        kv = norm_proj(mem, norm_mem[i], mem_w_kv[i].astype(BF16), BF16, tm=n_mem)
        if i % 2 == 0:
            proj = norm_proj(x, norm_mix[i], na_w_in[j].astype(BF16), BF16, tm=512)
            tok = neighbourhood_attention(proj, na_rpb[j], seq_lens)
            q_slab0 = 3 * na_heads
        else:
            proj = norm_proj(x, norm_mix[i], s5_w_in[j].astype(BF16), F32, tm=512)
            lam_re, lam_im, bb_re, bb_im = s5_discretise(s5_a_re[j], s5_a_im[j], s5_log_dt[j], s5_b_re[j], s5_b_im[j])
            b_op, c_op, lam, d_op = _s5_operands(lam_re, lam_im, bb_re, bb_im, s5_c_re[j], s5_c_im[j], s5_d[j])
            g = s5_mixer(proj, b_op, c_op, lam, d_op, seq_lens)
            tok = glu(g, s5_w_glu[j].astype(BF16), s5_b_glu[j])
            q_slab0 = b_op.shape[1]
        mo = memory_attention(proj, q_slab0, kv, blk_seq, n_mem, tq=tq_mem)
        x = out_proj(x, tok, mo, w_out[i].astype(BF16))
        meta, ys, pos0, pos1 = moe_experts(x, norm_ffn[i], moe_w_rg[i], moe_b_rg[i], moe_w_re[i], moe_b_re[i],
                                           w_gate_all, w_up_all, w_down_all, i)
    outs, base = [], 0
    for g in groups:
        count = g.shape[0] * g.shape[1]
        y = moe_combine(x, meta, ys, pos0, pos1, norm_final, base=base, count=count, final_norm=True)
        outs.append(y.reshape(g.shape))
        base += count
    return tuple(outs)
```

```python
import functools
import math

import numpy as np
import jax
import jax.numpy as jnp
from jax import lax
from jax.experimental import pallas as pl
from jax.experimental.pallas import tpu as pltpu

F32 = jnp.float32
BF16 = jnp.bfloat16
U32 = jnp.uint32

LANE = 128
SUBLANE = 8
VMEM_LIMIT_BYTES = 56 * 1024 * 1024

RMS_EPS = 1e-6
NEG_INF = -1e30
GRID_W = 64
ROW_WIN = 8
COL_WIN = 16
NA_HEAD_DIM = 128
MEM_HEADS = 4
S5_GROUP = 16
S5_STATE = 64
N_EXPERT_GROUPS = 4
EXPERTS_PER_GROUP = 4
N_EXPERTS = N_EXPERT_GROUPS * EXPERTS_PER_GROUP

NA_Q_ROWS = 8
NA_K_ROWS = 2 * ROW_WIN
NA_SUB_K_ROWS = ROW_WIN + 2
NA_HEADS_PER_STEP = 2
SUPER = 8192
S5_SEGMENTS = SUBLANE
S5_TBLK = 64
MOE_TILE = 256


def _cparams(*sem):
    return pltpu.CompilerParams(dimension_semantics=sem, vmem_limit_bytes=VMEM_LIMIT_BYTES)


def _rms_norm(x, gain):
    ms = jnp.mean(x * x, axis=-1, keepdims=True)
    return x * lax.rsqrt(ms + RMS_EPS) * gain


def _sigmoid(x):
    return 1.0 / (1.0 + jnp.exp(-x))


def _split_hi_lo(x):
    hi = lax.bitcast_convert_type(lax.bitcast_convert_type(x, U32) & jnp.uint32(0xFFFF0000), F32)
    return hi.astype(BF16), (x - hi).astype(BF16)


def _pack_bf16_pairs(x):
    k = x.shape[1] // 2
    bits = lax.bitcast_convert_type(x, U32)
    return (bits[:, k:] & jnp.uint32(0xFFFF0000)) | (bits[:, :k] >> 16)


def _unpack_bf16_pairs(w):
    lo = lax.bitcast_convert_type(w << 16, F32)
    hi = lax.bitcast_convert_type(w & jnp.uint32(0xFFFF0000), F32)
    return lo, hi


def _norm_proj_kernel(x_ref, g_ref, w_ref, o_ref, h_scr):
    @pl.when(pl.program_id(1) == 0)
    def _():
        h_scr[...] = _rms_norm(x_ref[...], g_ref[...]).astype(BF16)

    acc = jnp.dot(h_scr[...], w_ref[...], preferred_element_type=F32)
    for c in range(o_ref.shape[0]):
        o_ref[c] = acc[:, c * LANE:(c + 1) * LANE].astype(o_ref.dtype)


def norm_proj(x, gain, w, out_dtype, tm, tn=1024):
    m, d = x.shape
    n_out = w.shape[1]
    assert m % tm == 0 and n_out % tn == 0 and tn % LANE == 0
    return pl.pallas_call(
        _norm_proj_kernel,
        grid=(m // tm, n_out // tn),
        in_specs=[pl.BlockSpec((tm, d), lambda i, j: (i, 0)),
                  pl.BlockSpec((1, d), lambda i, j: (0, 0)),
                  pl.BlockSpec((d, tn), lambda i, j: (0, j))],
        out_specs=pl.BlockSpec((tn // LANE, tm, LANE), lambda i, j: (j, i, 0)),
        out_shape=jax.ShapeDtypeStruct((n_out // LANE, m, LANE), out_dtype),
        scratch_shapes=[pltpu.VMEM((tm, d), BF16)],
        compiler_params=_cparams("parallel", "arbitrary"),
        name="norm_proj",
    )(x, gain.reshape(1, d), w)


def _na_kernel(ks_ref, ty_ref, pi_ref, ko_ref, q_ref, k_ref, v_ref, b_ref, o_ref, *, nqb, scale):
    step = pl.program_id(1) * nqb + pl.program_id(2)
    npair = NA_K_ROWS // 2
    ty = ty_ref[step]
    tq2 = 2 * GRID_W
    nk = NA_SUB_K_ROWS * GRID_W
    for h in range(q_ref.shape[0]):
        for a in range(NA_Q_ROWS // 2):
            koff = ko_ref[ty * (NA_Q_ROWS // 2) + a]
            start = pl.multiple_of((ks_ref[step] + 2 * koff) * GRID_W, GRID_W)
            q = q_ref[h, a * tq2:(a + 1) * tq2, :]
            k = k_ref[h, pl.ds(start, nk), :]
            v = v_ref[h, pl.ds(start, nk), :]
            s = lax.dot_general(q, k, (((1,), (1,)), ((), ())), preferred_element_type=F32)
            bias = jnp.concatenate(
                [jnp.concatenate([b_ref[h, pi_ref[(ty * NA_Q_ROWS + i) * npair + koff + jp]]
                                  for jp in range(NA_SUB_K_ROWS // 2)], axis=1)
                 for i in (2 * a, 2 * a + 1)], axis=0)
            s = s * scale + bias
            m = jnp.max(s, axis=-1, keepdims=True)
            p = jnp.exp(s - m)
            l = jnp.sum(p, axis=-1, keepdims=True)
            o = jnp.dot(p.astype(BF16), v, preferred_element_type=F32) / l
            o_ref[a * tq2:(a + 1) * tq2, h * LANE:(h + 1) * LANE] = o.astype(o_ref.dtype)


def _na_tables(seq_lens):
    ks, ty = [], []
    tok = 0
    per_super = SUPER // GRID_W
    for length in seq_lens:
        rows = length // GRID_W
        assert length % (NA_Q_ROWS * GRID_W) == 0 and rows >= 3 * NA_Q_ROWS
        assert tok // SUPER == (tok + length - 1) // SUPER
        row0 = (tok % SUPER) // GRID_W
        for qb in range(rows // NA_Q_ROWS):
            ks.append(row0 + int(np.clip(qb * NA_Q_ROWS - ROW_WIN // 2, 0, rows - NA_K_ROWS)))
            ty.append(0 if qb == 0 else (2 if qb == rows // NA_Q_ROWS - 1 else 1))
        tok += length
    assert tok % SUPER == 0 and len(ks) == (tok // SUPER) * (per_super // NA_Q_ROWS)
    return np.asarray(ks, np.int32), np.asarray(ty, np.int32)


def _na_bias(rpb):
    heads = rpb.shape[0]
    cols = np.arange(GRID_W)
    c_start = np.clip(cols - COL_WIN // 2, 0, GRID_W - COL_WIN)
    in_win = (cols[None, :] >= c_start[:, None]) & (cols[None, :] < c_start[:, None] + COL_WIN)
    dc = np.clip(cols[None, :] - cols[:, None], -(COL_WIN - 1), COL_WIN - 1) + COL_WIN - 1
    col_bias = jnp.where(jnp.asarray(in_win), rpb.astype(F32)[:, :, dc], NEG_INF)
    n_dr = 2 * ROW_WIN - 1
    masked = jnp.full((heads, n_dr, GRID_W, GRID_W), NEG_INF, F32)
    left = jnp.concatenate([col_bias[:, :n_dr - 1], masked, col_bias, masked[:, :1]], axis=1)
    right = jnp.concatenate([col_bias[:, 1:], col_bias, masked, masked[:, :1]], axis=1)
    pairs = jnp.concatenate([left, right], axis=-1)
    both, right_only, left_only, none = 0, n_dr - 1, 2 * n_dr - 1, 3 * n_dr - 1
    half = ROW_WIN // 2
    index = []
    for ty in range(3):
        for i in range(NA_Q_ROWS):
            q_rel = (i, i + half, i + ROW_WIN)[ty]
            w_rel = (max(i - half, 0), i, min(i + half, ROW_WIN))[ty]
            for jp in range(NA_K_ROWS // 2):
                j0, j1 = 2 * jp, 2 * jp + 1
                v0, v1 = w_rel <= j0 < w_rel + ROW_WIN, w_rel <= j1 < w_rel + ROW_WIN
                d0, d1 = j0 - q_rel + ROW_WIN - 1, j1 - q_rel + ROW_WIN - 1
                assert (not v0 or 0 <= d0 < n_dr) and (not v1 or 0 <= d1 < n_dr)
                index.append(both + d0 if v0 and v1 else right_only + d1 if v1 else left_only + d0 if v0 else none)
    first_pair = []
    for ty in range(3):
        for a in range(NA_Q_ROWS // 2):
            w_rel = [(max(i - half, 0), i, min(i + half, ROW_WIN))[ty] for i in (2 * a, 2 * a + 1)]
            first = min(min(w_rel) // 2 * 2, NA_K_ROWS - NA_SUB_K_ROWS)
            assert first <= min(w_rel) and max(w_rel) + ROW_WIN <= first + NA_SUB_K_ROWS
            first_pair.append(first // 2)
    return pairs, np.asarray(index, np.int32), np.asarray(first_pair, np.int32)


def neighbourhood_attention(qkv, rpb, seq_lens):
    heads = rpb.shape[0]
    n = qkv.shape[1]
    nsup = n // SUPER
    tq = NA_Q_ROWS * GRID_W
    nqb = SUPER // tq
    ks_tab, ty_tab = _na_tables(seq_lens)
    pairs, pair_index, first_pair = _na_bias(rpb)
    hp = NA_HEADS_PER_STEP
    assert heads % hp == 0
    ng = heads // hp
    grid_spec = pltpu.PrefetchScalarGridSpec(
        num_scalar_prefetch=4,
        grid=(ng, nsup, nqb),
        in_specs=[
            pl.BlockSpec((hp, tq, LANE), lambda h, s, b, ks, ty, pi, ko: (h, s * nqb + b, 0)),
            pl.BlockSpec((hp, SUPER, LANE), lambda h, s, b, ks, ty, pi, ko: (ng + h, s, 0)),
            pl.BlockSpec((hp, SUPER, LANE), lambda h, s, b, ks, ty, pi, ko: (2 * ng + h, s, 0)),
            pl.BlockSpec((hp,) + pairs.shape[1:], lambda h, s, b, ks, ty, pi, ko: (h, 0, 0, 0)),
        ],
        out_specs=pl.BlockSpec((tq, hp * LANE), lambda h, s, b, ks, ty, pi, ko: (s * nqb + b, h)),
    )
    return pl.pallas_call(
        functools.partial(_na_kernel, nqb=nqb, scale=NA_HEAD_DIM ** -0.5),
        grid_spec=grid_spec,
        out_shape=jax.ShapeDtypeStruct((n, heads * LANE), BF16),
        compiler_params=_cparams("parallel", "arbitrary", "arbitrary"),
        name="neighbourhood_attention",
    )(jnp.asarray(ks_tab), jnp.asarray(ty_tab), jnp.asarray(pair_index), jnp.asarray(first_pair),
      qkv, qkv, qkv, pairs)


def _mem_attn_kernel(seq_ref, q_ref, kv_ref, o_ref, *, scale):
    del seq_ref
    hd = 2 * LANE
    for h in range(MEM_HEADS):
        q = jnp.concatenate([q_ref[2 * h], q_ref[2 * h + 1]], axis=-1).astype(BF16)
        k = jnp.concatenate([kv_ref[2 * h], kv_ref[2 * h + 1]], axis=-1)
        v = jnp.concatenate([kv_ref[2 * MEM_HEADS + 2 * h], kv_ref[2 * MEM_HEADS + 2 * h + 1]], axis=-1)
        s = lax.dot_general(q, k, (((1,), (1,)), ((), ())), preferred_element_type=F32) * scale
        m = jnp.max(s, axis=-1, keepdims=True)
        p = jnp.exp(s - m)
        l = jnp.sum(p, axis=-1, keepdims=True)
        o = jnp.dot(p.astype(BF16), v, preferred_element_type=F32) / l
        o_ref[:, h * hd:(h + 1) * hd] = o.astype(o_ref.dtype)


def memory_attention(proj, q_slab0, kv, blk_seq, n_mem, tq=512):
    n = proj.shape[1]
    nq = 2 * MEM_HEADS
    assert q_slab0 % nq == 0
    grid_spec = pltpu.PrefetchScalarGridSpec(
        num_scalar_prefetch=1,
        grid=(n // tq,),
        in_specs=[pl.BlockSpec((nq, tq, LANE), lambda i, sq: (q_slab0 // nq, i, 0)),
                  pl.BlockSpec((2 * nq, n_mem, LANE), lambda i, sq: (0, sq[i], 0))],
        out_specs=pl.BlockSpec((tq, nq * LANE), lambda i, sq: (i, 0)),
    )
    return pl.pallas_call(
        functools.partial(_mem_attn_kernel, scale=(2 * LANE) ** -0.5),
        grid_spec=grid_spec,
        out_shape=jax.ShapeDtypeStruct((n, nq * LANE), BF16),
        compiler_params=_cparams("parallel"),
        name="memory_attention",
    )(blk_seq, proj, kv)


def _s5_param_kernel(are_ref, aim_ref, ldt_ref, bre_ref, bim_ref, lre_ref, lim_ref, bbre_ref, bbim_ref):
    a_re = are_ref[...]
    a_im = aim_ref[...]
    dt = jnp.exp(ldt_ref[...])
    mag = jnp.exp(a_re * dt)
    ang = a_im * dt
    lam_re = mag * jnp.cos(ang)
    lam_im = mag * jnp.sin(ang)
    den = a_re * a_re + a_im * a_im
    n_re = lam_re - 1.0
    coef_re = (n_re * a_re + lam_im * a_im) / den
    coef_im = (lam_im * a_re - n_re * a_im) / den
    lre_ref[...] = lam_re
    lim_ref[...] = lam_im
    for c in range(bre_ref.shape[0]):
        b_re = bre_ref[c]
        b_im = bim_ref[c]
        bbre_ref[c] = coef_re * b_re - coef_im * b_im
        bbim_ref[c] = coef_re * b_im + coef_im * b_re


def s5_discretise(a_re, a_im, log_dt, b_re, b_im):
    two, g, p = a_re.shape
    c = b_re.shape[-1]
    rows = two * g
    ldt = jnp.broadcast_to(log_dt.reshape(rows, 1), (rows, p))
    b_re_t = b_re.reshape(rows, p, c).transpose(2, 0, 1)
    b_im_t = b_im.reshape(rows, p, c).transpose(2, 0, 1)
    return pl.pallas_call(
        _s5_param_kernel,
        out_shape=(jax.ShapeDtypeStruct((rows, p), F32), jax.ShapeDtypeStruct((rows, p), F32),
                   jax.ShapeDtypeStruct((c, rows, p), F32), jax.ShapeDtypeStruct((c, rows, p), F32)),
        name="s5_discretise",
    )(a_re.reshape(rows, p), a_im.reshape(rows, p), ldt, b_re_t, b_im_t)


def _s5_operands(lam_re, lam_im, bb_re, bb_im, c_re, c_im, d):
    c, rows, p = bb_re.shape
    g = rows // 2
    gs = LANE // c
    ns = g // gs
    eye = jnp.eye(gs, dtype=F32)

    def bmat(bb):
        x = bb.reshape(c, 2, ns, gs, p).transpose(1, 2, 3, 0, 4)
        return jnp.einsum('dsgcp,gh->dsgchp', x, eye).reshape(2, ns, gs * c, gs * p)

    def cmat(cc):
        x = cc.astype(F32).reshape(2, ns, gs, c, p)
        return jnp.einsum('dsgcp,gh->dsgphc', x, eye).reshape(2, ns, gs * p, gs * c)

    b_op = jnp.concatenate([bmat(bb_re), bmat(bb_im)], axis=-1).astype(BF16)
    c_op = jnp.concatenate([cmat(c_re), -cmat(c_im)], axis=-2).astype(BF16)
    lam = jnp.concatenate([lam_re.reshape(2, ns, gs * p), lam_im.reshape(2, ns, gs * p)], axis=-1)
    lam = jnp.broadcast_to(lam[:, :, None, :], (2, ns, SUBLANE, 2 * gs * p))
    d_op = d.astype(F32).reshape(ns, 1, gs * c)
    return b_op, c_op, lam, d_op


def _s5_kernel(keep_ref, u_hbm, bm_ref, cm_ref, lam_ref, d_ref, o_ref,
               u_scr, u_sem, up_scr, y_scr, y2_scr, bu0_scr, bu1_scr, xs0_scr, xs1_scr, e_scr, i_scr, *, lseg, tblk):
    sup = pl.program_id(0)
    nblk = lseg // tblk
    rb = SUBLANE * tblk
    ns = bu0_scr.shape[1] // 2
    nchunk = ns // LANE
    d_row = d_ref[0]
    pitch = u_scr.shape[1] // S5_SEGMENTS
    n_slab = pl.num_programs(1)
    lin = sup * n_slab + pl.program_id(1)
    slot = lin % 2

    def segment_copies(step, slot):
        b = step // n_slab
        s = step % n_slab
        return [pltpu.make_async_copy(u_hbm.at[s, pl.ds(b * (S5_SEGMENTS * lseg) + r * lseg, lseg), :],
                                      u_scr.at[slot, pl.ds(r * pitch, lseg), :], u_sem.at[slot])
                for r in range(S5_SEGMENTS)]

    @pl.when(lin == 0)
    def _():
        for cp in segment_copies(lin, slot):
            cp.start()

    for cp in segment_copies(lin, slot):
        cp.wait()

    @pl.when(lin + 1 < pl.num_programs(0) * n_slab)
    def _():
        for cp in segment_copies(lin + 1, 1 - slot):
            cp.start()

    u2 = u_scr.at[slot]

    def permute_in(t2, carry):
        t = t2 * 2
        a = u2[pl.ds(t, SUBLANE, stride=pitch), :]
        b = u2[pl.ds(t + 1, SUBLANE, stride=pitch), :]
        ab = jnp.concatenate([a, b], axis=0)
        rows = pl.ds(pl.multiple_of(t * SUBLANE, 2 * SUBLANE), 2 * SUBLANE)
        up_scr[rows, :] = ab.astype(BF16)
        y_scr[rows, :] = ab * d_row
        return carry
    lax.fori_loop(0, lseg // 2, permute_in, 0, unroll=4)
    xs1_scr[...] = jnp.zeros(xs1_scr.shape, xs1_scr.dtype)

    for direction in range(2):
        rev = direction == 1
        lam_r = [lam_ref[direction, 0, :, c * LANE:(c + 1) * LANE] for c in range(nchunk)]
        lam_i = [lam_ref[direction, 0, :, ns + c * LANE:ns + (c + 1) * LANE] for c in range(nchunk)]
        steps = list(range(tblk))[::-1] if rev else list(range(tblk))

        def block_rows(k, rev=rev):
            k = jnp.clip(k, 0, nblk - 1)
            blk = (nblk - 1 - k) if rev else k
            return pl.ds(pl.multiple_of(blk * rb, rb), rb)

        def bu_dot(k, direction=direction, block_rows=block_rows):
            return jnp.dot(up_scr[block_rows(k), :], bm_ref[direction, 0], preferred_element_type=F32)

        def scan_block(bu, x, xs, lam_r=lam_r, lam_i=lam_i, steps=steps):
            xr, xi = list(x[:nchunk]), list(x[nchunk:])
            held = None
            for n, t in enumerate(steps):
                rows = slice(t * SUBLANE, (t + 1) * SUBLANE)
                for c in range(nchunk):
                    b_r = bu[rows, c * LANE:(c + 1) * LANE]
                    b_i = bu[rows, ns + c * LANE:ns + (c + 1) * LANE]
                    r = lam_r[c] * xr[c] - lam_i[c] * xi[c] + b_r
                    i = lam_r[c] * xi[c] + lam_i[c] * xr[c] + b_i
                    xr[c], xi[c] = r, i
                if xs is None:
                    continue
                if n % 2 == 0:
                    held = (t, list(xr), list(xi))
                    continue
                lo, hi = (held, (t, xr, xi)) if held[0] < t else ((t, xr, xi), held)
                pair = slice(lo[0] * SUBLANE, (lo[0] + 2) * SUBLANE)
                for c in range(nchunk):
                    xs[pair, c * LANE:(c + 1) * LANE] = jnp.concatenate([lo[1][c], hi[1][c]], axis=0).astype(BF16)
                    xs[pair, ns + c * LANE:ns + (c + 1) * LANE] = (
                        jnp.concatenate([lo[2][c], hi[2][c]], axis=0).astype(BF16))
            return tuple(xr + xi)

        def pass_a(k2, x, bu_dot=bu_dot, scan_block=scan_block):
            bu1_scr[...] = bu_dot(2 * k2 + 1)
            x = scan_block(bu0_scr, x, None)
            bu0_scr[...] = bu_dot(2 * k2 + 2)
            return scan_block(bu1_scr, x, None)
        bu0_scr[...] = bu_dot(0)
        zero = tuple(jnp.zeros((SUBLANE, LANE), F32) for _ in range(2 * nchunk))
        ends = lax.fori_loop(0, nblk // 2, pass_a, zero)
        for c in range(2 * nchunk):
            e_scr[:, c * LANE:(c + 1) * LANE] = ends[c]

        p_r = lam_ref[direction, 0, 0:1, 0:ns]
        p_i = lam_ref[direction, 0, 0:1, ns:2 * ns]
        for _ in range(int(math.log2(lseg))):
            p_r, p_i = p_r * p_r - p_i * p_i, 2.0 * p_r * p_i
        order = list(range(S5_SEGMENTS))[::-1] if rev else list(range(S5_SEGMENTS))
        s_r = jnp.zeros((1, ns), F32)
        s_i = jnp.zeros((1, ns), F32)
        i_scr[order[0]:order[0] + 1, :] = jnp.zeros((1, 2 * ns), F32)
        for prev, cur in zip(order[:-1], order[1:]):
            e_r = e_scr[prev:prev + 1, 0:ns]
            e_i = e_scr[prev:prev + 1, ns:2 * ns]
            keep = keep_ref[(sup * 2 + direction) * S5_SEGMENTS + cur].astype(F32)
            n_r = (p_r * s_r - p_i * s_i + e_r) * keep
            n_i = (p_r * s_i + p_i * s_r + e_i) * keep
            i_scr[cur:cur + 1, 0:ns] = n_r
            i_scr[cur:cur + 1, ns:2 * ns] = n_i
            s_r, s_i = n_r, n_i

        def project(k, xs, scale, direction=direction, block_rows=block_rows):
            rows = block_rows(k)
            yb = jnp.dot(xs[...], cm_ref[direction, 0], preferred_element_type=F32)
            y_scr[rows, :] = y_scr[rows, :] + yb * scale

        def pass_b(k2, x, bu_dot=bu_dot, scan_block=scan_block, project=project):
            bu1_scr[...] = bu_dot(2 * k2 + 1)
            x = scan_block(bu0_scr, x, xs0_scr)
            project(2 * k2 - 1, xs1_scr, jnp.where(k2 > 0, 1.0, 0.0))
            bu0_scr[...] = bu_dot(2 * k2 + 2)
            x = scan_block(bu1_scr, x, xs1_scr)
            project(2 * k2, xs0_scr, 1.0)
            return x
        bu0_scr[...] = bu_dot(0)
        start = tuple(i_scr[:, c * LANE:(c + 1) * LANE] for c in range(2 * nchunk))
        lax.fori_loop(0, nblk // 2, pass_b, start)
        project(nblk - 1, xs1_scr, 1.0)

    def permute_out(t, carry):
        y = y_scr[pl.ds(pl.multiple_of(t * SUBLANE, SUBLANE), SUBLANE), :]
        gelu = 0.5 * y * (1.0 + jnp.tanh(math.sqrt(2.0 / math.pi) * (y + 0.044715 * (y * y * y))))
        y2_scr[pl.ds(t, SUBLANE, stride=pitch), :] = gelu
        return carry
    lax.fori_loop(0, lseg, permute_out, 0, unroll=8)
    for r in range(S5_SEGMENTS):
        o_ref[r * lseg:(r + 1) * lseg, :] = y2_scr[r * pitch:r * pitch + lseg, :].astype(o_ref.dtype)


def _s5_keep_table(seq_lens, lseg):
    starts, ends = set(), set()
    tok = 0
    for length in seq_lens:
        assert length % lseg == 0
        starts.add(tok // lseg)
        tok += length
        ends.add(tok // lseg - 1)
    nseg = tok // lseg
    assert nseg % S5_SEGMENTS == 0
    keep = np.ones((nseg // S5_SEGMENTS, 2, S5_SEGMENTS), np.int32)
    for s in range(nseg):
        if s in starts:
            keep[s // S5_SEGMENTS, 0, s % S5_SEGMENTS] = 0
        if s in ends:
            keep[s // S5_SEGMENTS, 1, s % S5_SEGMENTS] = 0
    return keep.reshape(-1)


def s5_mixer(proj, b_op, c_op, lam, d_op, seq_lens):
    n = proj.shape[1]
    ns = b_op.shape[1]
    nstate2 = b_op.shape[-1]
    nsup = n // SUPER
    lseg = SUPER // S5_SEGMENTS
    tblk = min(S5_TBLK, lseg)
    assert lseg & (lseg - 1) == 0 and lseg % (2 * tblk) == 0 and tblk % 2 == 0
    keep = _s5_keep_table(seq_lens, lseg)
    rows = SUBLANE * tblk
    pitch = lseg + SUBLANE
    grid_spec = pltpu.PrefetchScalarGridSpec(
        num_scalar_prefetch=1,
        grid=(nsup, ns),
        in_specs=[
            pl.BlockSpec(memory_space=pl.ANY),
            pl.BlockSpec((2, 1, LANE, nstate2), lambda b, s, kp: (0, s, 0, 0)),
            pl.BlockSpec((2, 1, nstate2, LANE), lambda b, s, kp: (0, s, 0, 0)),
            pl.BlockSpec((2, 1, SUBLANE, nstate2), lambda b, s, kp: (0, s, 0, 0)),
            pl.BlockSpec((1, 1, LANE), lambda b, s, kp: (s, 0, 0)),
        ],
        out_specs=pl.BlockSpec((SUPER, LANE), lambda b, s, kp: (b, s)),
        scratch_shapes=[
            pltpu.VMEM((2, S5_SEGMENTS * pitch, LANE), F32),
            pltpu.SemaphoreType.DMA((2,)),
            pltpu.VMEM((SUPER, LANE), BF16),
            pltpu.VMEM((SUPER, LANE), F32),
            pltpu.VMEM((S5_SEGMENTS * pitch, LANE), F32),
            pltpu.VMEM((rows, nstate2), F32),
            pltpu.VMEM((rows, nstate2), F32),
            pltpu.VMEM((rows, nstate2), BF16),
            pltpu.VMEM((rows, nstate2), BF16),
            pltpu.VMEM((SUBLANE, nstate2), F32),
            pltpu.VMEM((SUBLANE, nstate2), F32),
        ],
    )
    return pl.pallas_call(
        functools.partial(_s5_kernel, lseg=lseg, tblk=tblk),
        grid_spec=grid_spec,
        out_shape=jax.ShapeDtypeStruct((n, ns * LANE), BF16),
        compiler_params=_cparams("arbitrary", "arbitrary"),
        name="s5_mixer",
    )(jnp.asarray(keep), proj, b_op, c_op, lam, d_op)


def _glu_kernel(g_ref, w_ref, b_ref, o_ref):
    tn = o_ref.shape[1]
    col = pl.multiple_of(pl.program_id(1) * tn, tn)
    acc = jnp.dot(g_ref[...], w_ref[...], preferred_element_type=F32) + b_ref[...]
    g = g_ref[:, pl.ds(col, tn)].astype(F32)
    o_ref[...] = (g * _sigmoid(acc)).astype(o_ref.dtype)


def glu(g, w, b, tm=1024, tn=1024):
    m, k = g.shape
    assert m % tm == 0 and k % tn == 0
    return pl.pallas_call(
        _glu_kernel,
        grid=(m // tm, k // tn),
        in_specs=[pl.BlockSpec((tm, k), lambda i, j: (i, 0)),
                  pl.BlockSpec((k, tn), lambda i, j: (0, j)),
                  pl.BlockSpec((1, tn), lambda i, j: (0, j))],
        out_specs=pl.BlockSpec((tm, tn), lambda i, j: (i, j)),
        out_shape=jax.ShapeDtypeStruct((m, k), BF16),
        compiler_params=_cparams("parallel", "arbitrary"),
        name="glu",
    )(g, w, b.reshape(1, k).astype(F32))


def _out_proj_kernel(x_ref, tok_ref, mo_ref, w1_ref, w2_ref, o_ref):
    acc = jnp.dot(tok_ref[...], w1_ref[...], preferred_element_type=F32)
    acc = acc + jnp.dot(mo_ref[...], w2_ref[...], preferred_element_type=F32)
    o_ref[...] = x_ref[...] + acc


def out_proj(x, tok, mo, w, tm=1024, tn=512):
    m, d = x.shape
    k1, k2 = tok.shape[1], mo.shape[1]
    assert m % tm == 0 and d % tn == 0 and w.shape[0] == k1 + k2 and k1 % k2 == 0
    return pl.pallas_call(
        _out_proj_kernel,
        grid=(m // tm, d // tn),
        in_specs=[pl.BlockSpec((tm, tn), lambda i, j: (i, j)),
                  pl.BlockSpec((tm, k1), lambda i, j: (i, 0)),
                  pl.BlockSpec((tm, k2), lambda i, j: (i, 0)),
                  pl.BlockSpec((k1, tn), lambda i, j: (0, j)),
                  pl.BlockSpec((k2, tn), lambda i, j: (k1 // k2, j))],
        out_specs=pl.BlockSpec((tm, tn), lambda i, j: (i, j)),
        out_shape=jax.ShapeDtypeStruct((m, d), F32),
        compiler_params=_cparams("parallel", "arbitrary"),
        name="out_proj",
    )(x, tok, mo, w, w)


def _router_kernel(x_ref, g_ref, whi_ref, wlo_ref, b_ref, h_ref, meta_ref):
    t = _rms_norm(x_ref[...], g_ref[...])
    h_ref[...] = _pack_bf16_pairs(t)
    t_hi, t_lo = _split_hi_lo(t)
    logits = (jnp.dot(t_hi, whi_ref[...], preferred_element_type=F32)
              + jnp.dot(t_lo, whi_ref[...], preferred_element_type=F32)
              + jnp.dot(t_hi, wlo_ref[...], preferred_element_type=F32)) + b_ref[...]
    lane = lax.broadcasted_iota(jnp.int32, logits.shape, 1)
    big = jnp.int32(LANE)
    is_g = lane < N_EXPERT_GROUPS
    gl = jnp.where(is_g, logits, -jnp.inf)
    gmax = jnp.max(gl, axis=-1, keepdims=True)
    gsum = jnp.sum(jnp.where(is_g, jnp.exp(gl - gmax), 0.0), axis=-1, keepdims=True)
    g_val = 1.0 / gsum
    g_idx = jnp.min(jnp.where(gl == gmax, lane, big), axis=-1, keepdims=True)
    e_lane = lane - N_EXPERT_GROUPS
    in_grp = (e_lane >= 0) & (e_lane < N_EXPERTS) & ((e_lane >> 2) == g_idx)
    el = jnp.where(in_grp, logits, -jnp.inf)
    v1 = jnp.max(el, axis=-1, keepdims=True)
    i1 = jnp.min(jnp.where(el == v1, lane, big), axis=-1, keepdims=True)
    el2 = jnp.where(lane == i1, -jnp.inf, el)
    v2 = jnp.max(el2, axis=-1, keepdims=True)
    i2 = jnp.min(jnp.where(el2 == v2, lane, big), axis=-1, keepdims=True)
    z = jnp.exp(v2 - v1)
    w1 = g_val / (1.0 + z)
    w2 = g_val * z / (1.0 + z)
    e1 = (i1 - N_EXPERT_GROUPS).astype(F32)
    e2 = (i2 - N_EXPERT_GROUPS).astype(F32)
    meta = jnp.where(lane == 0, w1, jnp.where(lane == 1, w2, jnp.where(lane == 2, e1, jnp.where(lane == 3, e2, 0.0))))
    meta_ref[...] = meta


def router(x, gain, w_rg, b_rg, w_re, b_re, tm=512):
    m, d = x.shape
    assert EXPERTS_PER_GROUP == 4 and m % tm == 0
    pad = LANE - N_EXPERT_GROUPS - N_EXPERTS
    w = jnp.concatenate([w_rg, w_re, jnp.zeros((d, pad), F32)], axis=1).astype(F32)
    w_hi, w_lo = _split_hi_lo(w)
    b = jnp.concatenate([b_rg, b_re, jnp.zeros((pad,), F32)]).astype(F32).reshape(1, LANE)
    return pl.pallas_call(
        _router_kernel,
        grid=(m // tm,),
        in_specs=[pl.BlockSpec((tm, d), lambda i: (i, 0)),
                  pl.BlockSpec((1, d), lambda i: (0, 0)),
                  pl.BlockSpec((d, LANE), lambda i: (0, 0)),
                  pl.BlockSpec((d, LANE), lambda i: (0, 0)),
                  pl.BlockSpec((1, LANE), lambda i: (0, 0))],
        out_specs=(pl.BlockSpec((tm, d // 2), lambda i: (i, 0)),
                   pl.BlockSpec((tm, LANE), lambda i: (i, 0))),
        out_shape=(jax.ShapeDtypeStruct((m, d // 2), U32), jax.ShapeDtypeStruct((m, LANE), F32)),
        compiler_params=_cparams("parallel"),
        name="moe_router",
    )(x, gain.reshape(1, d), w_hi, w_lo, b)


def _routing_plan(meta, n_rows):
    n = meta.shape[0]
    e = meta[:, 2:4].astype(jnp.int32)
    onehot = (e[:, :, None] == jnp.arange(N_EXPERTS)[None, None, :]).any(axis=1).astype(jnp.int32)
    counts = onehot.sum(axis=0)
    rank = jnp.cumsum(onehot, axis=0) - onehot
    padded = ((counts + MOE_TILE - 1) // MOE_TILE) * MOE_TILE
    ends = jnp.cumsum(padded)
    offs = ends - padded
    pos = jnp.take_along_axis(offs[None, :] + rank, e, axis=1).astype(jnp.int32)
    tok = jnp.arange(n, dtype=jnp.int32)
    sorted_tok = jnp.zeros((n_rows,), jnp.int32).at[pos.T.reshape(-1)].set(jnp.concatenate([tok, tok]))
    tile_start = jnp.arange(n_rows // MOE_TILE, dtype=jnp.int32) * MOE_TILE
    tile_expert = jnp.minimum((ends[None, :] <= tile_start[:, None]).sum(axis=1), N_EXPERTS - 1).astype(jnp.int32)
    tile_valid = (tile_start < ends[-1]).astype(jnp.int32)
    return pos[:, 0], pos[:, 1], sorted_tok, tile_expert, tile_valid


def _row_gather_start(idx_ref, base, src_ref, dst_ref, sem):
    for r in range(dst_ref.shape[0]):
        pltpu.make_async_copy(src_ref.at[pl.ds(idx_ref[base + r], 1), :], dst_ref.at[pl.ds(r, 1), :], sem).start()


def _row_gather_wait(src_ref, dst_ref, sem):
    pltpu.make_async_copy(src_ref.at[pl.ds(0, dst_ref.shape[0]), :], dst_ref, sem).wait()


def _moe_up_kernel(st_ref, te_ref, tv_ref, h_ref, wg_ref, wu_ref, o_ref, xbuf, sem):
    del te_ref
    t = pl.program_id(0)
    last = pl.num_programs(0) - 1
    slot = t % 2
    tile = xbuf.shape[1]

    @pl.when(t == 0)
    def _():
        _row_gather_start(st_ref, 0, h_ref, xbuf.at[0], sem.at[0])

    @pl.when(tv_ref[t] == 1)
    def _():
        _row_gather_wait(h_ref, xbuf.at[slot], sem.at[slot])
        nxt = jnp.minimum(t + 1, last)
        _row_gather_start(st_ref, nxt * tile, h_ref, xbuf.at[1 - slot], sem.at[1 - slot])
        lo, hi = _unpack_bf16_pairs(xbuf[slot])
        lo = lo.astype(BF16)
        hi = hi.astype(BF16)
        half = lo.shape[1]
        g = jnp.dot(lo, wg_ref[0, :half], preferred_element_type=F32)
        g = g + jnp.dot(hi, wg_ref[0, half:], preferred_element_type=F32)
        u = jnp.dot(lo, wu_ref[0, :half], preferred_element_type=F32)
        u = u + jnp.dot(hi, wu_ref[0, half:], preferred_element_type=F32)
        o_ref[...] = (g * _sigmoid(g) * u).astype(o_ref.dtype)

        @pl.when(t == last)
        def _():
            _row_gather_wait(h_ref, xbuf.at[1 - slot], sem.at[1 - slot])

    @pl.when(tv_ref[t] == 0)
    def _():
        @pl.when(tv_ref[jnp.maximum(t - 1, 0)] == 1)
        def _():
            _row_gather_wait(h_ref, xbuf.at[slot], sem.at[slot])
        o_ref[...] = jnp.zeros(o_ref.shape, o_ref.dtype)


def moe_up(h, w_gate, w_up, sorted_tok, tile_expert, tile_valid):
    half = h.shape[1]
    d = 2 * half
    f = w_gate.shape[-1]
    n_tiles = tile_expert.shape[0]
    grid_spec = pltpu.PrefetchScalarGridSpec(
        num_scalar_prefetch=3,
        grid=(n_tiles,),
        in_specs=[pl.BlockSpec(memory_space=pl.ANY),
                  pl.BlockSpec((1, d, f), lambda t, st, te, tv: (te[t], 0, 0)),
                  pl.BlockSpec((1, d, f), lambda t, st, te, tv: (te[t], 0, 0))],
        out_specs=pl.BlockSpec((MOE_TILE, f), lambda t, st, te, tv: (t, 0)),
        scratch_shapes=[pltpu.VMEM((2, MOE_TILE, half), U32), pltpu.SemaphoreType.DMA((2,))],
    )
    return pl.pallas_call(
        _moe_up_kernel,
        grid_spec=grid_spec,
        out_shape=jax.ShapeDtypeStruct((n_tiles * MOE_TILE, f), BF16),
        compiler_params=_cparams("arbitrary"),
        name="moe_up",
    )(sorted_tok, tile_expert, tile_valid, h, w_gate, w_up)


def _moe_down_kernel(te_ref, tv_ref, h_ref, wd_ref, o_ref):
    del te_ref
    t = pl.program_id(0)

    @pl.when(tv_ref[t] == 1)
    def _():
        y = jnp.dot(h_ref[...], wd_ref[0], preferred_element_type=F32)
        o_ref[...] = _pack_bf16_pairs(y)

    @pl.when(tv_ref[t] == 0)
    def _():
        o_ref[...] = jnp.zeros(o_ref.shape, o_ref.dtype)


def moe_down(hid, w_down, tile_expert, tile_valid):
    n_rows, f = hid.shape
    d = w_down.shape[-1]
    grid_spec = pltpu.PrefetchScalarGridSpec(
        num_scalar_prefetch=2,
        grid=(n_rows // MOE_TILE,),
        in_specs=[pl.BlockSpec((MOE_TILE, f), lambda t, te, tv: (t, 0)),
                  pl.BlockSpec((1, f, d), lambda t, te, tv: (te[t], 0, 0))],
        out_specs=pl.BlockSpec((MOE_TILE, d // 2), lambda t, te, tv: (t, 0)),
    )
    return pl.pallas_call(
        _moe_down_kernel,
        grid_spec=grid_spec,
        out_shape=jax.ShapeDtypeStruct((n_rows, d // 2), U32),
        compiler_params=_cparams("arbitrary"),
        name="moe_down",
    )(tile_expert, tile_valid, hid, w_down)


def _combine_kernel(p0_ref, p1_ref, x_ref, meta_ref, g_ref, ys_ref, o_ref, buf0, buf1, sem, *, base, final_norm):
    i = pl.program_id(0)
    last = pl.num_programs(0) - 1
    tc = buf0.shape[1]

    def start(tile, buf, sems):
        n0 = base + tile * tc
        _row_gather_start(p0_ref, n0, ys_ref, buf.at[0], sems.at[0])
        _row_gather_start(p1_ref, n0, ys_ref, buf.at[1], sems.at[1])

    def wait(buf, sems):
        _row_gather_wait(ys_ref, buf.at[0], sems.at[0])
        _row_gather_wait(ys_ref, buf.at[1], sems.at[1])

    @pl.when(i == 0)
    def _():
        start(0, buf0, sem.at[0])

    def run(cur, cur_sems, nxt, nxt_sems):
        wait(cur, cur_sems)
        lo0, hi0 = _unpack_bf16_pairs(cur[0])
        lo1, hi1 = _unpack_bf16_pairs(cur[1])
        start(jnp.minimum(i + 1, last), nxt, nxt_sems)
        meta = meta_ref[...]
        w0 = meta[:, 0:1]
        w1 = meta[:, 1:2]
        half = lo0.shape[1]
        y_lo = x_ref[:, :half] + w0 * lo0 + w1 * lo1
        y_hi = x_ref[:, half:] + w0 * hi0 + w1 * hi1
        if final_norm:
            ms = (jnp.sum(y_lo * y_lo, axis=-1, keepdims=True)
                  + jnp.sum(y_hi * y_hi, axis=-1, keepdims=True)) / (2 * half)
            inv = lax.rsqrt(ms + RMS_EPS)
            y_lo = y_lo * inv * g_ref[:, :half]
            y_hi = y_hi * inv * g_ref[:, half:]
        o_ref[:, :half] = y_lo
        o_ref[:, half:] = y_hi

        @pl.when(i == last)
        def _():
            wait(nxt, nxt_sems)

    @pl.when(i % 2 == 0)
    def _():
        run(buf0, sem.at[0], buf1, sem.at[1])

    @pl.when(i % 2 == 1)
    def _():
        run(buf1, sem.at[1], buf0, sem.at[0])


def moe_combine(x, meta, ys, pos0, pos1, gain, *, base, count, final_norm, tc=256):
    d = x.shape[1]
    assert base % tc == 0 and count % tc == 0
    blk0 = base // tc
    grid_spec = pltpu.PrefetchScalarGridSpec(
        num_scalar_prefetch=2,
        grid=(count // tc,),
        in_specs=[pl.BlockSpec((tc, d), lambda i, p0, p1: (blk0 + i, 0)),
                  pl.BlockSpec((tc, LANE), lambda i, p0, p1: (blk0 + i, 0)),
                  pl.BlockSpec((1, d), lambda i, p0, p1: (0, 0)),
                  pl.BlockSpec(memory_space=pl.ANY)],
        out_specs=pl.BlockSpec((tc, d), lambda i, p0, p1: (i, 0)),
        scratch_shapes=[pltpu.VMEM((2, tc, d // 2), U32), pltpu.VMEM((2, tc, d // 2), U32),
                        pltpu.SemaphoreType.DMA((2, 2))],
    )
    return pl.pallas_call(
        functools.partial(_combine_kernel, base=base, final_norm=final_norm),
        grid_spec=grid_spec,
        out_shape=jax.ShapeDtypeStruct((count, d), F32),
        compiler_params=_cparams("arbitrary"),
        name="moe_combine",
    )(pos0, pos1, x, meta, gain.reshape(1, d), ys)


def moe_experts(x, gain, w_rg, b_rg, w_re, b_re, w_gate, w_up, w_down, layer):
    n = x.shape[0]
    n_rows = 2 * n + N_EXPERTS * MOE_TILE
    assert (2 * n) % MOE_TILE == 0
    h, meta = router(x, gain, w_rg, b_rg, w_re, b_re)
    pos0, pos1, sorted_tok, tile_expert, tile_valid = _routing_plan(meta, n_rows)
    tile_expert = tile_expert + layer * N_EXPERTS
    hid = moe_up(h, w_gate, w_up, sorted_tok, tile_expert, tile_valid)
    ys = moe_down(hid, w_down, tile_expert, tile_valid)
    return meta, ys, pos0, pos1


def kernel(x_prompt, x_sample, mem_prompt, mem_sample, norm_mix, norm_mem, norm_ffn, norm_final, na_w_in, na_rpb, s5_w_in, s5_a_re, s5_a_im, s5_log_dt, s5_b_re, s5_b_im, s5_c_re, s5_c_im, s5_d, s5_w_glu, s5_b_glu, mem_w_kv, w_out, moe_w_rg, moe_b_rg, moe_w_re, moe_b_re, moe_w_gate, moe_w_up, moe_w_down):
    d = x_prompt.shape[-1]
    groups = [x_prompt, x_sample]
    mems = [mem_prompt, mem_sample]
    seq_lens = [g.shape[1] for g in groups for _ in range(g.shape[0])]
    n_mem = mem_prompt.shape[1]
    depth = norm_mix.shape[0]
    x = jnp.concatenate([g.reshape(-1, d) for g in groups], axis=0)
    mem = jnp.concatenate([m.reshape(-1, d) for m in mems], axis=0)
    n = x.shape[0]
    tq_mem = 512
    blk_seq = []
    for s, length in enumerate(seq_lens):
        assert length % tq_mem == 0
        blk_seq += [s] * (length // tq_mem)
    blk_seq = jnp.asarray(np.asarray(blk_seq, np.int32))
    na_heads = na_rpb.shape[1]
    w_gate_all = moe_w_gate.astype(BF16).reshape((-1,) + moe_w_gate.shape[2:])
    w_up_all = moe_w_up.astype(BF16).reshape((-1,) + moe_w_up.shape[2:])
    w_down_all = moe_w_down.astype(BF16).reshape((-1,) + moe_w_down.shape[2:])

    meta = ys = pos0 = pos1 = None
    for i in range(depth):
        j = i // 2
        if i > 0:
            x = moe_combine(x, meta, ys, pos0, pos1, norm_final, base=0, count=n, final_norm=False)
        kv = norm_proj(mem, norm_mem[i], mem_w_kv[i].astype(BF16), BF16, tm=n_mem)
        if i % 2 == 0:
            proj = norm_proj(x, norm_mix[i], na_w_in[j].astype(BF16), BF16, tm=512)
            tok = neighbourhood_attention(proj, na_rpb[j], seq_lens)
            q_slab0 = 3 * na_heads
        else:
            proj = norm_proj(x, norm_mix[i], s5_w_in[j].astype(BF16), F32, tm=512)
            lam_re, lam_im, bb_re, bb_im = s5_discretise(s5_a_re[j], s5_a_im[j], s5_log_dt[j], s5_b_re[j], s5_b_im[j])
            b_op, c_op, lam, d_op = _s5_operands(lam_re, lam_im, bb_re, bb_im, s5_c_re[j], s5_c_im[j], s5_d[j])
            g = s5_mixer(proj, b_op, c_op, lam, d_op, seq_lens)
            tok = glu(g, s5_w_glu[j].astype(BF16), s5_b_glu[j])
            q_slab0 = b_op.shape[1]
        mo = memory_attention(proj, q_slab0, kv, blk_seq, n_mem, tq=tq_mem)
        x = out_proj(x, tok, mo, w_out[i].astype(BF16))
        meta, ys, pos0, pos1 = moe_experts(x, norm_ffn[i], moe_w_rg[i], moe_b_rg[i], moe_w_re[i], moe_b_re[i],
                                           w_gate_all, w_up_all, w_down_all, i)
    outs, base = [], 0
    for g in groups:
        count = g.shape[0] * g.shape[1]
        y = moe_combine(x, meta, ys, pos0, pos1, norm_final, base=base, count=count, final_norm=True)
        outs.append(y.reshape(g.shape))
        base += count
    return tuple(outs)
```

```python
import functools
import math

import numpy as np
import jax
import jax.numpy as jnp
from jax import lax
from jax.experimental import pallas as pl
from jax.experimental.pallas import tpu as pltpu

F32 = jnp.float32
BF16 = jnp.bfloat16
U32 = jnp.uint32

LANE = 128
SUBLANE = 8
VMEM_LIMIT_BYTES = 56 * 1024 * 1024

RMS_EPS = 1e-6
NEG_INF = -1e30
GRID_W = 64
ROW_WIN = 8
COL_WIN = 16
NA_HEAD_DIM = 128
MEM_HEADS = 4
S5_GROUP = 16
S5_STATE = 64
N_EXPERT_GROUPS = 4
EXPERTS_PER_GROUP = 4
N_EXPERTS = N_EXPERT_GROUPS * EXPERTS_PER_GROUP

NA_Q_ROWS = 8
NA_K_ROWS = 2 * ROW_WIN
NA_HEADS_PER_STEP = 3
SUPER = 8192
S5_SEGMENTS = SUBLANE
S5_TBLK = 64
MOE_TILE = 256


def _cparams(*sem):
    return pltpu.CompilerParams(dimension_semantics=sem, vmem_limit_bytes=VMEM_LIMIT_BYTES)


def _rms_norm(x, gain):
    ms = jnp.mean(x * x, axis=-1, keepdims=True)
    return x * lax.rsqrt(ms + RMS_EPS) * gain


def _sigmoid(x):
    return 1.0 / (1.0 + jnp.exp(-x))


def _split_hi_lo(x):
    hi = lax.bitcast_convert_type(lax.bitcast_convert_type(x, U32) & jnp.uint32(0xFFFF0000), F32)
    return hi.astype(BF16), (x - hi).astype(BF16)


def _pack_bf16_pairs(x):
    k = x.shape[1] // 2
    bits = lax.bitcast_convert_type(x, U32)
    return (bits[:, k:] & jnp.uint32(0xFFFF0000)) | (bits[:, :k] >> 16)


def _unpack_bf16_pairs(w):
    lo = lax.bitcast_convert_type(w << 16, F32)
    hi = lax.bitcast_convert_type(w & jnp.uint32(0xFFFF0000), F32)
    return lo, hi


def _norm_proj_kernel(x_ref, g_ref, w_ref, o_ref, h_scr):
    @pl.when(pl.program_id(1) == 0)
    def _():
        h_scr[...] = _rms_norm(x_ref[...], g_ref[...]).astype(BF16)

    acc = jnp.dot(h_scr[...], w_ref[...], preferred_element_type=F32)
    for c in range(o_ref.shape[0]):
        o_ref[c] = acc[:, c * LANE:(c + 1) * LANE].astype(o_ref.dtype)


def norm_proj(x, gain, w, out_dtype, tm, tn=1024):
    m, d = x.shape
    n_out = w.shape[1]
    assert m % tm == 0 and n_out % tn == 0 and tn % LANE == 0
    return pl.pallas_call(
        _norm_proj_kernel,
        grid=(m // tm, n_out // tn),
        in_specs=[pl.BlockSpec((tm, d), lambda i, j: (i, 0)),
                  pl.BlockSpec((1, d), lambda i, j: (0, 0)),
                  pl.BlockSpec((d, tn), lambda i, j: (0, j))],
        out_specs=pl.BlockSpec((tn // LANE, tm, LANE), lambda i, j: (j, i, 0)),
        out_shape=jax.ShapeDtypeStruct((n_out // LANE, m, LANE), out_dtype),
        scratch_shapes=[pltpu.VMEM((tm, d), BF16)],
        compiler_params=_cparams("parallel", "arbitrary"),
        name="norm_proj",
    )(x, gain.reshape(1, d), w)


def _na_kernel(ks_ref, ty_ref, pi_ref, q_ref, k_ref, v_ref, b_ref, o_ref, *, nqb, scale):
    step = pl.program_id(1) * nqb + pl.program_id(2)
    start = pl.multiple_of(ks_ref[step] * GRID_W, GRID_W)
    nk = NA_K_ROWS * GRID_W
    npair = NA_K_ROWS // 2
    ty = ty_ref[step]
    for h in range(q_ref.shape[0]):
        q = q_ref[h]
        k = k_ref[h, pl.ds(start, nk), :]
        v = v_ref[h, pl.ds(start, nk), :]
        s = lax.dot_general(q, k, (((1,), (1,)), ((), ())), preferred_element_type=F32)
        bias = jnp.concatenate(
            [jnp.concatenate([b_ref[h, pi_ref[(ty * NA_Q_ROWS + i) * npair + jp]] for jp in range(npair)], axis=1)
             for i in range(NA_Q_ROWS)], axis=0)
        s = s * scale + bias
        m = jnp.max(s, axis=-1, keepdims=True)
        p = jnp.exp(s - m)
        l = jnp.sum(p, axis=-1, keepdims=True)
        o = jnp.dot(p.astype(BF16), v, preferred_element_type=F32) / l
        o_ref[:, h * LANE:(h + 1) * LANE] = o.astype(o_ref.dtype)


def _na_tables(seq_lens):
    ks, ty = [], []
    tok = 0
    per_super = SUPER // GRID_W
    for length in seq_lens:
        rows = length // GRID_W
        assert length % (NA_Q_ROWS * GRID_W) == 0 and rows >= 3 * NA_Q_ROWS
        assert tok // SUPER == (tok + length - 1) // SUPER
        row0 = (tok % SUPER) // GRID_W
        for qb in range(rows // NA_Q_ROWS):
            ks.append(row0 + int(np.clip(qb * NA_Q_ROWS - ROW_WIN // 2, 0, rows - NA_K_ROWS)))
            ty.append(0 if qb == 0 else (2 if qb == rows // NA_Q_ROWS - 1 else 1))
        tok += length
    assert tok % SUPER == 0 and len(ks) == (tok // SUPER) * (per_super // NA_Q_ROWS)
    return np.asarray(ks, np.int32), np.asarray(ty, np.int32)


def _na_bias(rpb):
    heads = rpb.shape[0]
    cols = np.arange(GRID_W)
    c_start = np.clip(cols - COL_WIN // 2, 0, GRID_W - COL_WIN)
    in_win = (cols[None, :] >= c_start[:, None]) & (cols[None, :] < c_start[:, None] + COL_WIN)
    dc = np.clip(cols[None, :] - cols[:, None], -(COL_WIN - 1), COL_WIN - 1) + COL_WIN - 1
    col_bias = jnp.where(jnp.asarray(in_win), rpb.astype(F32)[:, :, dc], NEG_INF)
    n_dr = 2 * ROW_WIN - 1
    masked = jnp.full((heads, n_dr, GRID_W, GRID_W), NEG_INF, F32)
    left = jnp.concatenate([col_bias[:, :n_dr - 1], masked, col_bias, masked[:, :1]], axis=1)
    right = jnp.concatenate([col_bias[:, 1:], col_bias, masked, masked[:, :1]], axis=1)
    pairs = jnp.concatenate([left, right], axis=-1)
    both, right_only, left_only, none = 0, n_dr - 1, 2 * n_dr - 1, 3 * n_dr - 1
    half = ROW_WIN // 2
    index = []
    for ty in range(3):
        for i in range(NA_Q_ROWS):
            q_rel = (i, i + half, i + ROW_WIN)[ty]
            w_rel = (max(i - half, 0), i, min(i + half, ROW_WIN))[ty]
            for jp in range(NA_K_ROWS // 2):
                j0, j1 = 2 * jp, 2 * jp + 1
                v0, v1 = w_rel <= j0 < w_rel + ROW_WIN, w_rel <= j1 < w_rel + ROW_WIN
                d0, d1 = j0 - q_rel + ROW_WIN - 1, j1 - q_rel + ROW_WIN - 1
                assert (not v0 or 0 <= d0 < n_dr) and (not v1 or 0 <= d1 < n_dr)
                index.append(both + d0 if v0 and v1 else right_only + d1 if v1 else left_only + d0 if v0 else none)
    return pairs, np.asarray(index, np.int32)


def neighbourhood_attention(qkv, rpb, seq_lens):
    heads = rpb.shape[0]
    n = qkv.shape[1]
    nsup = n // SUPER
    tq = NA_Q_ROWS * GRID_W
    nqb = SUPER // tq
    ks_tab, ty_tab = _na_tables(seq_lens)
    pairs, pair_index = _na_bias(rpb)
    hp = NA_HEADS_PER_STEP
    assert heads % hp == 0
    ng = heads // hp
    grid_spec = pltpu.PrefetchScalarGridSpec(
        num_scalar_prefetch=3,
        grid=(ng, nsup, nqb),
        in_specs=[
            pl.BlockSpec((hp, tq, LANE), lambda h, s, b, ks, ty, pi: (h, s * nqb + b, 0)),
            pl.BlockSpec((hp, SUPER, LANE), lambda h, s, b, ks, ty, pi: (ng + h, s, 0)),
            pl.BlockSpec((hp, SUPER, LANE), lambda h, s, b, ks, ty, pi: (2 * ng + h, s, 0)),
            pl.BlockSpec((hp,) + pairs.shape[1:], lambda h, s, b, ks, ty, pi: (h, 0, 0, 0)),
        ],
        out_specs=pl.BlockSpec((tq, hp * LANE), lambda h, s, b, ks, ty, pi: (s * nqb + b, h)),
    )
    return pl.pallas_call(
        functools.partial(_na_kernel, nqb=nqb, scale=NA_HEAD_DIM ** -0.5),
        grid_spec=grid_spec,
        out_shape=jax.ShapeDtypeStruct((n, heads * LANE), BF16),
        compiler_params=_cparams("parallel", "arbitrary", "arbitrary"),
        name="neighbourhood_attention",
    )(jnp.asarray(ks_tab), jnp.asarray(ty_tab), jnp.asarray(pair_index), qkv, qkv, qkv, pairs)


def _mem_attn_kernel(seq_ref, q_ref, kv_ref, o_ref, *, scale):
    del seq_ref
    hd = 2 * LANE
    for h in range(MEM_HEADS):
        q = jnp.concatenate([q_ref[2 * h], q_ref[2 * h + 1]], axis=-1).astype(BF16)
        k = jnp.concatenate([kv_ref[2 * h], kv_ref[2 * h + 1]], axis=-1)
        v = jnp.concatenate([kv_ref[2 * MEM_HEADS + 2 * h], kv_ref[2 * MEM_HEADS + 2 * h + 1]], axis=-1)
        s = lax.dot_general(q, k, (((1,), (1,)), ((), ())), preferred_element_type=F32) * scale
        m = jnp.max(s, axis=-1, keepdims=True)
        p = jnp.exp(s - m)
        l = jnp.sum(p, axis=-1, keepdims=True)
        o = jnp.dot(p.astype(BF16), v, preferred_element_type=F32) / l
        o_ref[:, h * hd:(h + 1) * hd] = o.astype(o_ref.dtype)


def memory_attention(proj, q_slab0, kv, blk_seq, n_mem, tq=512):
    n = proj.shape[1]
    nq = 2 * MEM_HEADS
    assert q_slab0 % nq == 0
    grid_spec = pltpu.PrefetchScalarGridSpec(
        num_scalar_prefetch=1,
        grid=(n // tq,),
        in_specs=[pl.BlockSpec((nq, tq, LANE), lambda i, sq: (q_slab0 // nq, i, 0)),
                  pl.BlockSpec((2 * nq, n_mem, LANE), lambda i, sq: (0, sq[i], 0))],
        out_specs=pl.BlockSpec((tq, nq * LANE), lambda i, sq: (i, 0)),
    )
    return pl.pallas_call(
        functools.partial(_mem_attn_kernel, scale=(2 * LANE) ** -0.5),
        grid_spec=grid_spec,
        out_shape=jax.ShapeDtypeStruct((n, nq * LANE), BF16),
        compiler_params=_cparams("parallel"),
        name="memory_attention",
    )(blk_seq, proj, kv)


def _s5_param_kernel(are_ref, aim_ref, ldt_ref, bre_ref, bim_ref, lre_ref, lim_ref, bbre_ref, bbim_ref):
    a_re = are_ref[...]
    a_im = aim_ref[...]
    dt = jnp.exp(ldt_ref[...])
    mag = jnp.exp(a_re * dt)
    ang = a_im * dt
    lam_re = mag * jnp.cos(ang)
    lam_im = mag * jnp.sin(ang)
    den = a_re * a_re + a_im * a_im
    n_re = lam_re - 1.0
    coef_re = (n_re * a_re + lam_im * a_im) / den
    coef_im = (lam_im * a_re - n_re * a_im) / den
    lre_ref[...] = lam_re
    lim_ref[...] = lam_im
    for c in range(bre_ref.shape[0]):
        b_re = bre_ref[c]
        b_im = bim_ref[c]
        bbre_ref[c] = coef_re * b_re - coef_im * b_im
        bbim_ref[c] = coef_re * b_im + coef_im * b_re


def s5_discretise(a_re, a_im, log_dt, b_re, b_im):
    two, g, p = a_re.shape
    c = b_re.shape[-1]
    rows = two * g
    ldt = jnp.broadcast_to(log_dt.reshape(rows, 1), (rows, p))
    b_re_t = b_re.reshape(rows, p, c).transpose(2, 0, 1)
    b_im_t = b_im.reshape(rows, p, c).transpose(2, 0, 1)
    return pl.pallas_call(
        _s5_param_kernel,
        out_shape=(jax.ShapeDtypeStruct((rows, p), F32), jax.ShapeDtypeStruct((rows, p), F32),
                   jax.ShapeDtypeStruct((c, rows, p), F32), jax.ShapeDtypeStruct((c, rows, p), F32)),
        name="s5_discretise",
    )(a_re.reshape(rows, p), a_im.reshape(rows, p), ldt, b_re_t, b_im_t)


def _s5_operands(lam_re, lam_im, bb_re, bb_im, c_re, c_im, d):
    c, rows, p = bb_re.shape
    g = rows // 2
    gs = LANE // c
    ns = g // gs
    eye = jnp.eye(gs, dtype=F32)

    def bmat(bb):
        x = bb.reshape(c, 2, ns, gs, p).transpose(1, 2, 3, 0, 4)
        return jnp.einsum('dsgcp,gh->dsgchp', x, eye).reshape(2, ns, gs * c, gs * p)

    def cmat(cc):
        x = cc.astype(F32).reshape(2, ns, gs, c, p)
        return jnp.einsum('dsgcp,gh->dsgphc', x, eye).reshape(2, ns, gs * p, gs * c)

    b_op = jnp.concatenate([bmat(bb_re), bmat(bb_im)], axis=-1).astype(BF16)
    c_op = jnp.concatenate([cmat(c_re), -cmat(c_im)], axis=-2).astype(BF16)
    lam = jnp.concatenate([lam_re.reshape(2, ns, gs * p), lam_im.reshape(2, ns, gs * p)], axis=-1)
    lam = jnp.broadcast_to(lam[:, :, None, :], (2, ns, SUBLANE, 2 * gs * p))
    d_op = d.astype(F32).reshape(ns, 1, gs * c)
    return b_op, c_op, lam, d_op


def _s5_kernel(keep_ref, u_hbm, bm_ref, cm_ref, lam_ref, d_ref, o_ref,
               u_scr, u_sem, up_scr, y_scr, y2_scr, bu0_scr, bu1_scr, xs0_scr, xs1_scr, e_scr, i_scr, *, lseg, tblk):
    sup = pl.program_id(0)
    nblk = lseg // tblk
    rb = SUBLANE * tblk
    ns = bu0_scr.shape[1] // 2
    nchunk = ns // LANE
    d_row = d_ref[0]
    pitch = u_scr.shape[1] // S5_SEGMENTS
    n_slab = pl.num_programs(1)
    lin = sup * n_slab + pl.program_id(1)
    slot = lin % 2

    def segment_copies(step, slot):
        b = step // n_slab
        s = step % n_slab
        return [pltpu.make_async_copy(u_hbm.at[s, pl.ds(b * (S5_SEGMENTS * lseg) + r * lseg, lseg), :],
                                      u_scr.at[slot, pl.ds(r * pitch, lseg), :], u_sem.at[slot])
                for r in range(S5_SEGMENTS)]

    @pl.when(lin == 0)
    def _():
        for cp in segment_copies(lin, slot):
            cp.start()

    for cp in segment_copies(lin, slot):
        cp.wait()

    @pl.when(lin + 1 < pl.num_programs(0) * n_slab)
    def _():
        for cp in segment_copies(lin + 1, 1 - slot):
            cp.start()

    u2 = u_scr.at[slot]

    def permute_in(t2, carry):
        t = t2 * 2
        a = u2[pl.ds(t, SUBLANE, stride=pitch), :]
        b = u2[pl.ds(t + 1, SUBLANE, stride=pitch), :]
        ab = jnp.concatenate([a, b], axis=0)
        rows = pl.ds(pl.multiple_of(t * SUBLANE, 2 * SUBLANE), 2 * SUBLANE)
        up_scr[rows, :] = ab.astype(BF16)
        y_scr[rows, :] = ab * d_row
        return carry
    lax.fori_loop(0, lseg // 2, permute_in, 0, unroll=4)
    xs1_scr[...] = jnp.zeros(xs1_scr.shape, xs1_scr.dtype)

    for direction in range(2):
        rev = direction == 1
        lam_r = [lam_ref[direction, 0, :, c * LANE:(c + 1) * LANE] for c in range(nchunk)]
        lam_i = [lam_ref[direction, 0, :, ns + c * LANE:ns + (c + 1) * LANE] for c in range(nchunk)]
        steps = list(range(tblk))[::-1] if rev else list(range(tblk))

        def block_rows(k, rev=rev):
            k = jnp.clip(k, 0, nblk - 1)
            blk = (nblk - 1 - k) if rev else k
            return pl.ds(pl.multiple_of(blk * rb, rb), rb)

        def bu_dot(k, direction=direction, block_rows=block_rows):
            return jnp.dot(up_scr[block_rows(k), :], bm_ref[direction, 0], preferred_element_type=F32)

        def scan_block(bu, x, xs, lam_r=lam_r, lam_i=lam_i, steps=steps):
            xr, xi = list(x[:nchunk]), list(x[nchunk:])
            held = None
            for n, t in enumerate(steps):
                rows = slice(t * SUBLANE, (t + 1) * SUBLANE)
                for c in range(nchunk):
                    b_r = bu[rows, c * LANE:(c + 1) * LANE]
                    b_i = bu[rows, ns + c * LANE:ns + (c + 1) * LANE]
                    r = lam_r[c] * xr[c] - lam_i[c] * xi[c] + b_r
                    i = lam_r[c] * xi[c] + lam_i[c] * xr[c] + b_i
                    xr[c], xi[c] = r, i
                if xs is None:
                    continue
                if n % 2 == 0:
                    held = (t, list(xr), list(xi))
                    continue
                lo, hi = (held, (t, xr, xi)) if held[0] < t else ((t, xr, xi), held)
                pair = slice(lo[0] * SUBLANE, (lo[0] + 2) * SUBLANE)
                for c in range(nchunk):
                    xs[pair, c * LANE:(c + 1) * LANE] = jnp.concatenate([lo[1][c], hi[1][c]], axis=0).astype(BF16)
                    xs[pair, ns + c * LANE:ns + (c + 1) * LANE] = (
                        jnp.concatenate([lo[2][c], hi[2][c]], axis=0).astype(BF16))
            return tuple(xr + xi)

        def pass_a(k2, x, bu_dot=bu_dot, scan_block=scan_block):
            bu1_scr[...] = bu_dot(2 * k2 + 1)
            x = scan_block(bu0_scr, x, None)
            bu0_scr[...] = bu_dot(2 * k2 + 2)
            return scan_block(bu1_scr, x, None)
        bu0_scr[...] = bu_dot(0)
        zero = tuple(jnp.zeros((SUBLANE, LANE), F32) for _ in range(2 * nchunk))
        ends = lax.fori_loop(0, nblk // 2, pass_a, zero)
        for c in range(2 * nchunk):
            e_scr[:, c * LANE:(c + 1) * LANE] = ends[c]

        p_r = lam_ref[direction, 0, 0:1, 0:ns]
        p_i = lam_ref[direction, 0, 0:1, ns:2 * ns]
        for _ in range(int(math.log2(lseg))):
            p_r, p_i = p_r * p_r - p_i * p_i, 2.0 * p_r * p_i
        order = list(range(S5_SEGMENTS))[::-1] if rev else list(range(S5_SEGMENTS))
        s_r = jnp.zeros((1, ns), F32)
        s_i = jnp.zeros((1, ns), F32)
        i_scr[order[0]:order[0] + 1, :] = jnp.zeros((1, 2 * ns), F32)
        for prev, cur in zip(order[:-1], order[1:]):
            e_r = e_scr[prev:prev + 1, 0:ns]
            e_i = e_scr[prev:prev + 1, ns:2 * ns]
            keep = keep_ref[(sup * 2 + direction) * S5_SEGMENTS + cur].astype(F32)
            n_r = (p_r * s_r - p_i * s_i + e_r) * keep
            n_i = (p_r * s_i + p_i * s_r + e_i) * keep
            i_scr[cur:cur + 1, 0:ns] = n_r
            i_scr[cur:cur + 1, ns:2 * ns] = n_i
            s_r, s_i = n_r, n_i

        def project(k, xs, scale, direction=direction, block_rows=block_rows):
            rows = block_rows(k)
            yb = jnp.dot(xs[...], cm_ref[direction, 0], preferred_element_type=F32)
            y_scr[rows, :] = y_scr[rows, :] + yb * scale

        def pass_b(k2, x, bu_dot=bu_dot, scan_block=scan_block, project=project):
            bu1_scr[...] = bu_dot(2 * k2 + 1)
            x = scan_block(bu0_scr, x, xs0_scr)
            project(2 * k2 - 1, xs1_scr, jnp.where(k2 > 0, 1.0, 0.0))
            bu0_scr[...] = bu_dot(2 * k2 + 2)
            x = scan_block(bu1_scr, x, xs1_scr)
            project(2 * k2, xs0_scr, 1.0)
            return x
        bu0_scr[...] = bu_dot(0)
        start = tuple(i_scr[:, c * LANE:(c + 1) * LANE] for c in range(2 * nchunk))
        lax.fori_loop(0, nblk // 2, pass_b, start)
        project(nblk - 1, xs1_scr, 1.0)

    def permute_out(t, carry):
        y = y_scr[pl.ds(pl.multiple_of(t * SUBLANE, SUBLANE), SUBLANE), :]
        gelu = 0.5 * y * (1.0 + jnp.tanh(math.sqrt(2.0 / math.pi) * (y + 0.044715 * (y * y * y))))
        y2_scr[pl.ds(t, SUBLANE, stride=pitch), :] = gelu
        return carry
    lax.fori_loop(0, lseg, permute_out, 0, unroll=8)
    for r in range(S5_SEGMENTS):
        o_ref[r * lseg:(r + 1) * lseg, :] = y2_scr[r * pitch:r * pitch + lseg, :].astype(o_ref.dtype)


def _s5_keep_table(seq_lens, lseg):
    starts, ends = set(), set()
    tok = 0
    for length in seq_lens:
        assert length % lseg == 0
        starts.add(tok // lseg)
        tok += length
        ends.add(tok // lseg - 1)
    nseg = tok // lseg
    assert nseg % S5_SEGMENTS == 0
    keep = np.ones((nseg // S5_SEGMENTS, 2, S5_SEGMENTS), np.int32)
    for s in range(nseg):
        if s in starts:
            keep[s // S5_SEGMENTS, 0, s % S5_SEGMENTS] = 0
        if s in ends:
            keep[s // S5_SEGMENTS, 1, s % S5_SEGMENTS] = 0
    return keep.reshape(-1)


def s5_mixer(proj, b_op, c_op, lam, d_op, seq_lens):
    n = proj.shape[1]
    ns = b_op.shape[1]
    nstate2 = b_op.shape[-1]
    nsup = n // SUPER
    lseg = SUPER // S5_SEGMENTS
    tblk = min(S5_TBLK, lseg)
    assert lseg & (lseg - 1) == 0 and lseg % (2 * tblk) == 0 and tblk % 2 == 0
    keep = _s5_keep_table(seq_lens, lseg)
    rows = SUBLANE * tblk
    pitch = lseg + SUBLANE
    grid_spec = pltpu.PrefetchScalarGridSpec(
        num_scalar_prefetch=1,
        grid=(nsup, ns),
        in_specs=[
            pl.BlockSpec(memory_space=pl.ANY),
            pl.BlockSpec((2, 1, LANE, nstate2), lambda b, s, kp: (0, s, 0, 0)),
            pl.BlockSpec((2, 1, nstate2, LANE), lambda b, s, kp: (0, s, 0, 0)),
            pl.BlockSpec((2, 1, SUBLANE, nstate2), lambda b, s, kp: (0, s, 0, 0)),
            pl.BlockSpec((1, 1, LANE), lambda b, s, kp: (s, 0, 0)),
        ],
        out_specs=pl.BlockSpec((SUPER, LANE), lambda b, s, kp: (b, s)),
        scratch_shapes=[
            pltpu.VMEM((2, S5_SEGMENTS * pitch, LANE), F32),
            pltpu.SemaphoreType.DMA((2,)),
            pltpu.VMEM((SUPER, LANE), BF16),
            pltpu.VMEM((SUPER, LANE), F32),
            pltpu.VMEM((S5_SEGMENTS * pitch, LANE), F32),
            pltpu.VMEM((rows, nstate2), F32),
            pltpu.VMEM((rows, nstate2), F32),
            pltpu.VMEM((rows, nstate2), BF16),
            pltpu.VMEM((rows, nstate2), BF16),
            pltpu.VMEM((SUBLANE, nstate2), F32),
            pltpu.VMEM((SUBLANE, nstate2), F32),
        ],
    )
    return pl.pallas_call(
        functools.partial(_s5_kernel, lseg=lseg, tblk=tblk),
        grid_spec=grid_spec,
        out_shape=jax.ShapeDtypeStruct((n, ns * LANE), BF16),
        compiler_params=_cparams("arbitrary", "arbitrary"),
        name="s5_mixer",
    )(jnp.asarray(keep), proj, b_op, c_op, lam, d_op)


def _glu_kernel(g_ref, w_ref, b_ref, o_ref):
    tn = o_ref.shape[1]
    col = pl.multiple_of(pl.program_id(1) * tn, tn)
    acc = jnp.dot(g_ref[...], w_ref[...], preferred_element_type=F32) + b_ref[...]
    g = g_ref[:, pl.ds(col, tn)].astype(F32)
    o_ref[...] = (g * _sigmoid(acc)).astype(o_ref.dtype)


def glu(g, w, b, tm=1024, tn=1024):
    m, k = g.shape
    assert m % tm == 0 and k % tn == 0
    return pl.pallas_call(
        _glu_kernel,
        grid=(m // tm, k // tn),
        in_specs=[pl.BlockSpec((tm, k), lambda i, j: (i, 0)),
                  pl.BlockSpec((k, tn), lambda i, j: (0, j)),
                  pl.BlockSpec((1, tn), lambda i, j: (0, j))],
        out_specs=pl.BlockSpec((tm, tn), lambda i, j: (i, j)),
        out_shape=jax.ShapeDtypeStruct((m, k), BF16),
        compiler_params=_cparams("parallel", "arbitrary"),
        name="glu",
    )(g, w, b.reshape(1, k).astype(F32))


def _out_proj_kernel(x_ref, tok_ref, mo_ref, w1_ref, w2_ref, o_ref):
    acc = jnp.dot(tok_ref[...], w1_ref[...], preferred_element_type=F32)
    acc = acc + jnp.dot(mo_ref[...], w2_ref[...], preferred_element_type=F32)
    o_ref[...] = x_ref[...] + acc


def out_proj(x, tok, mo, w, tm=1024, tn=512):
    m, d = x.shape
    k1, k2 = tok.shape[1], mo.shape[1]
    assert m % tm == 0 and d % tn == 0 and w.shape[0] == k1 + k2 and k1 % k2 == 0
    return pl.pallas_call(
        _out_proj_kernel,
        grid=(m // tm, d // tn),
        in_specs=[pl.BlockSpec((tm, tn), lambda i, j: (i, j)),
                  pl.BlockSpec((tm, k1), lambda i, j: (i, 0)),
                  pl.BlockSpec((tm, k2), lambda i, j: (i, 0)),
                  pl.BlockSpec((k1, tn), lambda i, j: (0, j)),
                  pl.BlockSpec((k2, tn), lambda i, j: (k1 // k2, j))],
        out_specs=pl.BlockSpec((tm, tn), lambda i, j: (i, j)),
        out_shape=jax.ShapeDtypeStruct((m, d), F32),
        compiler_params=_cparams("parallel", "arbitrary"),
        name="out_proj",
    )(x, tok, mo, w, w)


def _router_kernel(x_ref, g_ref, whi_ref, wlo_ref, b_ref, h_ref, meta_ref):
    t = _rms_norm(x_ref[...], g_ref[...])
    h_ref[...] = _pack_bf16_pairs(t)
    t_hi, t_lo = _split_hi_lo(t)
    logits = (jnp.dot(t_hi, whi_ref[...], preferred_element_type=F32)
              + jnp.dot(t_lo, whi_ref[...], preferred_element_type=F32)
              + jnp.dot(t_hi, wlo_ref[...], preferred_element_type=F32)) + b_ref[...]
    lane = lax.broadcasted_iota(jnp.int32, logits.shape, 1)
    big = jnp.int32(LANE)
    is_g = lane < N_EXPERT_GROUPS
    gl = jnp.where(is_g, logits, -jnp.inf)
    gmax = jnp.max(gl, axis=-1, keepdims=True)
    gsum = jnp.sum(jnp.where(is_g, jnp.exp(gl - gmax), 0.0), axis=-1, keepdims=True)
    g_val = 1.0 / gsum
    g_idx = jnp.min(jnp.where(gl == gmax, lane, big), axis=-1, keepdims=True)
    e_lane = lane - N_EXPERT_GROUPS
    in_grp = (e_lane >= 0) & (e_lane < N_EXPERTS) & ((e_lane >> 2) == g_idx)
    el = jnp.where(in_grp, logits, -jnp.inf)
    v1 = jnp.max(el, axis=-1, keepdims=True)
    i1 = jnp.min(jnp.where(el == v1, lane, big), axis=-1, keepdims=True)
    el2 = jnp.where(lane == i1, -jnp.inf, el)
    v2 = jnp.max(el2, axis=-1, keepdims=True)
    i2 = jnp.min(jnp.where(el2 == v2, lane, big), axis=-1, keepdims=True)
    z = jnp.exp(v2 - v1)
    w1 = g_val / (1.0 + z)
    w2 = g_val * z / (1.0 + z)
    e1 = (i1 - N_EXPERT_GROUPS).astype(F32)
    e2 = (i2 - N_EXPERT_GROUPS).astype(F32)
    meta = jnp.where(lane == 0, w1, jnp.where(lane == 1, w2, jnp.where(lane == 2, e1, jnp.where(lane == 3, e2, 0.0))))
    meta_ref[...] = meta


def router(x, gain, w_rg, b_rg, w_re, b_re, tm=512):
    m, d = x.shape
    assert EXPERTS_PER_GROUP == 4 and m % tm == 0
    pad = LANE - N_EXPERT_GROUPS - N_EXPERTS
    w = jnp.concatenate([w_rg, w_re, jnp.zeros((d, pad), F32)], axis=1).astype(F32)
    w_hi, w_lo = _split_hi_lo(w)
    b = jnp.concatenate([b_rg, b_re, jnp.zeros((pad,), F32)]).astype(F32).reshape(1, LANE)
    return pl.pallas_call(
        _router_kernel,
        grid=(m // tm,),
        in_specs=[pl.BlockSpec((tm, d), lambda i: (i, 0)),
                  pl.BlockSpec((1, d), lambda i: (0, 0)),
                  pl.BlockSpec((d, LANE), lambda i: (0, 0)),
                  pl.BlockSpec((d, LANE), lambda i: (0, 0)),
                  pl.BlockSpec((1, LANE), lambda i: (0, 0))],
        out_specs=(pl.BlockSpec((tm, d // 2), lambda i: (i, 0)),
                   pl.BlockSpec((tm, LANE), lambda i: (i, 0))),
        out_shape=(jax.ShapeDtypeStruct((m, d // 2), U32), jax.ShapeDtypeStruct((m, LANE), F32)),
        compiler_params=_cparams("parallel"),
        name="moe_router",
    )(x, gain.reshape(1, d), w_hi, w_lo, b)


def _routing_plan(meta, n_rows):
    n = meta.shape[0]
    e = meta[:, 2:4].astype(jnp.int32)
    onehot = (e[:, :, None] == jnp.arange(N_EXPERTS)[None, None, :]).any(axis=1).astype(jnp.int32)
    counts = onehot.sum(axis=0)
    rank = jnp.cumsum(onehot, axis=0) - onehot
    padded = ((counts + MOE_TILE - 1) // MOE_TILE) * MOE_TILE
    ends = jnp.cumsum(padded)
    offs = ends - padded
    pos = jnp.take_along_axis(offs[None, :] + rank, e, axis=1).astype(jnp.int32)
    tok = jnp.arange(n, dtype=jnp.int32)
    sorted_tok = jnp.zeros((n_rows,), jnp.int32).at[pos.T.reshape(-1)].set(jnp.concatenate([tok, tok]))
    tile_start = jnp.arange(n_rows // MOE_TILE, dtype=jnp.int32) * MOE_TILE
    tile_expert = jnp.minimum((ends[None, :] <= tile_start[:, None]).sum(axis=1), N_EXPERTS - 1).astype(jnp.int32)
    tile_valid = (tile_start < ends[-1]).astype(jnp.int32)
    return pos[:, 0], pos[:, 1], sorted_tok, tile_expert, tile_valid


def _row_gather_start(idx_ref, base, src_ref, dst_ref, sem):
    for r in range(dst_ref.shape[0]):
        pltpu.make_async_copy(src_ref.at[pl.ds(idx_ref[base + r], 1), :], dst_ref.at[pl.ds(r, 1), :], sem).start()


def _row_gather_wait(src_ref, dst_ref, sem):
    pltpu.make_async_copy(src_ref.at[pl.ds(0, dst_ref.shape[0]), :], dst_ref, sem).wait()


def _moe_up_kernel(st_ref, te_ref, tv_ref, h_ref, wg_ref, wu_ref, o_ref, xbuf, sem):
    del te_ref
    t = pl.program_id(0)
    last = pl.num_programs(0) - 1
    slot = t % 2
    tile = xbuf.shape[1]

    @pl.when(t == 0)
    def _():
        _row_gather_start(st_ref, 0, h_ref, xbuf.at[0], sem.at[0])

    @pl.when(tv_ref[t] == 1)
    def _():
        _row_gather_wait(h_ref, xbuf.at[slot], sem.at[slot])
        nxt = jnp.minimum(t + 1, last)
        _row_gather_start(st_ref, nxt * tile, h_ref, xbuf.at[1 - slot], sem.at[1 - slot])
        lo, hi = _unpack_bf16_pairs(xbuf[slot])
        lo = lo.astype(BF16)
        hi = hi.astype(BF16)
        half = lo.shape[1]
        g = jnp.dot(lo, wg_ref[0, :half], preferred_element_type=F32)
        g = g + jnp.dot(hi, wg_ref[0, half:], preferred_element_type=F32)
        u = jnp.dot(lo, wu_ref[0, :half], preferred_element_type=F32)
        u = u + jnp.dot(hi, wu_ref[0, half:], preferred_element_type=F32)
        o_ref[...] = (g * _sigmoid(g) * u).astype(o_ref.dtype)

        @pl.when(t == last)
        def _():
            _row_gather_wait(h_ref, xbuf.at[1 - slot], sem.at[1 - slot])

    @pl.when(tv_ref[t] == 0)
    def _():
        @pl.when(tv_ref[jnp.maximum(t - 1, 0)] == 1)
        def _():
            _row_gather_wait(h_ref, xbuf.at[slot], sem.at[slot])
        o_ref[...] = jnp.zeros(o_ref.shape, o_ref.dtype)


def moe_up(h, w_gate, w_up, sorted_tok, tile_expert, tile_valid):
    half = h.shape[1]
    d = 2 * half
    f = w_gate.shape[-1]
    n_tiles = tile_expert.shape[0]
    grid_spec = pltpu.PrefetchScalarGridSpec(
        num_scalar_prefetch=3,
        grid=(n_tiles,),
        in_specs=[pl.BlockSpec(memory_space=pl.ANY),
                  pl.BlockSpec((1, d, f), lambda t, st, te, tv: (te[t], 0, 0)),
                  pl.BlockSpec((1, d, f), lambda t, st, te, tv: (te[t], 0, 0))],
        out_specs=pl.BlockSpec((MOE_TILE, f), lambda t, st, te, tv: (t, 0)),
        scratch_shapes=[pltpu.VMEM((2, MOE_TILE, half), U32), pltpu.SemaphoreType.DMA((2,))],
    )
    return pl.pallas_call(
        _moe_up_kernel,
        grid_spec=grid_spec,
        out_shape=jax.ShapeDtypeStruct((n_tiles * MOE_TILE, f), BF16),
        compiler_params=_cparams("arbitrary"),
        name="moe_up",
    )(sorted_tok, tile_expert, tile_valid, h, w_gate, w_up)


def _moe_down_kernel(te_ref, tv_ref, h_ref, wd_ref, o_ref):
    del te_ref
    t = pl.program_id(0)

    @pl.when(tv_ref[t] == 1)
    def _():
        y = jnp.dot(h_ref[...], wd_ref[0], preferred_element_type=F32)
        o_ref[...] = _pack_bf16_pairs(y)

    @pl.when(tv_ref[t] == 0)
    def _():
        o_ref[...] = jnp.zeros(o_ref.shape, o_ref.dtype)


def moe_down(hid, w_down, tile_expert, tile_valid):
    n_rows, f = hid.shape
    d = w_down.shape[-1]
    grid_spec = pltpu.PrefetchScalarGridSpec(
        num_scalar_prefetch=2,
        grid=(n_rows // MOE_TILE,),
        in_specs=[pl.BlockSpec((MOE_TILE, f), lambda t, te, tv: (t, 0)),
                  pl.BlockSpec((1, f, d), lambda t, te, tv: (te[t], 0, 0))],
        out_specs=pl.BlockSpec((MOE_TILE, d // 2), lambda t, te, tv: (t, 0)),
    )
    return pl.pallas_call(
        _moe_down_kernel,
        grid_spec=grid_spec,
        out_shape=jax.ShapeDtypeStruct((n_rows, d // 2), U32),
        compiler_params=_cparams("arbitrary"),
        name="moe_down",
    )(tile_expert, tile_valid, hid, w_down)


def _combine_kernel(p0_ref, p1_ref, x_ref, meta_ref, g_ref, ys_ref, o_ref, buf0, buf1, sem, *, base, final_norm):
    i = pl.program_id(0)
    last = pl.num_programs(0) - 1
    tc = buf0.shape[1]

    def start(tile, buf, sems):
        n0 = base + tile * tc
        _row_gather_start(p0_ref, n0, ys_ref, buf.at[0], sems.at[0])
        _row_gather_start(p1_ref, n0, ys_ref, buf.at[1], sems.at[1])

    def wait(buf, sems):
        _row_gather_wait(ys_ref, buf.at[0], sems.at[0])
        _row_gather_wait(ys_ref, buf.at[1], sems.at[1])

    @pl.when(i == 0)
    def _():
        start(0, buf0, sem.at[0])

    def run(cur, cur_sems, nxt, nxt_sems):
        wait(cur, cur_sems)
        lo0, hi0 = _unpack_bf16_pairs(cur[0])
        lo1, hi1 = _unpack_bf16_pairs(cur[1])
        start(jnp.minimum(i + 1, last), nxt, nxt_sems)
        meta = meta_ref[...]
        w0 = meta[:, 0:1]
        w1 = meta[:, 1:2]
        half = lo0.shape[1]
        y_lo = x_ref[:, :half] + w0 * lo0 + w1 * lo1
        y_hi = x_ref[:, half:] + w0 * hi0 + w1 * hi1
        if final_norm:
            ms = (jnp.sum(y_lo * y_lo, axis=-1, keepdims=True)
                  + jnp.sum(y_hi * y_hi, axis=-1, keepdims=True)) / (2 * half)
            inv = lax.rsqrt(ms + RMS_EPS)
            y_lo = y_lo * inv * g_ref[:, :half]
            y_hi = y_hi * inv * g_ref[:, half:]
        o_ref[:, :half] = y_lo
        o_ref[:, half:] = y_hi

        @pl.when(i == last)
        def _():
            wait(nxt, nxt_sems)

    @pl.when(i % 2 == 0)
    def _():
        run(buf0, sem.at[0], buf1, sem.at[1])

    @pl.when(i % 2 == 1)
    def _():
        run(buf1, sem.at[1], buf0, sem.at[0])


def moe_combine(x, meta, ys, pos0, pos1, gain, *, base, count, final_norm, tc=256):
    d = x.shape[1]
    assert base % tc == 0 and count % tc == 0
    blk0 = base // tc
    grid_spec = pltpu.PrefetchScalarGridSpec(
        num_scalar_prefetch=2,
        grid=(count // tc,),
        in_specs=[pl.BlockSpec((tc, d), lambda i, p0, p1: (blk0 + i, 0)),
                  pl.BlockSpec((tc, LANE), lambda i, p0, p1: (blk0 + i, 0)),
                  pl.BlockSpec((1, d), lambda i, p0, p1: (0, 0)),
                  pl.BlockSpec(memory_space=pl.ANY)],
        out_specs=pl.BlockSpec((tc, d), lambda i, p0, p1: (i, 0)),
        scratch_shapes=[pltpu.VMEM((2, tc, d // 2), U32), pltpu.VMEM((2, tc, d // 2), U32),
                        pltpu.SemaphoreType.DMA((2, 2))],
    )
    return pl.pallas_call(
        functools.partial(_combine_kernel, base=base, final_norm=final_norm),
        grid_spec=grid_spec,
        out_shape=jax.ShapeDtypeStruct((count, d), F32),
        compiler_params=_cparams("arbitrary"),
        name="moe_combine",
    )(pos0, pos1, x, meta, gain.reshape(1, d), ys)


def moe_experts(x, gain, w_rg, b_rg, w_re, b_re, w_gate, w_up, w_down, layer):
    n = x.shape[0]
    n_rows = 2 * n + N_EXPERTS * MOE_TILE
    assert (2 * n) % MOE_TILE == 0
    h, meta = router(x, gain, w_rg, b_rg, w_re, b_re)
    pos0, pos1, sorted_tok, tile_expert, tile_valid = _routing_plan(meta, n_rows)
    tile_expert = tile_expert + layer * N_EXPERTS
    hid = moe_up(h, w_gate, w_up, sorted_tok, tile_expert, tile_valid)
    ys = moe_down(hid, w_down, tile_expert, tile_valid)
    return meta, ys, pos0, pos1


def kernel(x_prompt, x_sample, mem_prompt, mem_sample, norm_mix, norm_mem, norm_ffn, norm_final, na_w_in, na_rpb, s5_w_in, s5_a_re, s5_a_im, s5_log_dt, s5_b_re, s5_b_im, s5_c_re, s5_c_im, s5_d, s5_w_glu, s5_b_glu, mem_w_kv, w_out, moe_w_rg, moe_b_rg, moe_w_re, moe_b_re, moe_w_gate, moe_w_up, moe_w_down):
    d = x_prompt.shape[-1]
    groups = [x_prompt, x_sample]
    mems = [mem_prompt, mem_sample]
    seq_lens = [g.shape[1] for g in groups for _ in range(g.shape[0])]
    n_mem = mem_prompt.shape[1]
    depth = norm_mix.shape[0]
    x = jnp.concatenate([g.reshape(-1, d) for g in groups], axis=0)
    mem = jnp.concatenate([m.reshape(-1, d) for m in mems], axis=0)
    n = x.shape[0]
    tq_mem = 512
    blk_seq = []
    for s, length in enumerate(seq_lens):
        assert length % tq_mem == 0
        blk_seq += [s] * (length // tq_mem)
    blk_seq = jnp.asarray(np.asarray(blk_seq, np.int32))
    na_heads = na_rpb.shape[1]
    w_gate_all = moe_w_gate.astype(BF16).reshape((-1,) + moe_w_gate.shape[2:])
    w_up_all = moe_w_up.astype(BF16).reshape((-1,) + moe_w_up.shape[2:])
    w_down_all = moe_w_down.astype(BF16).reshape((-1,) + moe_w_down.shape[2:])

    meta = ys = pos0 = pos1 = None
    for i in range(depth):
        j = i // 2
        if i > 0:
            x = moe_combine(x, meta, ys, pos0, pos1, norm_final, base=0, count=n, final_norm=False)
        kv = norm_proj(mem, norm_mem[i], mem_w_kv[i].astype(BF16), BF16, tm=n_mem)
        if i % 2 == 0:
            proj = norm_proj(x, norm_mix[i], na_w_in[j].astype(BF16), BF16, tm=512)
            tok = neighbourhood_attention(proj, na_rpb[j], seq_lens)
            q_slab0 = 3 * na_heads
        else:
            proj = norm_proj(x, norm_mix[i], s5_w_in[j].astype(BF16), F32, tm=512)
            lam_re, lam_im, bb_re, bb_im = s5_discretise(s5_a_re[j], s5_a_im[j], s5_log_dt[j], s5_b_re[j], s5_b_im[j])
            b_op, c_op, lam, d_op = _s5_operands(lam_re, lam_im, bb_re, bb_im, s5_c_re[j], s5_c_im[j], s5_d[j])
            g = s5_mixer(proj, b_op, c_op, lam, d_op, seq_lens)
            tok = glu(g, s5_w_glu[j].astype(BF16), s5_b_glu[j])
            q_slab0 = b_op.shape[1]
        mo = memory_attention(proj, q_slab0, kv, blk_seq, n_mem, tq=tq_mem)
        x = out_proj(x, tok, mo, w_out[i].astype(BF16))
        meta, ys, pos0, pos1 = moe_experts(x, norm_ffn[i], moe_w_rg[i], moe_b_rg[i], moe_w_re[i], moe_b_re[i],
                                           w_gate_all, w_up_all, w_down_all, i)
    outs, base = [], 0
    for g in groups:
        count = g.shape[0] * g.shape[1]
        y = moe_combine(x, meta, ys, pos0, pos1, norm_final, base=base, count=count, final_norm=True)
        outs.append(y.reshape(g.shape))
        base += count
    return tuple(outs)
```

```python
import functools
import math

import numpy as np
import jax
import jax.numpy as jnp
from jax import lax
from jax.experimental import pallas as pl
from jax.experimental.pallas import tpu as pltpu

F32 = jnp.float32
BF16 = jnp.bfloat16
U32 = jnp.uint32

LANE = 128
SUBLANE = 8
VMEM_LIMIT_BYTES = 56 * 1024 * 1024

RMS_EPS = 1e-6
NEG_INF = -1e30
GRID_W = 64
ROW_WIN = 8
COL_WIN = 16
NA_HEAD_DIM = 128
MEM_HEADS = 4
S5_GROUP = 16
S5_STATE = 64
N_EXPERT_GROUPS = 4
EXPERTS_PER_GROUP = 4
N_EXPERTS = N_EXPERT_GROUPS * EXPERTS_PER_GROUP

NA_Q_ROWS = 8
NA_K_ROWS = 2 * ROW_WIN
NA_HEADS_PER_STEP = 3
SUPER = 8192
S5_SEGMENTS = SUBLANE
S5_TBLK = 64
MOE_TILE = 256


def _cparams(*sem):
    return pltpu.CompilerParams(dimension_semantics=sem, vmem_limit_bytes=VMEM_LIMIT_BYTES)


def _rms_norm(x, gain):
    ms = jnp.mean(x * x, axis=-1, keepdims=True)
    return x * lax.rsqrt(ms + RMS_EPS) * gain


def _sigmoid(x):
    return 1.0 / (1.0 + jnp.exp(-x))


def _split_hi_lo(x):
    hi = lax.bitcast_convert_type(lax.bitcast_convert_type(x, U32) & jnp.uint32(0xFFFF0000), F32)
    return hi.astype(BF16), (x - hi).astype(BF16)


def _pack_bf16_pairs(x):
    k = x.shape[1] // 2
    bits = lax.bitcast_convert_type(x, U32)
    return (bits[:, k:] & jnp.uint32(0xFFFF0000)) | (bits[:, :k] >> 16)


def _unpack_bf16_pairs(w):
    lo = lax.bitcast_convert_type(w << 16, F32)
    hi = lax.bitcast_convert_type(w & jnp.uint32(0xFFFF0000), F32)
    return lo, hi


def _norm_proj_kernel(x_ref, g_ref, w_ref, o_ref, h_scr):
    @pl.when(pl.program_id(1) == 0)
    def _():
        h_scr[...] = _rms_norm(x_ref[...], g_ref[...]).astype(BF16)

    acc = jnp.dot(h_scr[...], w_ref[...], preferred_element_type=F32)
    for c in range(o_ref.shape[0]):
        o_ref[c] = acc[:, c * LANE:(c + 1) * LANE].astype(o_ref.dtype)


def norm_proj(x, gain, w, out_dtype, tm, tn=1024):
    m, d = x.shape
    n_out = w.shape[1]
    assert m % tm == 0 and n_out % tn == 0 and tn % LANE == 0
    return pl.pallas_call(
        _norm_proj_kernel,
        grid=(m // tm, n_out // tn),
        in_specs=[pl.BlockSpec((tm, d), lambda i, j: (i, 0)),
                  pl.BlockSpec((1, d), lambda i, j: (0, 0)),
                  pl.BlockSpec((d, tn), lambda i, j: (0, j))],
        out_specs=pl.BlockSpec((tn // LANE, tm, LANE), lambda i, j: (j, i, 0)),
        out_shape=jax.ShapeDtypeStruct((n_out // LANE, m, LANE), out_dtype),
        scratch_shapes=[pltpu.VMEM((tm, d), BF16)],
        compiler_params=_cparams("parallel", "arbitrary"),
        name="norm_proj",
    )(x, gain.reshape(1, d), w)


def _na_kernel(ks_ref, ty_ref, pi_ref, q_ref, k_ref, v_ref, b_ref, o_ref, *, nqb, scale):
    step = pl.program_id(1) * nqb + pl.program_id(2)
    start = pl.multiple_of(ks_ref[step] * GRID_W, GRID_W)
    nk = NA_K_ROWS * GRID_W
    npair = NA_K_ROWS // 2
    ty = ty_ref[step]
    for h in range(q_ref.shape[0]):
        q = q_ref[h]
        k = k_ref[h, pl.ds(start, nk), :]
        v = v_ref[h, pl.ds(start, nk), :]
        s = lax.dot_general(q, k, (((1,), (1,)), ((), ())), preferred_element_type=F32)
        bias = jnp.concatenate(
            [jnp.concatenate([b_ref[h, pi_ref[(ty * NA_Q_ROWS + i) * npair + jp]] for jp in range(npair)], axis=1)
             for i in range(NA_Q_ROWS)], axis=0)
        s = s * scale + bias
        m = jnp.max(s, axis=-1, keepdims=True)
        p = jnp.exp(s - m)
        l = jnp.sum(p, axis=-1, keepdims=True)
        o = jnp.dot(p.astype(BF16), v, preferred_element_type=F32) / l
        o_ref[:, h * LANE:(h + 1) * LANE] = o.astype(o_ref.dtype)


def _na_tables(seq_lens):
    ks, ty = [], []
    tok = 0
    per_super = SUPER // GRID_W
    for length in seq_lens:
        rows = length // GRID_W
        assert length % (NA_Q_ROWS * GRID_W) == 0 and rows >= 3 * NA_Q_ROWS
        assert tok // SUPER == (tok + length - 1) // SUPER
        row0 = (tok % SUPER) // GRID_W
        for qb in range(rows // NA_Q_ROWS):
            ks.append(row0 + int(np.clip(qb * NA_Q_ROWS - ROW_WIN // 2, 0, rows - NA_K_ROWS)))
            ty.append(0 if qb == 0 else (2 if qb == rows // NA_Q_ROWS - 1 else 1))
        tok += length
    assert tok % SUPER == 0 and len(ks) == (tok // SUPER) * (per_super // NA_Q_ROWS)
    return np.asarray(ks, np.int32), np.asarray(ty, np.int32)


def _na_bias(rpb):
    heads = rpb.shape[0]
    cols = np.arange(GRID_W)
    c_start = np.clip(cols - COL_WIN // 2, 0, GRID_W - COL_WIN)
    in_win = (cols[None, :] >= c_start[:, None]) & (cols[None, :] < c_start[:, None] + COL_WIN)
    dc = np.clip(cols[None, :] - cols[:, None], -(COL_WIN - 1), COL_WIN - 1) + COL_WIN - 1
    col_bias = jnp.where(jnp.asarray(in_win), rpb.astype(F32)[:, :, dc], NEG_INF)
    n_dr = 2 * ROW_WIN - 1
    masked = jnp.full((heads, n_dr, GRID_W, GRID_W), NEG_INF, F32)
    left = jnp.concatenate([col_bias[:, :n_dr - 1], masked, col_bias, masked[:, :1]], axis=1)
    right = jnp.concatenate([col_bias[:, 1:], col_bias, masked, masked[:, :1]], axis=1)
    pairs = jnp.concatenate([left, right], axis=-1)
    both, right_only, left_only, none = 0, n_dr - 1, 2 * n_dr - 1, 3 * n_dr - 1
    half = ROW_WIN // 2
    index = []
    for ty in range(3):
        for i in range(NA_Q_ROWS):
            q_rel = (i, i + half, i + ROW_WIN)[ty]
            w_rel = (max(i - half, 0), i, min(i + half, ROW_WIN))[ty]
            for jp in range(NA_K_ROWS // 2):
                j0, j1 = 2 * jp, 2 * jp + 1
                v0, v1 = w_rel <= j0 < w_rel + ROW_WIN, w_rel <= j1 < w_rel + ROW_WIN
                d0, d1 = j0 - q_rel + ROW_WIN - 1, j1 - q_rel + ROW_WIN - 1
                assert (not v0 or 0 <= d0 < n_dr) and (not v1 or 0 <= d1 < n_dr)
                index.append(both + d0 if v0 and v1 else right_only + d1 if v1 else left_only + d0 if v0 else none)
    return pairs, np.asarray(index, np.int32)


def neighbourhood_attention(qkv, rpb, seq_lens):
    heads = rpb.shape[0]
    n = qkv.shape[1]
    nsup = n // SUPER
    tq = NA_Q_ROWS * GRID_W
    nqb = SUPER // tq
    ks_tab, ty_tab = _na_tables(seq_lens)
    pairs, pair_index = _na_bias(rpb)
    hp = NA_HEADS_PER_STEP
    assert heads % hp == 0
    ng = heads // hp
    grid_spec = pltpu.PrefetchScalarGridSpec(
        num_scalar_prefetch=3,
        grid=(ng, nsup, nqb),
        in_specs=[
            pl.BlockSpec((hp, tq, LANE), lambda h, s, b, ks, ty, pi: (h, s * nqb + b, 0)),
            pl.BlockSpec((hp, SUPER, LANE), lambda h, s, b, ks, ty, pi: (ng + h, s, 0)),
            pl.BlockSpec((hp, SUPER, LANE), lambda h, s, b, ks, ty, pi: (2 * ng + h, s, 0)),
            pl.BlockSpec((hp,) + pairs.shape[1:], lambda h, s, b, ks, ty, pi: (h, 0, 0, 0)),
        ],
        out_specs=pl.BlockSpec((tq, hp * LANE), lambda h, s, b, ks, ty, pi: (s * nqb + b, h)),
    )
    return pl.pallas_call(
        functools.partial(_na_kernel, nqb=nqb, scale=NA_HEAD_DIM ** -0.5),
        grid_spec=grid_spec,
        out_shape=jax.ShapeDtypeStruct((n, heads * LANE), BF16),
        compiler_params=_cparams("parallel", "arbitrary", "arbitrary"),
        name="neighbourhood_attention",
    )(jnp.asarray(ks_tab), jnp.asarray(ty_tab), jnp.asarray(pair_index), qkv, qkv, qkv, pairs)


def _mem_attn_kernel(seq_ref, q_ref, kv_ref, o_ref, *, scale):
    del seq_ref
    hd = 2 * LANE
    for h in range(MEM_HEADS):
        q = jnp.concatenate([q_ref[2 * h], q_ref[2 * h + 1]], axis=-1).astype(BF16)
        k = jnp.concatenate([kv_ref[2 * h], kv_ref[2 * h + 1]], axis=-1)
        v = jnp.concatenate([kv_ref[2 * MEM_HEADS + 2 * h], kv_ref[2 * MEM_HEADS + 2 * h + 1]], axis=-1)
        s = lax.dot_general(q, k, (((1,), (1,)), ((), ())), preferred_element_type=F32) * scale
        m = jnp.max(s, axis=-1, keepdims=True)
        p = jnp.exp(s - m)
        l = jnp.sum(p, axis=-1, keepdims=True)
        o = jnp.dot(p.astype(BF16), v, preferred_element_type=F32) / l
        o_ref[:, h * hd:(h + 1) * hd] = o.astype(o_ref.dtype)


def memory_attention(proj, q_slab0, kv, blk_seq, n_mem, tq=512):
    n = proj.shape[1]
    nq = 2 * MEM_HEADS
    assert q_slab0 % nq == 0
    grid_spec = pltpu.PrefetchScalarGridSpec(
        num_scalar_prefetch=1,
        grid=(n // tq,),
        in_specs=[pl.BlockSpec((nq, tq, LANE), lambda i, sq: (q_slab0 // nq, i, 0)),
                  pl.BlockSpec((2 * nq, n_mem, LANE), lambda i, sq: (0, sq[i], 0))],
        out_specs=pl.BlockSpec((tq, nq * LANE), lambda i, sq: (i, 0)),
    )
    return pl.pallas_call(
        functools.partial(_mem_attn_kernel, scale=(2 * LANE) ** -0.5),
        grid_spec=grid_spec,
        out_shape=jax.ShapeDtypeStruct((n, nq * LANE), BF16),
        compiler_params=_cparams("parallel"),
        name="memory_attention",
    )(blk_seq, proj, kv)


def _s5_param_kernel(are_ref, aim_ref, ldt_ref, bre_ref, bim_ref, lre_ref, lim_ref, bbre_ref, bbim_ref):
    a_re = are_ref[...]
    a_im = aim_ref[...]
    dt = jnp.exp(ldt_ref[...])
    mag = jnp.exp(a_re * dt)
    ang = a_im * dt
    lam_re = mag * jnp.cos(ang)
    lam_im = mag * jnp.sin(ang)
    den = a_re * a_re + a_im * a_im
    n_re = lam_re - 1.0
    coef_re = (n_re * a_re + lam_im * a_im) / den
    coef_im = (lam_im * a_re - n_re * a_im) / den
    lre_ref[...] = lam_re
    lim_ref[...] = lam_im
    for c in range(bre_ref.shape[0]):
        b_re = bre_ref[c]
        b_im = bim_ref[c]
        bbre_ref[c] = coef_re * b_re - coef_im * b_im
        bbim_ref[c] = coef_re * b_im + coef_im * b_re


def s5_discretise(a_re, a_im, log_dt, b_re, b_im):
    two, g, p = a_re.shape
    c = b_re.shape[-1]
    rows = two * g
    ldt = jnp.broadcast_to(log_dt.reshape(rows, 1), (rows, p))
    b_re_t = b_re.reshape(rows, p, c).transpose(2, 0, 1)
    b_im_t = b_im.reshape(rows, p, c).transpose(2, 0, 1)
    return pl.pallas_call(
        _s5_param_kernel,
        out_shape=(jax.ShapeDtypeStruct((rows, p), F32), jax.ShapeDtypeStruct((rows, p), F32),
                   jax.ShapeDtypeStruct((c, rows, p), F32), jax.ShapeDtypeStruct((c, rows, p), F32)),
        name="s5_discretise",
    )(a_re.reshape(rows, p), a_im.reshape(rows, p), ldt, b_re_t, b_im_t)


def _s5_operands(lam_re, lam_im, bb_re, bb_im, c_re, c_im, d):
    c, rows, p = bb_re.shape
    g = rows // 2
    gs = LANE // c
    ns = g // gs
    same_group = jnp.eye(gs, dtype=jnp.bool_)

    def bmat(bb):
        x = bb.reshape(c, 2, ns, gs, p).transpose(1, 2, 3, 0, 4)
        blocks = jnp.where(same_group[None, None, :, None, :, None], x[:, :, :, :, None, :], 0.0)
        return blocks.reshape(2, ns, gs * c, gs * p)

    def cmat(cc):
        x = cc.astype(F32).reshape(2, ns, gs, c, p).transpose(0, 1, 2, 4, 3)
        blocks = jnp.where(same_group[None, None, :, None, :, None], x[:, :, :, :, None, :], 0.0)
        return blocks.reshape(2, ns, gs * p, gs * c)

    b_op = jnp.concatenate([bmat(bb_re), bmat(bb_im)], axis=-1).astype(BF16)
    c_op = jnp.concatenate([cmat(c_re), -cmat(c_im)], axis=-2).astype(BF16)
    lam = jnp.concatenate([lam_re.reshape(2, ns, gs * p), lam_im.reshape(2, ns, gs * p)], axis=-1)
    lam = jnp.broadcast_to(lam[:, :, None, :], (2, ns, SUBLANE, 2 * gs * p))
    d_op = d.astype(F32).reshape(ns, 1, gs * c)
    return b_op, c_op, lam, d_op


def _s5_kernel(keep_ref, u_hbm, bm_ref, cm_ref, lam_ref, d_ref, o_ref,
               u_scr, u_sem, up_scr, y_scr, y2_scr, bu0_scr, bu1_scr, xs0_scr, xs1_scr, e_scr, i_scr, *, lseg, tblk):
    sup = pl.program_id(0)
    nblk = lseg // tblk
    rb = SUBLANE * tblk
    ns = bu0_scr.shape[1] // 2
    nchunk = ns // LANE
    d_row = d_ref[0]
    pitch = u_scr.shape[1] // S5_SEGMENTS
    n_slab = pl.num_programs(1)
    lin = sup * n_slab + pl.program_id(1)
    slot = lin % 2

    def segment_copies(step, slot):
        b = step // n_slab
        s = step % n_slab
        return [pltpu.make_async_copy(u_hbm.at[s, pl.ds(b * (S5_SEGMENTS * lseg) + r * lseg, lseg), :],
                                      u_scr.at[slot, pl.ds(r * pitch, lseg), :], u_sem.at[slot])
                for r in range(S5_SEGMENTS)]

    @pl.when(lin == 0)
    def _():
        for cp in segment_copies(lin, slot):
            cp.start()

    for cp in segment_copies(lin, slot):
        cp.wait()

    @pl.when(lin + 1 < pl.num_programs(0) * n_slab)
    def _():
        for cp in segment_copies(lin + 1, 1 - slot):
            cp.start()

    u2 = u_scr.at[slot]

    def permute_in(t2, carry):
        t = t2 * 2
        a = u2[pl.ds(t, SUBLANE, stride=pitch), :]
        b = u2[pl.ds(t + 1, SUBLANE, stride=pitch), :]
        ab = jnp.concatenate([a, b], axis=0)
        rows = pl.ds(pl.multiple_of(t * SUBLANE, 2 * SUBLANE), 2 * SUBLANE)
        up_scr[rows, :] = ab.astype(BF16)
        y_scr[rows, :] = ab * d_row
        return carry
    lax.fori_loop(0, lseg // 2, permute_in, 0, unroll=4)
    xs1_scr[...] = jnp.zeros(xs1_scr.shape, xs1_scr.dtype)

    for direction in range(2):
        rev = direction == 1
        lam_r = [lam_ref[direction, 0, :, c * LANE:(c + 1) * LANE] for c in range(nchunk)]
        lam_i = [lam_ref[direction, 0, :, ns + c * LANE:ns + (c + 1) * LANE] for c in range(nchunk)]
        steps = list(range(tblk))[::-1] if rev else list(range(tblk))

        def block_rows(k, rev=rev):
            k = jnp.clip(k, 0, nblk - 1)
            blk = (nblk - 1 - k) if rev else k
            return pl.ds(pl.multiple_of(blk * rb, rb), rb)

        def bu_dot(k, direction=direction, block_rows=block_rows):
            return jnp.dot(up_scr[block_rows(k), :], bm_ref[direction, 0], preferred_element_type=F32)

        def scan_block(bu, x, xs, lam_r=lam_r, lam_i=lam_i, steps=steps):
            xr, xi = list(x[:nchunk]), list(x[nchunk:])
            held = None
            for n, t in enumerate(steps):
                rows = slice(t * SUBLANE, (t + 1) * SUBLANE)
                for c in range(nchunk):
                    b_r = bu[rows, c * LANE:(c + 1) * LANE]
                    b_i = bu[rows, ns + c * LANE:ns + (c + 1) * LANE]
                    r = lam_r[c] * xr[c] - lam_i[c] * xi[c] + b_r
                    i = lam_r[c] * xi[c] + lam_i[c] * xr[c] + b_i
                    xr[c], xi[c] = r, i
                if xs is None:
                    continue
                if n % 2 == 0:
                    held = (t, list(xr), list(xi))
                    continue
                lo, hi = (held, (t, xr, xi)) if held[0] < t else ((t, xr, xi), held)
                pair = slice(lo[0] * SUBLANE, (lo[0] + 2) * SUBLANE)
                for c in range(nchunk):
                    xs[pair, c * LANE:(c + 1) * LANE] = jnp.concatenate([lo[1][c], hi[1][c]], axis=0).astype(BF16)
                    xs[pair, ns + c * LANE:ns + (c + 1) * LANE] = (
                        jnp.concatenate([lo[2][c], hi[2][c]], axis=0).astype(BF16))
            return tuple(xr + xi)

        def pass_a(k2, x, bu_dot=bu_dot, scan_block=scan_block):
            bu1_scr[...] = bu_dot(2 * k2 + 1)
            x = scan_block(bu0_scr, x, None)
            bu0_scr[...] = bu_dot(2 * k2 + 2)
            return scan_block(bu1_scr, x, None)
        bu0_scr[...] = bu_dot(0)
        zero = tuple(jnp.zeros((SUBLANE, LANE), F32) for _ in range(2 * nchunk))
        ends = lax.fori_loop(0, nblk // 2, pass_a, zero)
        for c in range(2 * nchunk):
            e_scr[:, c * LANE:(c + 1) * LANE] = ends[c]

        p_r = lam_ref[direction, 0, 0:1, 0:ns]
        p_i = lam_ref[direction, 0, 0:1, ns:2 * ns]
        for _ in range(int(math.log2(lseg))):
            p_r, p_i = p_r * p_r - p_i * p_i, 2.0 * p_r * p_i
        order = list(range(S5_SEGMENTS))[::-1] if rev else list(range(S5_SEGMENTS))
        s_r = jnp.zeros((1, ns), F32)
        s_i = jnp.zeros((1, ns), F32)
        i_scr[order[0]:order[0] + 1, :] = jnp.zeros((1, 2 * ns), F32)
        for prev, cur in zip(order[:-1], order[1:]):
            e_r = e_scr[prev:prev + 1, 0:ns]
            e_i = e_scr[prev:prev + 1, ns:2 * ns]
            keep = keep_ref[(sup * 2 + direction) * S5_SEGMENTS + cur].astype(F32)
            n_r = (p_r * s_r - p_i * s_i + e_r) * keep
            n_i = (p_r * s_i + p_i * s_r + e_i) * keep
            i_scr[cur:cur + 1, 0:ns] = n_r
            i_scr[cur:cur + 1, ns:2 * ns] = n_i
            s_r, s_i = n_r, n_i

        def project(k, xs, scale, direction=direction, block_rows=block_rows):
            rows = block_rows(k)
            yb = jnp.dot(xs[...], cm_ref[direction, 0], preferred_element_type=F32)
            y_scr[rows, :] = y_scr[rows, :] + yb * scale

        def pass_b(k2, x, bu_dot=bu_dot, scan_block=scan_block, project=project):
            bu1_scr[...] = bu_dot(2 * k2 + 1)
            x = scan_block(bu0_scr, x, xs0_scr)
            project(2 * k2 - 1, xs1_scr, jnp.where(k2 > 0, 1.0, 0.0))
            bu0_scr[...] = bu_dot(2 * k2 + 2)
            x = scan_block(bu1_scr, x, xs1_scr)
            project(2 * k2, xs0_scr, 1.0)
            return x
        bu0_scr[...] = bu_dot(0)
        start = tuple(i_scr[:, c * LANE:(c + 1) * LANE] for c in range(2 * nchunk))
        lax.fori_loop(0, nblk // 2, pass_b, start)
        project(nblk - 1, xs1_scr, 1.0)

    def permute_out(t, carry):
        y = y_scr[pl.ds(pl.multiple_of(t * SUBLANE, SUBLANE), SUBLANE), :]
        gelu = 0.5 * y * (1.0 + jnp.tanh(math.sqrt(2.0 / math.pi) * (y + 0.044715 * (y * y * y))))
        y2_scr[pl.ds(t, SUBLANE, stride=pitch), :] = gelu
        return carry
    lax.fori_loop(0, lseg, permute_out, 0, unroll=8)
    for r in range(S5_SEGMENTS):
        o_ref[r * lseg:(r + 1) * lseg, :] = y2_scr[r * pitch:r * pitch + lseg, :].astype(o_ref.dtype)


def _s5_keep_table(seq_lens, lseg):
    starts, ends = set(), set()
    tok = 0
    for length in seq_lens:
        assert length % lseg == 0
        starts.add(tok // lseg)
        tok += length
        ends.add(tok // lseg - 1)
    nseg = tok // lseg
    assert nseg % S5_SEGMENTS == 0
    keep = np.ones((nseg // S5_SEGMENTS, 2, S5_SEGMENTS), np.int32)
    for s in range(nseg):
        if s in starts:
            keep[s // S5_SEGMENTS, 0, s % S5_SEGMENTS] = 0
        if s in ends:
            keep[s // S5_SEGMENTS, 1, s % S5_SEGMENTS] = 0
    return keep.reshape(-1)


def s5_mixer(proj, b_op, c_op, lam, d_op, seq_lens):
    n = proj.shape[1]
    ns = b_op.shape[1]
    nstate2 = b_op.shape[-1]
    nsup = n // SUPER
    lseg = SUPER // S5_SEGMENTS
    tblk = min(S5_TBLK, lseg)
    assert lseg & (lseg - 1) == 0 and lseg % (2 * tblk) == 0 and tblk % 2 == 0
    keep = _s5_keep_table(seq_lens, lseg)
    rows = SUBLANE * tblk
    pitch = lseg + SUBLANE
    grid_spec = pltpu.PrefetchScalarGridSpec(
        num_scalar_prefetch=1,
        grid=(nsup, ns),
        in_specs=[
            pl.BlockSpec(memory_space=pl.ANY),
            pl.BlockSpec((2, 1, LANE, nstate2), lambda b, s, kp: (0, s, 0, 0)),
            pl.BlockSpec((2, 1, nstate2, LANE), lambda b, s, kp: (0, s, 0, 0)),
            pl.BlockSpec((2, 1, SUBLANE, nstate2), lambda b, s, kp: (0, s, 0, 0)),
            pl.BlockSpec((1, 1, LANE), lambda b, s, kp: (s, 0, 0)),
        ],
        out_specs=pl.BlockSpec((SUPER, LANE), lambda b, s, kp: (b, s)),
        scratch_shapes=[
            pltpu.VMEM((2, S5_SEGMENTS * pitch, LANE), F32),
            pltpu.SemaphoreType.DMA((2,)),
            pltpu.VMEM((SUPER, LANE), BF16),
            pltpu.VMEM((SUPER, LANE), F32),
            pltpu.VMEM((S5_SEGMENTS * pitch, LANE), F32),
            pltpu.VMEM((rows, nstate2), F32),
            pltpu.VMEM((rows, nstate2), F32),
            pltpu.VMEM((rows, nstate2), BF16),
            pltpu.VMEM((rows, nstate2), BF16),
            pltpu.VMEM((SUBLANE, nstate2), F32),
            pltpu.VMEM((SUBLANE, nstate2), F32),
        ],
    )
    return pl.pallas_call(
        functools.partial(_s5_kernel, lseg=lseg, tblk=tblk),
        grid_spec=grid_spec,
        out_shape=jax.ShapeDtypeStruct((n, ns * LANE), BF16),
        compiler_params=_cparams("arbitrary", "arbitrary"),
        name="s5_mixer",
    )(jnp.asarray(keep), proj, b_op, c_op, lam, d_op)


def _glu_kernel(g_ref, w_ref, b_ref, o_ref):
    tn = o_ref.shape[1]
    col = pl.multiple_of(pl.program_id(1) * tn, tn)
    acc = jnp.dot(g_ref[...], w_ref[...], preferred_element_type=F32) + b_ref[...]
    g = g_ref[:, pl.ds(col, tn)].astype(F32)
    o_ref[...] = (g * _sigmoid(acc)).astype(o_ref.dtype)


def glu(g, w, b, tm=1024, tn=1024):
    m, k = g.shape
    assert m % tm == 0 and k % tn == 0
    return pl.pallas_call(
        _glu_kernel,
        grid=(m // tm, k // tn),
        in_specs=[pl.BlockSpec((tm, k), lambda i, j: (i, 0)),
                  pl.BlockSpec((k, tn), lambda i, j: (0, j)),
                  pl.BlockSpec((1, tn), lambda i, j: (0, j))],
        out_specs=pl.BlockSpec((tm, tn), lambda i, j: (i, j)),
        out_shape=jax.ShapeDtypeStruct((m, k), BF16),
        compiler_params=_cparams("parallel", "arbitrary"),
        name="glu",
    )(g, w, b.reshape(1, k).astype(F32))


def _out_proj_kernel(x_ref, tok_ref, mo_ref, w1_ref, w2_ref, o_ref):
    acc = jnp.dot(tok_ref[...], w1_ref[...], preferred_element_type=F32)
    acc = acc + jnp.dot(mo_ref[...], w2_ref[...], preferred_element_type=F32)
    o_ref[...] = x_ref[...] + acc


def out_proj(x, tok, mo, w, tm=1024, tn=512):
    m, d = x.shape
    k1, k2 = tok.shape[1], mo.shape[1]
    assert m % tm == 0 and d % tn == 0 and w.shape[0] == k1 + k2 and k1 % k2 == 0
    return pl.pallas_call(
        _out_proj_kernel,
        grid=(m // tm, d // tn),
        in_specs=[pl.BlockSpec((tm, tn), lambda i, j: (i, j)),
                  pl.BlockSpec((tm, k1), lambda i, j: (i, 0)),
                  pl.BlockSpec((tm, k2), lambda i, j: (i, 0)),
                  pl.BlockSpec((k1, tn), lambda i, j: (0, j)),
                  pl.BlockSpec((k2, tn), lambda i, j: (k1 // k2, j))],
        out_specs=pl.BlockSpec((tm, tn), lambda i, j: (i, j)),
        out_shape=jax.ShapeDtypeStruct((m, d), F32),
        compiler_params=_cparams("parallel", "arbitrary"),
        name="out_proj",
    )(x, tok, mo, w, w)


def _router_kernel(x_ref, g_ref, whi_ref, wlo_ref, b_ref, h_ref, meta_ref):
    t = _rms_norm(x_ref[...], g_ref[...])
    h_ref[...] = _pack_bf16_pairs(t)
    t_hi, t_lo = _split_hi_lo(t)
    logits = (jnp.dot(t_hi, whi_ref[...], preferred_element_type=F32)
              + jnp.dot(t_lo, whi_ref[...], preferred_element_type=F32)
              + jnp.dot(t_hi, wlo_ref[...], preferred_element_type=F32)) + b_ref[...]
    lane = lax.broadcasted_iota(jnp.int32, logits.shape, 1)
    big = jnp.int32(LANE)
    is_g = lane < N_EXPERT_GROUPS
    gl = jnp.where(is_g, logits, -jnp.inf)
    gmax = jnp.max(gl, axis=-1, keepdims=True)
    gsum = jnp.sum(jnp.where(is_g, jnp.exp(gl - gmax), 0.0), axis=-1, keepdims=True)
    g_val = 1.0 / gsum
    g_idx = jnp.min(jnp.where(gl == gmax, lane, big), axis=-1, keepdims=True)
    e_lane = lane - N_EXPERT_GROUPS
    in_grp = (e_lane >= 0) & (e_lane < N_EXPERTS) & ((e_lane >> 2) == g_idx)
    el = jnp.where(in_grp, logits, -jnp.inf)
    v1 = jnp.max(el, axis=-1, keepdims=True)
    i1 = jnp.min(jnp.where(el == v1, lane, big), axis=-1, keepdims=True)
    el2 = jnp.where(lane == i1, -jnp.inf, el)
    v2 = jnp.max(el2, axis=-1, keepdims=True)
    i2 = jnp.min(jnp.where(el2 == v2, lane, big), axis=-1, keepdims=True)
    z = jnp.exp(v2 - v1)
    w1 = g_val / (1.0 + z)
    w2 = g_val * z / (1.0 + z)
    e1 = (i1 - N_EXPERT_GROUPS).astype(F32)
    e2 = (i2 - N_EXPERT_GROUPS).astype(F32)
    meta = jnp.where(lane == 0, w1, jnp.where(lane == 1, w2, jnp.where(lane == 2, e1, jnp.where(lane == 3, e2, 0.0))))
    meta_ref[...] = meta


def router(x, gain, w_rg, b_rg, w_re, b_re, tm=512):
    m, d = x.shape
    assert EXPERTS_PER_GROUP == 4 and m % tm == 0
    pad = LANE - N_EXPERT_GROUPS - N_EXPERTS
    w = jnp.concatenate([w_rg, w_re, jnp.zeros((d, pad), F32)], axis=1).astype(F32)
    w_hi, w_lo = _split_hi_lo(w)
    b = jnp.concatenate([b_rg, b_re, jnp.zeros((pad,), F32)]).astype(F32).reshape(1, LANE)
    return pl.pallas_call(
        _router_kernel,
        grid=(m // tm,),
        in_specs=[pl.BlockSpec((tm, d), lambda i: (i, 0)),
                  pl.BlockSpec((1, d), lambda i: (0, 0)),
                  pl.BlockSpec((d, LANE), lambda i: (0, 0)),
                  pl.BlockSpec((d, LANE), lambda i: (0, 0)),
                  pl.BlockSpec((1, LANE), lambda i: (0, 0))],
        out_specs=(pl.BlockSpec((tm, d // 2), lambda i: (i, 0)),
                   pl.BlockSpec((tm, LANE), lambda i: (i, 0))),
        out_shape=(jax.ShapeDtypeStruct((m, d // 2), U32), jax.ShapeDtypeStruct((m, LANE), F32)),
        compiler_params=_cparams("parallel"),
        name="moe_router",
    )(x, gain.reshape(1, d), w_hi, w_lo, b)


def _routing_plan(meta, n_rows):
    n = meta.shape[0]
    e = meta[:, 2:4].astype(jnp.int32).T
    experts = jnp.arange(N_EXPERTS, dtype=jnp.int32)[:, None]
    onehot = ((e[0][None, :] == experts) | (e[1][None, :] == experts)).astype(jnp.int32)
    counts = onehot.sum(axis=1)
    rank = jnp.cumsum(onehot, axis=1) - onehot
    padded = ((counts + MOE_TILE - 1) // MOE_TILE) * MOE_TILE
    ends = jnp.cumsum(padded)
    offs = ends - padded
    pos = jnp.take_along_axis(offs[:, None] + rank, e, axis=0).astype(jnp.int32)
    tok = jnp.arange(n, dtype=jnp.int32)
    sorted_tok = jnp.zeros((n_rows,), jnp.int32).at[pos.reshape(-1)].set(jnp.concatenate([tok, tok]))
    tile_start = jnp.arange(n_rows // MOE_TILE, dtype=jnp.int32) * MOE_TILE
    tile_expert = jnp.minimum((ends[None, :] <= tile_start[:, None]).sum(axis=1), N_EXPERTS - 1).astype(jnp.int32)
    tile_valid = (tile_start < ends[-1]).astype(jnp.int32)
    return pos[0], pos[1], sorted_tok, tile_expert, tile_valid


def _row_gather_start(idx_ref, base, src_ref, dst_ref, sem):
    for r in range(dst_ref.shape[0]):
        pltpu.make_async_copy(src_ref.at[pl.ds(idx_ref[base + r], 1), :], dst_ref.at[pl.ds(r, 1), :], sem).start()


def _row_gather_wait(src_ref, dst_ref, sem):
    pltpu.make_async_copy(src_ref.at[pl.ds(0, dst_ref.shape[0]), :], dst_ref, sem).wait()


def _moe_up_kernel(st_ref, te_ref, tv_ref, h_ref, wg_ref, wu_ref, o_ref, xbuf, sem):
    del te_ref
    t = pl.program_id(0)
    last = pl.num_programs(0) - 1
    slot = t % 2
    tile = xbuf.shape[1]

    @pl.when(t == 0)
    def _():
        _row_gather_start(st_ref, 0, h_ref, xbuf.at[0], sem.at[0])

    @pl.when(tv_ref[t] == 1)
    def _():
        _row_gather_wait(h_ref, xbuf.at[slot], sem.at[slot])
        nxt = jnp.minimum(t + 1, last)
        _row_gather_start(st_ref, nxt * tile, h_ref, xbuf.at[1 - slot], sem.at[1 - slot])
        lo, hi = _unpack_bf16_pairs(xbuf[slot])
        lo = lo.astype(BF16)
        hi = hi.astype(BF16)
        half = lo.shape[1]
        g = jnp.dot(lo, wg_ref[0, :half], preferred_element_type=F32)
        g = g + jnp.dot(hi, wg_ref[0, half:], preferred_element_type=F32)
        u = jnp.dot(lo, wu_ref[0, :half], preferred_element_type=F32)
        u = u + jnp.dot(hi, wu_ref[0, half:], preferred_element_type=F32)
        o_ref[...] = (g * _sigmoid(g) * u).astype(o_ref.dtype)

        @pl.when(t == last)
        def _():
            _row_gather_wait(h_ref, xbuf.at[1 - slot], sem.at[1 - slot])

    @pl.when(tv_ref[t] == 0)
    def _():
        @pl.when(tv_ref[jnp.maximum(t - 1, 0)] == 1)
        def _():
            _row_gather_wait(h_ref, xbuf.at[slot], sem.at[slot])
        o_ref[...] = jnp.zeros(o_ref.shape, o_ref.dtype)


def moe_up(h, w_gate, w_up, sorted_tok, tile_expert, tile_valid):
    half = h.shape[1]
    d = 2 * half
    f = w_gate.shape[-1]
    n_tiles = tile_expert.shape[0]
    grid_spec = pltpu.PrefetchScalarGridSpec(
        num_scalar_prefetch=3,
        grid=(n_tiles,),
        in_specs=[pl.BlockSpec(memory_space=pl.ANY),
                  pl.BlockSpec((1, d, f), lambda t, st, te, tv: (te[t], 0, 0)),
                  pl.BlockSpec((1, d, f), lambda t, st, te, tv: (te[t], 0, 0))],
        out_specs=pl.BlockSpec((MOE_TILE, f), lambda t, st, te, tv: (t, 0)),
        scratch_shapes=[pltpu.VMEM((2, MOE_TILE, half), U32), pltpu.SemaphoreType.DMA((2,))],
    )
    return pl.pallas_call(
        _moe_up_kernel,
        grid_spec=grid_spec,
        out_shape=jax.ShapeDtypeStruct((n_tiles * MOE_TILE, f), BF16),
        compiler_params=_cparams("arbitrary"),
        name="moe_up",
    )(sorted_tok, tile_expert, tile_valid, h, w_gate, w_up)


def _moe_down_kernel(te_ref, tv_ref, h_ref, wd_ref, o_ref):
    del te_ref
    t = pl.program_id(0)

    @pl.when(tv_ref[t] == 1)
    def _():
        y = jnp.dot(h_ref[...], wd_ref[0], preferred_element_type=F32)
        o_ref[...] = _pack_bf16_pairs(y)

    @pl.when(tv_ref[t] == 0)
    def _():
        o_ref[...] = jnp.zeros(o_ref.shape, o_ref.dtype)


def moe_down(hid, w_down, tile_expert, tile_valid):
    n_rows, f = hid.shape
    d = w_down.shape[-1]
    grid_spec = pltpu.PrefetchScalarGridSpec(
        num_scalar_prefetch=2,
        grid=(n_rows // MOE_TILE,),
        in_specs=[pl.BlockSpec((MOE_TILE, f), lambda t, te, tv: (t, 0)),
                  pl.BlockSpec((1, f, d), lambda t, te, tv: (te[t], 0, 0))],
        out_specs=pl.BlockSpec((MOE_TILE, d // 2), lambda t, te, tv: (t, 0)),
    )
    return pl.pallas_call(
        _moe_down_kernel,
        grid_spec=grid_spec,
        out_shape=jax.ShapeDtypeStruct((n_rows, d // 2), U32),
        compiler_params=_cparams("arbitrary"),
        name="moe_down",
    )(tile_expert, tile_valid, hid, w_down)


def _combine_kernel(p0_ref, p1_ref, x_ref, meta_ref, g_ref, ys_ref, o_ref, buf0, buf1, sem, *, base, final_norm):
    i = pl.program_id(0)
    last = pl.num_programs(0) - 1
    tc = buf0.shape[1]

    def start(tile, buf, sems):
        n0 = base + tile * tc
        _row_gather_start(p0_ref, n0, ys_ref, buf.at[0], sems.at[0])
        _row_gather_start(p1_ref, n0, ys_ref, buf.at[1], sems.at[1])

    def wait(buf, sems):
        _row_gather_wait(ys_ref, buf.at[0], sems.at[0])
        _row_gather_wait(ys_ref, buf.at[1], sems.at[1])

    @pl.when(i == 0)
    def _():
        start(0, buf0, sem.at[0])

    def run(cur, cur_sems, nxt, nxt_sems):
        wait(cur, cur_sems)
        lo0, hi0 = _unpack_bf16_pairs(cur[0])
        lo1, hi1 = _unpack_bf16_pairs(cur[1])
        start(jnp.minimum(i + 1, last), nxt, nxt_sems)
        meta = meta_ref[...]
        w0 = meta[:, 0:1]
        w1 = meta[:, 1:2]
        half = lo0.shape[1]
        y_lo = x_ref[:, :half] + w0 * lo0 + w1 * lo1
        y_hi = x_ref[:, half:] + w0 * hi0 + w1 * hi1
        if final_norm:
            ms = (jnp.sum(y_lo * y_lo, axis=-1, keepdims=True)
                  + jnp.sum(y_hi * y_hi, axis=-1, keepdims=True)) / (2 * half)
            inv = lax.rsqrt(ms + RMS_EPS)
            y_lo = y_lo * inv * g_ref[:, :half]
            y_hi = y_hi * inv * g_ref[:, half:]
        o_ref[:, :half] = y_lo
        o_ref[:, half:] = y_hi

        @pl.when(i == last)
        def _():
            wait(nxt, nxt_sems)

    @pl.when(i % 2 == 0)
    def _():
        run(buf0, sem.at[0], buf1, sem.at[1])

    @pl.when(i % 2 == 1)
    def _():
        run(buf1, sem.at[1], buf0, sem.at[0])


def moe_combine(x, meta, ys, pos0, pos1, gain, *, base, count, final_norm, tc=256):
    d = x.shape[1]
    assert base % tc == 0 and count % tc == 0
    blk0 = base // tc
    grid_spec = pltpu.PrefetchScalarGridSpec(
        num_scalar_prefetch=2,
        grid=(count // tc,),
        in_specs=[pl.BlockSpec((tc, d), lambda i, p0, p1: (blk0 + i, 0)),
                  pl.BlockSpec((tc, LANE), lambda i, p0, p1: (blk0 + i, 0)),
                  pl.BlockSpec((1, d), lambda i, p0, p1: (0, 0)),
                  pl.BlockSpec(memory_space=pl.ANY)],
        out_specs=pl.BlockSpec((tc, d), lambda i, p0, p1: (i, 0)),
        scratch_shapes=[pltpu.VMEM((2, tc, d // 2), U32), pltpu.VMEM((2, tc, d // 2), U32),
                        pltpu.SemaphoreType.DMA((2, 2))],
    )
    return pl.pallas_call(
        functools.partial(_combine_kernel, base=base, final_norm=final_norm),
        grid_spec=grid_spec,
        out_shape=jax.ShapeDtypeStruct((count, d), F32),
        compiler_params=_cparams("arbitrary"),
        name="moe_combine",
    )(pos0, pos1, x, meta, gain.reshape(1, d), ys)


def moe_experts(x, gain, w_rg, b_rg, w_re, b_re, w_gate, w_up, w_down, layer):
    n = x.shape[0]
    n_rows = 2 * n + N_EXPERTS * MOE_TILE
    assert (2 * n) % MOE_TILE == 0
    h, meta = router(x, gain, w_rg, b_rg, w_re, b_re)
    pos0, pos1, sorted_tok, tile_expert, tile_valid = _routing_plan(meta, n_rows)
    tile_expert = tile_expert + layer * N_EXPERTS
    hid = moe_up(h, w_gate, w_up, sorted_tok, tile_expert, tile_valid)
    ys = moe_down(hid, w_down, tile_expert, tile_valid)
    return meta, ys, pos0, pos1


def kernel(x_prompt, x_sample, mem_prompt, mem_sample, norm_mix, norm_mem, norm_ffn, norm_final, na_w_in, na_rpb, s5_w_in, s5_a_re, s5_a_im, s5_log_dt, s5_b_re, s5_b_im, s5_c_re, s5_c_im, s5_d, s5_w_glu, s5_b_glu, mem_w_kv, w_out, moe_w_rg, moe_b_rg, moe_w_re, moe_b_re, moe_w_gate, moe_w_up, moe_w_down):
    d = x_prompt.shape[-1]
    groups = [x_prompt, x_sample]
    mems = [mem_prompt, mem_sample]
    seq_lens = [g.shape[1] for g in groups for _ in range(g.shape[0])]
    n_mem = mem_prompt.shape[1]
    depth = norm_mix.shape[0]
    x = jnp.concatenate([g.reshape(-1, d) for g in groups], axis=0)
    mem = jnp.concatenate([m.reshape(-1, d) for m in mems], axis=0)
    n = x.shape[0]
    tq_mem = 512
    blk_seq = []
    for s, length in enumerate(seq_lens):
        assert length % tq_mem == 0
        blk_seq += [s] * (length // tq_mem)
    blk_seq = jnp.asarray(np.asarray(blk_seq, np.int32))
    na_heads = na_rpb.shape[1]
    w_gate_all = moe_w_gate.astype(BF16).reshape((-1,) + moe_w_gate.shape[2:])
    w_up_all = moe_w_up.astype(BF16).reshape((-1,) + moe_w_up.shape[2:])
    w_down_all = moe_w_down.astype(BF16).reshape((-1,) + moe_w_down.shape[2:])

    meta = ys = pos0 = pos1 = None
    for i in range(depth):
        j = i // 2
        if i > 0:
            x = moe_combine(x, meta, ys, pos0, pos1, norm_final, base=0, count=n, final_norm=False)
        kv = norm_proj(mem, norm_mem[i], mem_w_kv[i].astype(BF16), BF16, tm=n_mem)
        if i % 2 == 0:
            proj = norm_proj(x, norm_mix[i], na_w_in[j].astype(BF16), BF16, tm=512)
            tok = neighbourhood_attention(proj, na_rpb[j], seq_lens)
            q_slab0 = 3 * na_heads
        else:
            proj = norm_proj(x, norm_mix[i], s5_w_in[j].astype(BF16), F32, tm=512)
            lam_re, lam_im, bb_re, bb_im = s5_discretise(s5_a_re[j], s5_a_im[j], s5_log_dt[j], s5_b_re[j], s5_b_im[j])
            b_op, c_op, lam, d_op = _s5_operands(lam_re, lam_im, bb_re, bb_im, s5_c_re[j], s5_c_im[j], s5_d[j])
            g = s5_mixer(proj, b_op, c_op, lam, d_op, seq_lens)
            tok = glu(g, s5_w_glu[j].astype(BF16), s5_b_glu[j])
            q_slab0 = b_op.shape[1]
        mo = memory_attention(proj, q_slab0, kv, blk_seq, n_mem, tq=tq_mem)
        x = out_proj(x, tok, mo, w_out[i].astype(BF16))
        meta, ys, pos0, pos1 = moe_experts(x, norm_ffn[i], moe_w_rg[i], moe_b_rg[i], moe_w_re[i], moe_b_re[i],
                                           w_gate_all, w_up_all, w_down_all, i)
    outs, base = [], 0
    for g in groups:
        count = g.shape[0] * g.shape[1]
        y = moe_combine(x, meta, ys, pos0, pos1, norm_final, base=base, count=count, final_norm=True)
        outs.append(y.reshape(g.shape))
        base += count
    return tuple(outs)
```

```python
import functools
import math

import numpy as np
import jax
import jax.numpy as jnp
from jax import lax
from jax.experimental import pallas as pl
from jax.experimental.pallas import tpu as pltpu

F32 = jnp.float32
BF16 = jnp.bfloat16
U32 = jnp.uint32

LANE = 128
SUBLANE = 8
VMEM_LIMIT_BYTES = 56 * 1024 * 1024

RMS_EPS = 1e-6
NEG_INF = -1e30
GRID_W = 64
ROW_WIN = 8
COL_WIN = 16
NA_HEAD_DIM = 128
MEM_HEADS = 4
S5_GROUP = 16
S5_STATE = 64
N_EXPERT_GROUPS = 4
EXPERTS_PER_GROUP = 4
N_EXPERTS = N_EXPERT_GROUPS * EXPERTS_PER_GROUP

NA_Q_ROWS = 8
NA_K_ROWS = 2 * ROW_WIN
NA_HEADS_PER_STEP = 3
SUPER = 8192
S5_SEGMENTS = SUBLANE
S5_TBLK = 64
MOE_TILE = 256


def _cparams(*sem):
    return pltpu.CompilerParams(dimension_semantics=sem, vmem_limit_bytes=VMEM_LIMIT_BYTES)


def _rms_norm(x, gain):
    ms = jnp.mean(x * x, axis=-1, keepdims=True)
    return x * lax.rsqrt(ms + RMS_EPS) * gain


def _sigmoid(x):
    return 1.0 / (1.0 + jnp.exp(-x))


def _split_hi_lo(x):
    hi = lax.bitcast_convert_type(lax.bitcast_convert_type(x, U32) & jnp.uint32(0xFFFF0000), F32)
    return hi.astype(BF16), (x - hi).astype(BF16)


def _pack_bf16_pairs(x):
    k = x.shape[1] // 2
    bits = lax.bitcast_convert_type(x, U32)
    return (bits[:, k:] & jnp.uint32(0xFFFF0000)) | (bits[:, :k] >> 16)


def _unpack_bf16_pairs(w):
    lo = lax.bitcast_convert_type(w << 16, F32)
    hi = lax.bitcast_convert_type(w & jnp.uint32(0xFFFF0000), F32)
    return lo, hi


def _norm_proj_kernel(x_ref, g_ref, w_ref, o_ref, h_scr):
    @pl.when(pl.program_id(1) == 0)
    def _():
        h_scr[...] = _rms_norm(x_ref[...], g_ref[...]).astype(BF16)

    acc = jnp.dot(h_scr[...], w_ref[...], preferred_element_type=F32)
    for c in range(o_ref.shape[0]):
        o_ref[c] = acc[:, c * LANE:(c + 1) * LANE].astype(o_ref.dtype)


def norm_proj(x, gain, w, out_dtype, tm, tn=1024):
    m, d = x.shape
    n_out = w.shape[1]
    assert m % tm == 0 and n_out % tn == 0 and tn % LANE == 0
    return pl.pallas_call(
        _norm_proj_kernel,
        grid=(m // tm, n_out // tn),
        in_specs=[pl.BlockSpec((tm, d), lambda i, j: (i, 0)),
                  pl.BlockSpec((1, d), lambda i, j: (0, 0)),
                  pl.BlockSpec((d, tn), lambda i, j: (0, j))],
        out_specs=pl.BlockSpec((tn // LANE, tm, LANE), lambda i, j: (j, i, 0)),
        out_shape=jax.ShapeDtypeStruct((n_out // LANE, m, LANE), out_dtype),
        scratch_shapes=[pltpu.VMEM((tm, d), BF16)],
        compiler_params=_cparams("parallel", "arbitrary"),
        name="norm_proj",
    )(x, gain.reshape(1, d), w)


def _na_kernel(ks_ref, ty_ref, pi_ref, q_ref, k_ref, v_ref, b_ref, o_ref, *, nqb):
    step = pl.program_id(1) * nqb + pl.program_id(2)
    start = pl.multiple_of(ks_ref[step] * GRID_W, GRID_W)
    nk = NA_K_ROWS * GRID_W
    npair = NA_K_ROWS // 2
    ty = ty_ref[step]
    for h in range(q_ref.shape[0]):
        q = q_ref[h]
        k = k_ref[h, pl.ds(start, nk), :]
        v = v_ref[h, pl.ds(start, nk), :]
        s = lax.dot_general(q, k, (((1,), (1,)), ((), ())), preferred_element_type=F32)
        bias = jnp.concatenate(
            [jnp.concatenate([b_ref[h, pi_ref[(ty * NA_Q_ROWS + i) * npair + jp]] for jp in range(npair)], axis=1)
             for i in range(NA_Q_ROWS)], axis=0)
        s = s + bias
        m = jnp.max(s, axis=-1, keepdims=True)
        p = jnp.exp(s - m).astype(BF16)
        v_ones = jnp.concatenate([v, jnp.ones(v.shape, v.dtype)], axis=1)
        ol = jnp.dot(p, v_ones, preferred_element_type=F32)
        o = ol[:, :LANE] / ol[:, LANE:]
        o_ref[:, h * LANE:(h + 1) * LANE] = o.astype(o_ref.dtype)


def _na_tables(seq_lens):
    ks, ty = [], []
    tok = 0
    per_super = SUPER // GRID_W
    for length in seq_lens:
        rows = length // GRID_W
        assert length % (NA_Q_ROWS * GRID_W) == 0 and rows >= 3 * NA_Q_ROWS
        assert tok // SUPER == (tok + length - 1) // SUPER
        row0 = (tok % SUPER) // GRID_W
        for qb in range(rows // NA_Q_ROWS):
            ks.append(row0 + int(np.clip(qb * NA_Q_ROWS - ROW_WIN // 2, 0, rows - NA_K_ROWS)))
            ty.append(0 if qb == 0 else (2 if qb == rows // NA_Q_ROWS - 1 else 1))
        tok += length
    assert tok % SUPER == 0 and len(ks) == (tok // SUPER) * (per_super // NA_Q_ROWS)
    return np.asarray(ks, np.int32), np.asarray(ty, np.int32)


def _na_bias(rpb):
    heads = rpb.shape[0]
    cols = np.arange(GRID_W)
    c_start = np.clip(cols - COL_WIN // 2, 0, GRID_W - COL_WIN)
    in_win = (cols[None, :] >= c_start[:, None]) & (cols[None, :] < c_start[:, None] + COL_WIN)
    dc = np.clip(cols[None, :] - cols[:, None], -(COL_WIN - 1), COL_WIN - 1) + COL_WIN - 1
    col_bias = jnp.where(jnp.asarray(in_win), rpb.astype(F32)[:, :, dc], NEG_INF)
    n_dr = 2 * ROW_WIN - 1
    masked = jnp.full((heads, n_dr, GRID_W, GRID_W), NEG_INF, F32)
    left = jnp.concatenate([col_bias[:, :n_dr - 1], masked, col_bias, masked[:, :1]], axis=1)
    right = jnp.concatenate([col_bias[:, 1:], col_bias, masked, masked[:, :1]], axis=1)
    pairs = jnp.concatenate([left, right], axis=-1)
    both, right_only, left_only, none = 0, n_dr - 1, 2 * n_dr - 1, 3 * n_dr - 1
    half = ROW_WIN // 2
    index = []
    for ty in range(3):
        for i in range(NA_Q_ROWS):
            q_rel = (i, i + half, i + ROW_WIN)[ty]
            w_rel = (max(i - half, 0), i, min(i + half, ROW_WIN))[ty]
            for jp in range(NA_K_ROWS // 2):
                j0, j1 = 2 * jp, 2 * jp + 1
                v0, v1 = w_rel <= j0 < w_rel + ROW_WIN, w_rel <= j1 < w_rel + ROW_WIN
                d0, d1 = j0 - q_rel + ROW_WIN - 1, j1 - q_rel + ROW_WIN - 1
                assert (not v0 or 0 <= d0 < n_dr) and (not v1 or 0 <= d1 < n_dr)
                index.append(both + d0 if v0 and v1 else right_only + d1 if v1 else left_only + d0 if v0 else none)
    return pairs, np.asarray(index, np.int32)


def neighbourhood_attention(qkv, rpb, seq_lens):
    heads = rpb.shape[0]
    n = qkv.shape[1]
    nsup = n // SUPER
    tq = NA_Q_ROWS * GRID_W
    nqb = SUPER // tq
    ks_tab, ty_tab = _na_tables(seq_lens)
    pairs, pair_index = _na_bias(rpb)
    hp = NA_HEADS_PER_STEP
    assert heads % hp == 0
    ng = heads // hp
    grid_spec = pltpu.PrefetchScalarGridSpec(
        num_scalar_prefetch=3,
        grid=(ng, nsup, nqb),
        in_specs=[
            pl.BlockSpec((hp, tq, LANE), lambda h, s, b, ks, ty, pi: (h, s * nqb + b, 0)),
            pl.BlockSpec((hp, SUPER, LANE), lambda h, s, b, ks, ty, pi: (ng + h, s, 0)),
            pl.BlockSpec((hp, SUPER, LANE), lambda h, s, b, ks, ty, pi: (2 * ng + h, s, 0)),
            pl.BlockSpec((hp,) + pairs.shape[1:], lambda h, s, b, ks, ty, pi: (h, 0, 0, 0)),
        ],
        out_specs=pl.BlockSpec((tq, hp * LANE), lambda h, s, b, ks, ty, pi: (s * nqb + b, h)),
    )
    return pl.pallas_call(
        functools.partial(_na_kernel, nqb=nqb),
        grid_spec=grid_spec,
        out_shape=jax.ShapeDtypeStruct((n, heads * LANE), BF16),
        compiler_params=_cparams("parallel", "arbitrary", "arbitrary"),
        name="neighbourhood_attention",
    )(jnp.asarray(ks_tab), jnp.asarray(ty_tab), jnp.asarray(pair_index), qkv, qkv, qkv, pairs)


def _mem_attn_kernel(seq_ref, q_ref, kv_ref, o_ref, *, scale):
    del seq_ref
    hd = 2 * LANE
    for h in range(MEM_HEADS):
        q = jnp.concatenate([q_ref[2 * h], q_ref[2 * h + 1]], axis=-1).astype(BF16)
        k = jnp.concatenate([kv_ref[2 * h], kv_ref[2 * h + 1]], axis=-1)
        v = jnp.concatenate([kv_ref[2 * MEM_HEADS + 2 * h], kv_ref[2 * MEM_HEADS + 2 * h + 1]], axis=-1)
        s = lax.dot_general(q, k, (((1,), (1,)), ((), ())), preferred_element_type=F32) * scale
        m = jnp.max(s, axis=-1, keepdims=True)
        p = jnp.exp(s - m)
        l = jnp.sum(p, axis=-1, keepdims=True)
        o = jnp.dot(p.astype(BF16), v, preferred_element_type=F32) / l
        o_ref[:, h * hd:(h + 1) * hd] = o.astype(o_ref.dtype)


def memory_attention(proj, q_slab0, kv, blk_seq, n_mem, tq=512):
    n = proj.shape[1]
    nq = 2 * MEM_HEADS
    assert q_slab0 % nq == 0
    grid_spec = pltpu.PrefetchScalarGridSpec(
        num_scalar_prefetch=1,
        grid=(n // tq,),
        in_specs=[pl.BlockSpec((nq, tq, LANE), lambda i, sq: (q_slab0 // nq, i, 0)),
                  pl.BlockSpec((2 * nq, n_mem, LANE), lambda i, sq: (0, sq[i], 0))],
        out_specs=pl.BlockSpec((tq, nq * LANE), lambda i, sq: (i, 0)),
    )
    return pl.pallas_call(
        functools.partial(_mem_attn_kernel, scale=(2 * LANE) ** -0.5),
        grid_spec=grid_spec,
        out_shape=jax.ShapeDtypeStruct((n, nq * LANE), BF16),
        compiler_params=_cparams("parallel"),
        name="memory_attention",
    )(blk_seq, proj, kv)


def _s5_param_kernel(are_ref, aim_ref, ldt_ref, bre_ref, bim_ref, lre_ref, lim_ref, bbre_ref, bbim_ref):
    a_re = are_ref[...]
    a_im = aim_ref[...]
    dt = jnp.exp(ldt_ref[...])
    mag = jnp.exp(a_re * dt)
    ang = a_im * dt
    lam_re = mag * jnp.cos(ang)
    lam_im = mag * jnp.sin(ang)
    den = a_re * a_re + a_im * a_im
    n_re = lam_re - 1.0
    coef_re = (n_re * a_re + lam_im * a_im) / den
    coef_im = (lam_im * a_re - n_re * a_im) / den
    lre_ref[...] = lam_re
    lim_ref[...] = lam_im
    for c in range(bre_ref.shape[0]):
        b_re = bre_ref[c]
        b_im = bim_ref[c]
        bbre_ref[c] = coef_re * b_re - coef_im * b_im
        bbim_ref[c] = coef_re * b_im + coef_im * b_re


def s5_discretise(a_re, a_im, log_dt, b_re, b_im):
    two, g, p = a_re.shape
    c = b_re.shape[-1]
    rows = two * g
    ldt = jnp.broadcast_to(log_dt.reshape(rows, 1), (rows, p))
    b_re_t = b_re.reshape(rows, p, c).transpose(2, 0, 1)
    b_im_t = b_im.reshape(rows, p, c).transpose(2, 0, 1)
    return pl.pallas_call(
        _s5_param_kernel,
        out_shape=(jax.ShapeDtypeStruct((rows, p), F32), jax.ShapeDtypeStruct((rows, p), F32),
                   jax.ShapeDtypeStruct((c, rows, p), F32), jax.ShapeDtypeStruct((c, rows, p), F32)),
        name="s5_discretise",
    )(a_re.reshape(rows, p), a_im.reshape(rows, p), ldt, b_re_t, b_im_t)


def _s5_operands(lam_re, lam_im, bb_re, bb_im, c_re, c_im, d):
    c, rows, p = bb_re.shape
    g = rows // 2
    gs = LANE // c
    ns = g // gs
    same_group = jnp.eye(gs, dtype=jnp.bool_)

    def bmat(bb):
        x = bb.reshape(c, 2, ns, gs, p).transpose(1, 2, 3, 0, 4)
        blocks = jnp.where(same_group[None, None, :, None, :, None], x[:, :, :, :, None, :], 0.0)
        return blocks.reshape(2, ns, gs * c, gs * p)

    def cmat(cc):
        x = cc.astype(F32).reshape(2, ns, gs, c, p).transpose(0, 1, 2, 4, 3)
        blocks = jnp.where(same_group[None, None, :, None, :, None], x[:, :, :, :, None, :], 0.0)
        return blocks.reshape(2, ns, gs * p, gs * c)

    b_op = jnp.concatenate([bmat(bb_re), bmat(bb_im)], axis=-1).astype(BF16)
    c_op = jnp.concatenate([cmat(c_re), -cmat(c_im)], axis=-2).astype(BF16)
    lam = jnp.concatenate([lam_re.reshape(2, ns, gs * p), lam_im.reshape(2, ns, gs * p)], axis=-1)
    lam = jnp.broadcast_to(lam[:, :, None, :], (2, ns, SUBLANE, 2 * gs * p))
    d_op = d.astype(F32).reshape(ns, 1, gs * c)
    return b_op, c_op, lam, d_op


def _s5_kernel(keep_ref, u_hbm, bm_ref, cm_ref, lam_ref, d_ref, o_ref,
               u_scr, u_sem, up_scr, y_scr, y2_scr, bu0_scr, bu1_scr, xs0_scr, xs1_scr, e_scr, i_scr, *, lseg, tblk):
    sup = pl.program_id(0)
    nblk = lseg // tblk
    rb = SUBLANE * tblk
    ns = bu0_scr.shape[1] // 2
    nchunk = ns // LANE
    d_row = d_ref[0]
    pitch = u_scr.shape[1] // S5_SEGMENTS
    n_slab = pl.num_programs(1)
    lin = sup * n_slab + pl.program_id(1)
    slot = lin % 2

    def segment_copies(step, slot):
        b = step // n_slab
        s = step % n_slab
        return [pltpu.make_async_copy(u_hbm.at[s, pl.ds(b * (S5_SEGMENTS * lseg) + r * lseg, lseg), :],
                                      u_scr.at[slot, pl.ds(r * pitch, lseg), :], u_sem.at[slot])
                for r in range(S5_SEGMENTS)]

    @pl.when(lin == 0)
    def _():
        for cp in segment_copies(lin, slot):
            cp.start()

    for cp in segment_copies(lin, slot):
        cp.wait()

    @pl.when(lin + 1 < pl.num_programs(0) * n_slab)
    def _():
        for cp in segment_copies(lin + 1, 1 - slot):
            cp.start()

    u2 = u_scr.at[slot]

    def permute_in(t2, carry):
        t = t2 * 2
        a = u2[pl.ds(t, SUBLANE, stride=pitch), :]
        b = u2[pl.ds(t + 1, SUBLANE, stride=pitch), :]
        ab = jnp.concatenate([a, b], axis=0)
        rows = pl.ds(pl.multiple_of(t * SUBLANE, 2 * SUBLANE), 2 * SUBLANE)
        up_scr[rows, :] = ab.astype(BF16)
        y_scr[rows, :] = ab * d_row
        return carry
    lax.fori_loop(0, lseg // 2, permute_in, 0, unroll=4)
    xs1_scr[...] = jnp.zeros(xs1_scr.shape, xs1_scr.dtype)

    for direction in range(2):
        rev = direction == 1
        lam_r = [lam_ref[direction, 0, :, c * LANE:(c + 1) * LANE] for c in range(nchunk)]
        lam_i = [lam_ref[direction, 0, :, ns + c * LANE:ns + (c + 1) * LANE] for c in range(nchunk)]
        steps = list(range(tblk))[::-1] if rev else list(range(tblk))

        def block_rows(k, rev=rev):
            k = jnp.clip(k, 0, nblk - 1)
            blk = (nblk - 1 - k) if rev else k
            return pl.ds(pl.multiple_of(blk * rb, rb), rb)

        def bu_dot(k, direction=direction, block_rows=block_rows):
            return jnp.dot(up_scr[block_rows(k), :], bm_ref[direction, 0], preferred_element_type=F32)

        def scan_block(bu, x, xs, lam_r=lam_r, lam_i=lam_i, steps=steps):
            xr, xi = list(x[:nchunk]), list(x[nchunk:])
            held = None
            for n, t in enumerate(steps):
                rows = slice(t * SUBLANE, (t + 1) * SUBLANE)
                for c in range(nchunk):
                    b_r = bu[rows, c * LANE:(c + 1) * LANE]
                    b_i = bu[rows, ns + c * LANE:ns + (c + 1) * LANE]
                    r = lam_r[c] * xr[c] - lam_i[c] * xi[c] + b_r
                    i = lam_r[c] * xi[c] + lam_i[c] * xr[c] + b_i
                    xr[c], xi[c] = r, i
                if xs is None:
                    continue
                if n % 2 == 0:
                    held = (t, list(xr), list(xi))
                    continue
                lo, hi = (held, (t, xr, xi)) if held[0] < t else ((t, xr, xi), held)
                pair = slice(lo[0] * SUBLANE, (lo[0] + 2) * SUBLANE)
                for c in range(nchunk):
                    xs[pair, c * LANE:(c + 1) * LANE] = jnp.concatenate([lo[1][c], hi[1][c]], axis=0).astype(BF16)
                    xs[pair, ns + c * LANE:ns + (c + 1) * LANE] = (
                        jnp.concatenate([lo[2][c], hi[2][c]], axis=0).astype(BF16))
            return tuple(xr + xi)

        def pass_a(k2, x, bu_dot=bu_dot, scan_block=scan_block):
            bu1_scr[...] = bu_dot(2 * k2 + 1)
            x = scan_block(bu0_scr, x, None)
            bu0_scr[...] = bu_dot(2 * k2 + 2)
            return scan_block(bu1_scr, x, None)
        bu0_scr[...] = bu_dot(0)
        zero = tuple(jnp.zeros((SUBLANE, LANE), F32) for _ in range(2 * nchunk))
        ends = lax.fori_loop(0, nblk // 2, pass_a, zero)
        for c in range(2 * nchunk):
            e_scr[:, c * LANE:(c + 1) * LANE] = ends[c]

        p_r = lam_ref[direction, 0, 0:1, 0:ns]
        p_i = lam_ref[direction, 0, 0:1, ns:2 * ns]
        for _ in range(int(math.log2(lseg))):
            p_r, p_i = p_r * p_r - p_i * p_i, 2.0 * p_r * p_i
        order = list(range(S5_SEGMENTS))[::-1] if rev else list(range(S5_SEGMENTS))
        s_r = jnp.zeros((1, ns), F32)
        s_i = jnp.zeros((1, ns), F32)
        i_scr[order[0]:order[0] + 1, :] = jnp.zeros((1, 2 * ns), F32)
        for prev, cur in zip(order[:-1], order[1:]):
            e_r = e_scr[prev:prev + 1, 0:ns]
            e_i = e_scr[prev:prev + 1, ns:2 * ns]
            keep = keep_ref[(sup * 2 + direction) * S5_SEGMENTS + cur].astype(F32)
            n_r = (p_r * s_r - p_i * s_i + e_r) * keep
            n_i = (p_r * s_i + p_i * s_r + e_i) * keep
            i_scr[cur:cur + 1, 0:ns] = n_r
            i_scr[cur:cur + 1, ns:2 * ns] = n_i
            s_r, s_i = n_r, n_i

        def project(k, xs, scale, direction=direction, block_rows=block_rows):
            rows = block_rows(k)
            yb = jnp.dot(xs[...], cm_ref[direction, 0], preferred_element_type=F32)
            y_scr[rows, :] = y_scr[rows, :] + yb * scale

        def pass_b(k2, x, bu_dot=bu_dot, scan_block=scan_block, project=project):
            bu1_scr[...] = bu_dot(2 * k2 + 1)
            x = scan_block(bu0_scr, x, xs0_scr)
            project(2 * k2 - 1, xs1_scr, jnp.where(k2 > 0, 1.0, 0.0))
            bu0_scr[...] = bu_dot(2 * k2 + 2)
            x = scan_block(bu1_scr, x, xs1_scr)
            project(2 * k2, xs0_scr, 1.0)
            return x
        bu0_scr[...] = bu_dot(0)
        start = tuple(i_scr[:, c * LANE:(c + 1) * LANE] for c in range(2 * nchunk))
        lax.fori_loop(0, nblk // 2, pass_b, start)
        project(nblk - 1, xs1_scr, 1.0)

    def permute_out(t, carry):
        y = y_scr[pl.ds(pl.multiple_of(t * SUBLANE, SUBLANE), SUBLANE), :]
        gelu = 0.5 * y * (1.0 + jnp.tanh(math.sqrt(2.0 / math.pi) * (y + 0.044715 * (y * y * y))))
        y2_scr[pl.ds(t, SUBLANE, stride=pitch), :] = gelu
        return carry
    lax.fori_loop(0, lseg, permute_out, 0, unroll=8)
    for r in range(S5_SEGMENTS):
        o_ref[r * lseg:(r + 1) * lseg, :] = y2_scr[r * pitch:r * pitch + lseg, :].astype(o_ref.dtype)


def _s5_keep_table(seq_lens, lseg):
    starts, ends = set(), set()
    tok = 0
    for length in seq_lens:
        assert length % lseg == 0
        starts.add(tok // lseg)
        tok += length
        ends.add(tok // lseg - 1)
    nseg = tok // lseg
    assert nseg % S5_SEGMENTS == 0
    keep = np.ones((nseg // S5_SEGMENTS, 2, S5_SEGMENTS), np.int32)
    for s in range(nseg):
        if s in starts:
            keep[s // S5_SEGMENTS, 0, s % S5_SEGMENTS] = 0
        if s in ends:
            keep[s // S5_SEGMENTS, 1, s % S5_SEGMENTS] = 0
    return keep.reshape(-1)


def s5_mixer(proj, b_op, c_op, lam, d_op, seq_lens):
    n = proj.shape[1]
    ns = b_op.shape[1]
    nstate2 = b_op.shape[-1]
    nsup = n // SUPER
    lseg = SUPER // S5_SEGMENTS
    tblk = min(S5_TBLK, lseg)
    assert lseg & (lseg - 1) == 0 and lseg % (2 * tblk) == 0 and tblk % 2 == 0
    keep = _s5_keep_table(seq_lens, lseg)
    rows = SUBLANE * tblk
    pitch = lseg + SUBLANE
    grid_spec = pltpu.PrefetchScalarGridSpec(
        num_scalar_prefetch=1,
        grid=(nsup, ns),
        in_specs=[
            pl.BlockSpec(memory_space=pl.ANY),
            pl.BlockSpec((2, 1, LANE, nstate2), lambda b, s, kp: (0, s, 0, 0)),
            pl.BlockSpec((2, 1, nstate2, LANE), lambda b, s, kp: (0, s, 0, 0)),
            pl.BlockSpec((2, 1, SUBLANE, nstate2), lambda b, s, kp: (0, s, 0, 0)),
            pl.BlockSpec((1, 1, LANE), lambda b, s, kp: (s, 0, 0)),
        ],
        out_specs=pl.BlockSpec((SUPER, LANE), lambda b, s, kp: (b, s)),
        scratch_shapes=[
            pltpu.VMEM((2, S5_SEGMENTS * pitch, LANE), F32),
            pltpu.SemaphoreType.DMA((2,)),
            pltpu.VMEM((SUPER, LANE), BF16),
            pltpu.VMEM((SUPER, LANE), F32),
            pltpu.VMEM((S5_SEGMENTS * pitch, LANE), F32),
            pltpu.VMEM((rows, nstate2), F32),
            pltpu.VMEM((rows, nstate2), F32),
            pltpu.VMEM((rows, nstate2), BF16),
            pltpu.VMEM((rows, nstate2), BF16),
            pltpu.VMEM((SUBLANE, nstate2), F32),
            pltpu.VMEM((SUBLANE, nstate2), F32),
        ],
    )
    return pl.pallas_call(
        functools.partial(_s5_kernel, lseg=lseg, tblk=tblk),
        grid_spec=grid_spec,
        out_shape=jax.ShapeDtypeStruct((n, ns * LANE), BF16),
        compiler_params=_cparams("arbitrary", "arbitrary"),
        name="s5_mixer",
    )(jnp.asarray(keep), proj, b_op, c_op, lam, d_op)


def _glu_kernel(g_ref, w_ref, b_ref, o_ref):
    tn = o_ref.shape[1]
    col = pl.multiple_of(pl.program_id(1) * tn, tn)
    acc = jnp.dot(g_ref[...], w_ref[...], preferred_element_type=F32) + b_ref[...]
    g = g_ref[:, pl.ds(col, tn)].astype(F32)
    o_ref[...] = (g * _sigmoid(acc)).astype(o_ref.dtype)


def glu(g, w, b, tm=1024, tn=1024):
    m, k = g.shape
    assert m % tm == 0 and k % tn == 0
    return pl.pallas_call(
        _glu_kernel,
        grid=(m // tm, k // tn),
        in_specs=[pl.BlockSpec((tm, k), lambda i, j: (i, 0)),
                  pl.BlockSpec((k, tn), lambda i, j: (0, j)),
                  pl.BlockSpec((1, tn), lambda i, j: (0, j))],
        out_specs=pl.BlockSpec((tm, tn), lambda i, j: (i, j)),
        out_shape=jax.ShapeDtypeStruct((m, k), BF16),
        compiler_params=_cparams("parallel", "arbitrary"),
        name="glu",
    )(g, w, b.reshape(1, k).astype(F32))


def _out_proj_kernel(x_ref, tok_ref, mo_ref, w1_ref, w2_ref, o_ref):
    acc = jnp.dot(tok_ref[...], w1_ref[...], preferred_element_type=F32)
    acc = acc + jnp.dot(mo_ref[...], w2_ref[...], preferred_element_type=F32)
    o_ref[...] = x_ref[...] + acc


def out_proj(x, tok, mo, w, tm=1024, tn=512):
    m, d = x.shape
    k1, k2 = tok.shape[1], mo.shape[1]
    assert m % tm == 0 and d % tn == 0 and w.shape[0] == k1 + k2 and k1 % k2 == 0
    return pl.pallas_call(
        _out_proj_kernel,
        grid=(m // tm, d // tn),
        in_specs=[pl.BlockSpec((tm, tn), lambda i, j: (i, j)),
                  pl.BlockSpec((tm, k1), lambda i, j: (i, 0)),
                  pl.BlockSpec((tm, k2), lambda i, j: (i, 0)),
                  pl.BlockSpec((k1, tn), lambda i, j: (0, j)),
                  pl.BlockSpec((k2, tn), lambda i, j: (k1 // k2, j))],
        out_specs=pl.BlockSpec((tm, tn), lambda i, j: (i, j)),
        out_shape=jax.ShapeDtypeStruct((m, d), F32),
        compiler_params=_cparams("parallel", "arbitrary"),
        name="out_proj",
    )(x, tok, mo, w, w)


def _router_kernel(x_ref, g_ref, whi_ref, wlo_ref, b_ref, h_ref, meta_ref):
    t = _rms_norm(x_ref[...], g_ref[...])
    h_ref[...] = _pack_bf16_pairs(t)
    t_hi, t_lo = _split_hi_lo(t)
    logits = (jnp.dot(t_hi, whi_ref[...], preferred_element_type=F32)
              + jnp.dot(t_lo, whi_ref[...], preferred_element_type=F32)
              + jnp.dot(t_hi, wlo_ref[...], preferred_element_type=F32)) + b_ref[...]
    lane = lax.broadcasted_iota(jnp.int32, logits.shape, 1)
    big = jnp.int32(LANE)
    is_g = lane < N_EXPERT_GROUPS
    gl = jnp.where(is_g, logits, -jnp.inf)
    gmax = jnp.max(gl, axis=-1, keepdims=True)
    gsum = jnp.sum(jnp.where(is_g, jnp.exp(gl - gmax), 0.0), axis=-1, keepdims=True)
    g_val = 1.0 / gsum
    g_idx = jnp.min(jnp.where(gl == gmax, lane, big), axis=-1, keepdims=True)
    e_lane = lane - N_EXPERT_GROUPS
    in_grp = (e_lane >= 0) & (e_lane < N_EXPERTS) & ((e_lane >> 2) == g_idx)
    el = jnp.where(in_grp, logits, -jnp.inf)
    v1 = jnp.max(el, axis=-1, keepdims=True)
    i1 = jnp.min(jnp.where(el == v1, lane, big), axis=-1, keepdims=True)
    el2 = jnp.where(lane == i1, -jnp.inf, el)
    v2 = jnp.max(el2, axis=-1, keepdims=True)
    i2 = jnp.min(jnp.where(el2 == v2, lane, big), axis=-1, keepdims=True)
    z = jnp.exp(v2 - v1)
    w1 = g_val / (1.0 + z)
    w2 = g_val * z / (1.0 + z)
    e1 = (i1 - N_EXPERT_GROUPS).astype(F32)
    e2 = (i2 - N_EXPERT_GROUPS).astype(F32)
    meta = jnp.where(lane == 0, w1, jnp.where(lane == 1, w2, jnp.where(lane == 2, e1, jnp.where(lane == 3, e2, 0.0))))
    meta_ref[...] = meta


def router(x, gain, w_rg, b_rg, w_re, b_re, tm=512):
    m, d = x.shape
    assert EXPERTS_PER_GROUP == 4 and m % tm == 0
    pad = LANE - N_EXPERT_GROUPS - N_EXPERTS
    w = jnp.concatenate([w_rg, w_re, jnp.zeros((d, pad), F32)], axis=1).astype(F32)
    w_hi, w_lo = _split_hi_lo(w)
    b = jnp.concatenate([b_rg, b_re, jnp.zeros((pad,), F32)]).astype(F32).reshape(1, LANE)
    return pl.pallas_call(
        _router_kernel,
        grid=(m // tm,),
        in_specs=[pl.BlockSpec((tm, d), lambda i: (i, 0)),
                  pl.BlockSpec((1, d), lambda i: (0, 0)),
                  pl.BlockSpec((d, LANE), lambda i: (0, 0)),
                  pl.BlockSpec((d, LANE), lambda i: (0, 0)),
                  pl.BlockSpec((1, LANE), lambda i: (0, 0))],
        out_specs=(pl.BlockSpec((tm, d // 2), lambda i: (i, 0)),
                   pl.BlockSpec((tm, LANE), lambda i: (i, 0))),
        out_shape=(jax.ShapeDtypeStruct((m, d // 2), U32), jax.ShapeDtypeStruct((m, LANE), F32)),
        compiler_params=_cparams("parallel"),
        name="moe_router",
    )(x, gain.reshape(1, d), w_hi, w_lo, b)


def _routing_plan(meta, n_rows):
    n = meta.shape[0]
    e = meta[:, 2:4].astype(jnp.int32).T
    experts = jnp.arange(N_EXPERTS, dtype=jnp.int32)[:, None]
    onehot = ((e[0][None, :] == experts) | (e[1][None, :] == experts)).astype(jnp.int32)
    counts = onehot.sum(axis=1)
    rank = jnp.cumsum(onehot, axis=1) - onehot
    padded = ((counts + MOE_TILE - 1) // MOE_TILE) * MOE_TILE
    ends = jnp.cumsum(padded)
    offs = ends - padded
    pos = jnp.take_along_axis(offs[:, None] + rank, e, axis=0).astype(jnp.int32)
    tok = jnp.arange(n, dtype=jnp.int32)
    sorted_tok = jnp.zeros((n_rows,), jnp.int32).at[pos.reshape(-1)].set(jnp.concatenate([tok, tok]))
    tile_start = jnp.arange(n_rows // MOE_TILE, dtype=jnp.int32) * MOE_TILE
    tile_expert = jnp.minimum((ends[None, :] <= tile_start[:, None]).sum(axis=1), N_EXPERTS - 1).astype(jnp.int32)
    tile_valid = (tile_start < ends[-1]).astype(jnp.int32)
    return pos[0], pos[1], sorted_tok, tile_expert, tile_valid


def _row_gather_start(idx_ref, base, src_ref, dst_ref, sem):
    for r in range(dst_ref.shape[0]):
        pltpu.make_async_copy(src_ref.at[pl.ds(idx_ref[base + r], 1), :], dst_ref.at[pl.ds(r, 1), :], sem).start()


def _row_gather_wait(src_ref, dst_ref, sem):
    pltpu.make_async_copy(src_ref.at[pl.ds(0, dst_ref.shape[0]), :], dst_ref, sem).wait()


def _moe_up_kernel(st_ref, te_ref, tv_ref, h_ref, wg_ref, wu_ref, o_ref, xbuf, sem):
    del te_ref
    t = pl.program_id(0)
    last = pl.num_programs(0) - 1
    slot = t % 2
    tile = xbuf.shape[1]

    @pl.when(t == 0)
    def _():
        _row_gather_start(st_ref, 0, h_ref, xbuf.at[0], sem.at[0])

    @pl.when(tv_ref[t] == 1)
    def _():
        _row_gather_wait(h_ref, xbuf.at[slot], sem.at[slot])
        nxt = jnp.minimum(t + 1, last)
        _row_gather_start(st_ref, nxt * tile, h_ref, xbuf.at[1 - slot], sem.at[1 - slot])
        lo, hi = _unpack_bf16_pairs(xbuf[slot])
        lo = lo.astype(BF16)
        hi = hi.astype(BF16)
        half = lo.shape[1]
        g = jnp.dot(lo, wg_ref[0, :half], preferred_element_type=F32)
        g = g + jnp.dot(hi, wg_ref[0, half:], preferred_element_type=F32)
        u = jnp.dot(lo, wu_ref[0, :half], preferred_element_type=F32)
        u = u + jnp.dot(hi, wu_ref[0, half:], preferred_element_type=F32)
        o_ref[...] = (g * _sigmoid(g) * u).astype(o_ref.dtype)

        @pl.when(t == last)
        def _():
            _row_gather_wait(h_ref, xbuf.at[1 - slot], sem.at[1 - slot])

    @pl.when(tv_ref[t] == 0)
    def _():
        @pl.when(tv_ref[jnp.maximum(t - 1, 0)] == 1)
        def _():
            _row_gather_wait(h_ref, xbuf.at[slot], sem.at[slot])
        o_ref[...] = jnp.zeros(o_ref.shape, o_ref.dtype)


def moe_up(h, w_gate, w_up, sorted_tok, tile_expert, tile_valid):
    half = h.shape[1]
    d = 2 * half
    f = w_gate.shape[-1]
    n_tiles = tile_expert.shape[0]
    grid_spec = pltpu.PrefetchScalarGridSpec(
        num_scalar_prefetch=3,
        grid=(n_tiles,),
        in_specs=[pl.BlockSpec(memory_space=pl.ANY),
                  pl.BlockSpec((1, d, f), lambda t, st, te, tv: (te[t], 0, 0)),
                  pl.BlockSpec((1, d, f), lambda t, st, te, tv: (te[t], 0, 0))],
        out_specs=pl.BlockSpec((MOE_TILE, f), lambda t, st, te, tv: (t, 0)),
        scratch_shapes=[pltpu.VMEM((2, MOE_TILE, half), U32), pltpu.SemaphoreType.DMA((2,))],
    )
    return pl.pallas_call(
        _moe_up_kernel,
        grid_spec=grid_spec,
        out_shape=jax.ShapeDtypeStruct((n_tiles * MOE_TILE, f), BF16),
        compiler_params=_cparams("arbitrary"),
        name="moe_up",
    )(sorted_tok, tile_expert, tile_valid, h, w_gate, w_up)


def _moe_down_kernel(te_ref, tv_ref, h_ref, wd_ref, o_ref):
    del te_ref
    t = pl.program_id(0)

    @pl.when(tv_ref[t] == 1)
    def _():
        y = jnp.dot(h_ref[...], wd_ref[0], preferred_element_type=F32)
        o_ref[...] = _pack_bf16_pairs(y)

    @pl.when(tv_ref[t] == 0)
    def _():
        o_ref[...] = jnp.zeros(o_ref.shape, o_ref.dtype)


def moe_down(hid, w_down, tile_expert, tile_valid):
    n_rows, f = hid.shape
    d = w_down.shape[-1]
    grid_spec = pltpu.PrefetchScalarGridSpec(
        num_scalar_prefetch=2,
        grid=(n_rows // MOE_TILE,),
        in_specs=[pl.BlockSpec((MOE_TILE, f), lambda t, te, tv: (t, 0)),
                  pl.BlockSpec((1, f, d), lambda t, te, tv: (te[t], 0, 0))],
        out_specs=pl.BlockSpec((MOE_TILE, d // 2), lambda t, te, tv: (t, 0)),
    )
    return pl.pallas_call(
        _moe_down_kernel,
        grid_spec=grid_spec,
        out_shape=jax.ShapeDtypeStruct((n_rows, d // 2), U32),
        compiler_params=_cparams("arbitrary"),
        name="moe_down",
    )(tile_expert, tile_valid, hid, w_down)


def _combine_kernel(p0_ref, p1_ref, x_ref, meta_ref, g_ref, ys_ref, o_ref, buf0, buf1, sem, *, base, final_norm):
    i = pl.program_id(0)
    last = pl.num_programs(0) - 1
    tc = buf0.shape[1]

    def start(tile, buf, sems):
        n0 = base + tile * tc
        _row_gather_start(p0_ref, n0, ys_ref, buf.at[0], sems.at[0])
        _row_gather_start(p1_ref, n0, ys_ref, buf.at[1], sems.at[1])

    def wait(buf, sems):
        _row_gather_wait(ys_ref, buf.at[0], sems.at[0])
        _row_gather_wait(ys_ref, buf.at[1], sems.at[1])

    @pl.when(i == 0)
    def _():
        start(0, buf0, sem.at[0])

    def run(cur, cur_sems, nxt, nxt_sems):
        wait(cur, cur_sems)
        lo0, hi0 = _unpack_bf16_pairs(cur[0])
        lo1, hi1 = _unpack_bf16_pairs(cur[1])
        start(jnp.minimum(i + 1, last), nxt, nxt_sems)
        meta = meta_ref[...]
        w0 = meta[:, 0:1]
        w1 = meta[:, 1:2]
        half = lo0.shape[1]
        y_lo = x_ref[:, :half] + w0 * lo0 + w1 * lo1
        y_hi = x_ref[:, half:] + w0 * hi0 + w1 * hi1
        if final_norm:
            ms = (jnp.sum(y_lo * y_lo, axis=-1, keepdims=True)
                  + jnp.sum(y_hi * y_hi, axis=-1, keepdims=True)) / (2 * half)
            inv = lax.rsqrt(ms + RMS_EPS)
            y_lo = y_lo * inv * g_ref[:, :half]
            y_hi = y_hi * inv * g_ref[:, half:]
        o_ref[:, :half] = y_lo
        o_ref[:, half:] = y_hi

        @pl.when(i == last)
        def _():
            wait(nxt, nxt_sems)

    @pl.when(i % 2 == 0)
    def _():
        run(buf0, sem.at[0], buf1, sem.at[1])

    @pl.when(i % 2 == 1)
    def _():
        run(buf1, sem.at[1], buf0, sem.at[0])


def moe_combine(x, meta, ys, pos0, pos1, gain, *, base, count, final_norm, tc=256):
    d = x.shape[1]
    assert base % tc == 0 and count % tc == 0
    blk0 = base // tc
    grid_spec = pltpu.PrefetchScalarGridSpec(
        num_scalar_prefetch=2,
        grid=(count // tc,),
        in_specs=[pl.BlockSpec((tc, d), lambda i, p0, p1: (blk0 + i, 0)),
                  pl.BlockSpec((tc, LANE), lambda i, p0, p1: (blk0 + i, 0)),
                  pl.BlockSpec((1, d), lambda i, p0, p1: (0, 0)),
                  pl.BlockSpec(memory_space=pl.ANY)],
        out_specs=pl.BlockSpec((tc, d), lambda i, p0, p1: (i, 0)),
        scratch_shapes=[pltpu.VMEM((2, tc, d // 2), U32), pltpu.VMEM((2, tc, d // 2), U32),
                        pltpu.SemaphoreType.DMA((2, 2))],
    )
    return pl.pallas_call(
        functools.partial(_combine_kernel, base=base, final_norm=final_norm),
        grid_spec=grid_spec,
        out_shape=jax.ShapeDtypeStruct((count, d), F32),
        compiler_params=_cparams("arbitrary"),
        name="moe_combine",
    )(pos0, pos1, x, meta, gain.reshape(1, d), ys)


def moe_experts(x, gain, w_rg, b_rg, w_re, b_re, w_gate, w_up, w_down, layer):
    n = x.shape[0]
    n_rows = 2 * n + N_EXPERTS * MOE_TILE
    assert (2 * n) % MOE_TILE == 0
    h, meta = router(x, gain, w_rg, b_rg, w_re, b_re)
    pos0, pos1, sorted_tok, tile_expert, tile_valid = _routing_plan(meta, n_rows)
    tile_expert = tile_expert + layer * N_EXPERTS
    hid = moe_up(h, w_gate, w_up, sorted_tok, tile_expert, tile_valid)
    ys = moe_down(hid, w_down, tile_expert, tile_valid)
    return meta, ys, pos0, pos1


def kernel(x_prompt, x_sample, mem_prompt, mem_sample, norm_mix, norm_mem, norm_ffn, norm_final, na_w_in, na_rpb, s5_w_in, s5_a_re, s5_a_im, s5_log_dt, s5_b_re, s5_b_im, s5_c_re, s5_c_im, s5_d, s5_w_glu, s5_b_glu, mem_w_kv, w_out, moe_w_rg, moe_b_rg, moe_w_re, moe_b_re, moe_w_gate, moe_w_up, moe_w_down):
    d = x_prompt.shape[-1]
    groups = [x_prompt, x_sample]
    mems = [mem_prompt, mem_sample]
    seq_lens = [g.shape[1] for g in groups for _ in range(g.shape[0])]
    n_mem = mem_prompt.shape[1]
    depth = norm_mix.shape[0]
    x = jnp.concatenate([g.reshape(-1, d) for g in groups], axis=0)
    mem = jnp.concatenate([m.reshape(-1, d) for m in mems], axis=0)
    n = x.shape[0]
    tq_mem = 512
    blk_seq = []
    for s, length in enumerate(seq_lens):
        assert length % tq_mem == 0
        blk_seq += [s] * (length // tq_mem)
    blk_seq = jnp.asarray(np.asarray(blk_seq, np.int32))
    na_heads = na_rpb.shape[1]
    w_gate_all = moe_w_gate.astype(BF16).reshape((-1,) + moe_w_gate.shape[2:])
    w_up_all = moe_w_up.astype(BF16).reshape((-1,) + moe_w_up.shape[2:])
    w_down_all = moe_w_down.astype(BF16).reshape((-1,) + moe_w_down.shape[2:])

    meta = ys = pos0 = pos1 = None
    for i in range(depth):
        j = i // 2
        if i > 0:
            x = moe_combine(x, meta, ys, pos0, pos1, norm_final, base=0, count=n, final_norm=False)
        kv = norm_proj(mem, norm_mem[i], mem_w_kv[i].astype(BF16), BF16, tm=n_mem)
        if i % 2 == 0:
            n_q = na_heads * NA_HEAD_DIM
            col_scale = jnp.concatenate([jnp.full((n_q,), NA_HEAD_DIM ** -0.5, F32),
                                         jnp.ones((na_w_in.shape[2] - n_q,), F32)])
            proj = norm_proj(x, norm_mix[i], (na_w_in[j] * col_scale[None, :]).astype(BF16), BF16, tm=512)
            tok = neighbourhood_attention(proj, na_rpb[j], seq_lens)
            q_slab0 = 3 * na_heads
        else:
            proj = norm_proj(x, norm_mix[i], s5_w_in[j].astype(BF16), F32, tm=512)
            lam_re, lam_im, bb_re, bb_im = s5_discretise(s5_a_re[j], s5_a_im[j], s5_log_dt[j], s5_b_re[j], s5_b_im[j])
            b_op, c_op, lam, d_op = _s5_operands(lam_re, lam_im, bb_re, bb_im, s5_c_re[j], s5_c_im[j], s5_d[j])
            g = s5_mixer(proj, b_op, c_op, lam, d_op, seq_lens)
            tok = glu(g, s5_w_glu[j].astype(BF16), s5_b_glu[j])
            q_slab0 = b_op.shape[1]
        mo = memory_attention(proj, q_slab0, kv, blk_seq, n_mem, tq=tq_mem)
        x = out_proj(x, tok, mo, w_out[i].astype(BF16))
        meta, ys, pos0, pos1 = moe_experts(x, norm_ffn[i], moe_w_rg[i], moe_b_rg[i], moe_w_re[i], moe_b_re[i],
                                           w_gate_all, w_up_all, w_down_all, i)
    outs, base = [], 0
    for g in groups:
        count = g.shape[0] * g.shape[1]
        y = moe_combine(x, meta, ys, pos0, pos1, norm_final, base=base, count=count, final_norm=True)
        outs.append(y.reshape(g.shape))
        base += count
    return tuple(outs)
```

```python
import functools
import math

import numpy as np
import jax
import jax.numpy as jnp
from jax import lax
from jax.experimental import pallas as pl
from jax.experimental.pallas import tpu as pltpu

F32 = jnp.float32
BF16 = jnp.bfloat16
U32 = jnp.uint32

LANE = 128
SUBLANE = 8
VMEM_LIMIT_BYTES = 56 * 1024 * 1024

RMS_EPS = 1e-6
NEG_INF = -1e30
GRID_W = 64
ROW_WIN = 8
COL_WIN = 16
NA_HEAD_DIM = 128
MEM_HEADS = 4
S5_GROUP = 16
S5_STATE = 64
N_EXPERT_GROUPS = 4
EXPERTS_PER_GROUP = 4
N_EXPERTS = N_EXPERT_GROUPS * EXPERTS_PER_GROUP

NA_Q_ROWS = 8
NA_K_ROWS = 2 * ROW_WIN
NA_HEADS_PER_STEP = 4
SUPER = 8192
S5_SEGMENTS = SUBLANE
S5_TBLK = 64
MOE_TILE = 256


def _cparams(*sem):
    return pltpu.CompilerParams(dimension_semantics=sem, vmem_limit_bytes=VMEM_LIMIT_BYTES)


def _rms_norm(x, gain):
    ms = jnp.mean(x * x, axis=-1, keepdims=True)
    return x * lax.rsqrt(ms + RMS_EPS) * gain


def _sigmoid(x):
    return 1.0 / (1.0 + jnp.exp(-x))


def _split_hi_lo(x):
    hi = lax.bitcast_convert_type(lax.bitcast_convert_type(x, U32) & jnp.uint32(0xFFFF0000), F32)
    return hi.astype(BF16), (x - hi).astype(BF16)


def _pack_bf16_pairs(x):
    k = x.shape[1] // 2
    bits = lax.bitcast_convert_type(x, U32)
    return (bits[:, k:] & jnp.uint32(0xFFFF0000)) | (bits[:, :k] >> 16)


def _unpack_bf16_pairs(w):
    lo = lax.bitcast_convert_type(w << 16, F32)
    hi = lax.bitcast_convert_type(w & jnp.uint32(0xFFFF0000), F32)
    return lo, hi


def _norm_proj_kernel(x_ref, g_ref, w_ref, o_ref, h_scr):
    @pl.when(pl.program_id(1) == 0)
    def _():
        h_scr[...] = _rms_norm(x_ref[...], g_ref[...]).astype(BF16)

    acc = jnp.dot(h_scr[...], w_ref[...], preferred_element_type=F32)
    for c in range(o_ref.shape[0]):
        o_ref[c] = acc[:, c * LANE:(c + 1) * LANE].astype(o_ref.dtype)


def norm_proj(x, gain, w, out_dtype, tm, tn=1024):
    m, d = x.shape
    n_out = w.shape[1]
    assert m % tm == 0 and n_out % tn == 0 and tn % LANE == 0
    return pl.pallas_call(
        _norm_proj_kernel,
        grid=(m // tm, n_out // tn),
        in_specs=[pl.BlockSpec((tm, d), lambda i, j: (i, 0)),
                  pl.BlockSpec((1, d), lambda i, j: (0, 0)),
                  pl.BlockSpec((d, tn), lambda i, j: (0, j))],
        out_specs=pl.BlockSpec((tn // LANE, tm, LANE), lambda i, j: (j, i, 0)),
        out_shape=jax.ShapeDtypeStruct((n_out // LANE, m, LANE), out_dtype),
        scratch_shapes=[pltpu.VMEM((tm, d), BF16)],
        compiler_params=_cparams("parallel", "arbitrary"),
        name="norm_proj",
    )(x, gain.reshape(1, d), w)


def _na_kernel(ks_ref, ty_ref, pi_ref, q_ref, k_ref, v_ref, b_ref, o_ref, *, nqb):
    step = pl.program_id(1) * nqb + pl.program_id(2)
    start = pl.multiple_of(ks_ref[step] * GRID_W, GRID_W)
    nk = NA_K_ROWS * GRID_W
    npair = NA_K_ROWS // 2
    ty = ty_ref[step]
    for h in range(q_ref.shape[0]):
        q = q_ref[h]
        k = k_ref[h, pl.ds(start, nk), :]
        v = v_ref[h, pl.ds(start, nk), :]
        s = lax.dot_general(q, k, (((1,), (1,)), ((), ())), preferred_element_type=F32)
        bias = jnp.concatenate(
            [jnp.concatenate([b_ref[h, pi_ref[(ty * NA_Q_ROWS + i) * npair + jp]] for jp in range(npair)], axis=1)
             for i in range(NA_Q_ROWS)], axis=0)
        s = s + bias
        m = jnp.max(s, axis=-1, keepdims=True)
        p = jnp.exp(s - m).astype(BF16)
        v_ones = jnp.concatenate([v, jnp.ones(v.shape, v.dtype)], axis=1)
        ol = jnp.dot(p, v_ones, preferred_element_type=F32)
        o = ol[:, :LANE] / ol[:, LANE:]
        o_ref[:, h * LANE:(h + 1) * LANE] = o.astype(o_ref.dtype)


def _na_tables(seq_lens):
    ks, ty = [], []
    tok = 0
    per_super = SUPER // GRID_W
    for length in seq_lens:
        rows = length // GRID_W
        assert length % (NA_Q_ROWS * GRID_W) == 0 and rows >= 3 * NA_Q_ROWS
        assert tok // SUPER == (tok + length - 1) // SUPER
        row0 = (tok % SUPER) // GRID_W
        for qb in range(rows // NA_Q_ROWS):
            ks.append(row0 + int(np.clip(qb * NA_Q_ROWS - ROW_WIN // 2, 0, rows - NA_K_ROWS)))
            ty.append(0 if qb == 0 else (2 if qb == rows // NA_Q_ROWS - 1 else 1))
        tok += length
    assert tok % SUPER == 0 and len(ks) == (tok // SUPER) * (per_super // NA_Q_ROWS)
    return np.asarray(ks, np.int32), np.asarray(ty, np.int32)


def _na_bias(rpb):
    heads = rpb.shape[0]
    cols = np.arange(GRID_W)
    c_start = np.clip(cols - COL_WIN // 2, 0, GRID_W - COL_WIN)
    in_win = (cols[None, :] >= c_start[:, None]) & (cols[None, :] < c_start[:, None] + COL_WIN)
    dc = np.clip(cols[None, :] - cols[:, None], -(COL_WIN - 1), COL_WIN - 1) + COL_WIN - 1
    col_bias = jnp.where(jnp.asarray(in_win), rpb.astype(F32)[:, :, dc], NEG_INF)
    n_dr = 2 * ROW_WIN - 1
    masked = jnp.full((heads, n_dr, GRID_W, GRID_W), NEG_INF, F32)
    left = jnp.concatenate([col_bias[:, :n_dr - 1], masked, col_bias, masked[:, :1]], axis=1)
    right = jnp.concatenate([col_bias[:, 1:], col_bias, masked, masked[:, :1]], axis=1)
    pairs = jnp.concatenate([left, right], axis=-1)
    both, right_only, left_only, none = 0, n_dr - 1, 2 * n_dr - 1, 3 * n_dr - 1
    half = ROW_WIN // 2
    index = []
    for ty in range(3):
        for i in range(NA_Q_ROWS):
            q_rel = (i, i + half, i + ROW_WIN)[ty]
            w_rel = (max(i - half, 0), i, min(i + half, ROW_WIN))[ty]
            for jp in range(NA_K_ROWS // 2):
                j0, j1 = 2 * jp, 2 * jp + 1
                v0, v1 = w_rel <= j0 < w_rel + ROW_WIN, w_rel <= j1 < w_rel + ROW_WIN
                d0, d1 = j0 - q_rel + ROW_WIN - 1, j1 - q_rel + ROW_WIN - 1
                assert (not v0 or 0 <= d0 < n_dr) and (not v1 or 0 <= d1 < n_dr)
                index.append(both + d0 if v0 and v1 else right_only + d1 if v1 else left_only + d0 if v0 else none)
    return pairs, np.asarray(index, np.int32)


def neighbourhood_attention(qkv, rpb, seq_lens):
    heads = rpb.shape[0]
    n = qkv.shape[1]
    nsup = n // SUPER
    tq = NA_Q_ROWS * GRID_W
    nqb = SUPER // tq
    ks_tab, ty_tab = _na_tables(seq_lens)
    pairs, pair_index = _na_bias(rpb)
    hp = NA_HEADS_PER_STEP
    assert heads % hp == 0
    ng = heads // hp
    grid_spec = pltpu.PrefetchScalarGridSpec(
        num_scalar_prefetch=3,
        grid=(ng, nsup, nqb),
        in_specs=[
            pl.BlockSpec((hp, tq, LANE), lambda h, s, b, ks, ty, pi: (h, s * nqb + b, 0)),
            pl.BlockSpec((hp, SUPER, LANE), lambda h, s, b, ks, ty, pi: (ng + h, s, 0)),
            pl.BlockSpec((hp, SUPER, LANE), lambda h, s, b, ks, ty, pi: (2 * ng + h, s, 0)),
            pl.BlockSpec((hp,) + pairs.shape[1:], lambda h, s, b, ks, ty, pi: (h, 0, 0, 0)),
        ],
        out_specs=pl.BlockSpec((tq, hp * LANE), lambda h, s, b, ks, ty, pi: (s * nqb + b, h)),
    )
    return pl.pallas_call(
        functools.partial(_na_kernel, nqb=nqb),
        grid_spec=grid_spec,
        out_shape=jax.ShapeDtypeStruct((n, heads * LANE), BF16),
        compiler_params=_cparams("parallel", "arbitrary", "arbitrary"),
        name="neighbourhood_attention",
    )(jnp.asarray(ks_tab), jnp.asarray(ty_tab), jnp.asarray(pair_index), qkv, qkv, qkv, pairs)


def _mem_attn_kernel(seq_ref, q_ref, kv_ref, o_ref, *, scale):
    del seq_ref
    hd = 2 * LANE
    for h in range(MEM_HEADS):
        q = jnp.concatenate([q_ref[2 * h], q_ref[2 * h + 1]], axis=-1).astype(BF16)
        k = jnp.concatenate([kv_ref[2 * h], kv_ref[2 * h + 1]], axis=-1)
        v = jnp.concatenate([kv_ref[2 * MEM_HEADS + 2 * h], kv_ref[2 * MEM_HEADS + 2 * h + 1]], axis=-1)
        s = lax.dot_general(q, k, (((1,), (1,)), ((), ())), preferred_element_type=F32) * scale
        m = jnp.max(s, axis=-1, keepdims=True)
        p = jnp.exp(s - m)
        l = jnp.sum(p, axis=-1, keepdims=True)
        o = jnp.dot(p.astype(BF16), v, preferred_element_type=F32) / l
        o_ref[:, h * hd:(h + 1) * hd] = o.astype(o_ref.dtype)


def memory_attention(proj, q_slab0, kv, blk_seq, n_mem, tq=512):
    n = proj.shape[1]
    nq = 2 * MEM_HEADS
    assert q_slab0 % nq == 0
    grid_spec = pltpu.PrefetchScalarGridSpec(
        num_scalar_prefetch=1,
        grid=(n // tq,),
        in_specs=[pl.BlockSpec((nq, tq, LANE), lambda i, sq: (q_slab0 // nq, i, 0)),
                  pl.BlockSpec((2 * nq, n_mem, LANE), lambda i, sq: (0, sq[i], 0))],
        out_specs=pl.BlockSpec((tq, nq * LANE), lambda i, sq: (i, 0)),
    )
    return pl.pallas_call(
        functools.partial(_mem_attn_kernel, scale=(2 * LANE) ** -0.5),
        grid_spec=grid_spec,
        out_shape=jax.ShapeDtypeStruct((n, nq * LANE), BF16),
        compiler_params=_cparams("parallel"),
        name="memory_attention",
    )(blk_seq, proj, kv)


def _s5_param_kernel(are_ref, aim_ref, ldt_ref, bre_ref, bim_ref, lre_ref, lim_ref, bbre_ref, bbim_ref):
    a_re = are_ref[...]
    a_im = aim_ref[...]
    dt = jnp.exp(ldt_ref[...])
    mag = jnp.exp(a_re * dt)
    ang = a_im * dt
    lam_re = mag * jnp.cos(ang)
    lam_im = mag * jnp.sin(ang)
    den = a_re * a_re + a_im * a_im
    n_re = lam_re - 1.0
    coef_re = (n_re * a_re + lam_im * a_im) / den
    coef_im = (lam_im * a_re - n_re * a_im) / den
    lre_ref[...] = lam_re
    lim_ref[...] = lam_im
    for c in range(bre_ref.shape[0]):
        b_re = bre_ref[c]
        b_im = bim_ref[c]
        bbre_ref[c] = coef_re * b_re - coef_im * b_im
        bbim_ref[c] = coef_re * b_im + coef_im * b_re


def s5_discretise(a_re, a_im, log_dt, b_re, b_im):
    two, g, p = a_re.shape
    c = b_re.shape[-1]
    rows = two * g
    ldt = jnp.broadcast_to(log_dt.reshape(rows, 1), (rows, p))
    b_re_t = b_re.reshape(rows, p, c).transpose(2, 0, 1)
    b_im_t = b_im.reshape(rows, p, c).transpose(2, 0, 1)
    return pl.pallas_call(
        _s5_param_kernel,
        out_shape=(jax.ShapeDtypeStruct((rows, p), F32), jax.ShapeDtypeStruct((rows, p), F32),
                   jax.ShapeDtypeStruct((c, rows, p), F32), jax.ShapeDtypeStruct((c, rows, p), F32)),
        name="s5_discretise",
    )(a_re.reshape(rows, p), a_im.reshape(rows, p), ldt, b_re_t, b_im_t)


def _s5_operands(lam_re, lam_im, bb_re, bb_im, c_re, c_im, d):
    c, rows, p = bb_re.shape
    g = rows // 2
    gs = LANE // c
    ns = g // gs
    same_group = jnp.eye(gs, dtype=jnp.bool_)

    def bmat(bb):
        x = bb.reshape(c, 2, ns, gs, p).transpose(1, 2, 3, 0, 4)
        blocks = jnp.where(same_group[None, None, :, None, :, None], x[:, :, :, :, None, :], 0.0)
        return blocks.reshape(2, ns, gs * c, gs * p)

    def cmat(cc):
        x = cc.astype(F32).reshape(2, ns, gs, c, p).transpose(0, 1, 2, 4, 3)
        blocks = jnp.where(same_group[None, None, :, None, :, None], x[:, :, :, :, None, :], 0.0)
        return blocks.reshape(2, ns, gs * p, gs * c)

    b_op = jnp.concatenate([bmat(bb_re), bmat(bb_im)], axis=-1).astype(BF16)
    c_op = jnp.concatenate([cmat(c_re), -cmat(c_im)], axis=-2).astype(BF16)
    lam = jnp.concatenate([lam_re.reshape(2, ns, gs * p), lam_im.reshape(2, ns, gs * p)], axis=-1)
    lam = jnp.broadcast_to(lam[:, :, None, :], (2, ns, SUBLANE, 2 * gs * p))
    d_op = d.astype(F32).reshape(ns, 1, gs * c)
    return b_op, c_op, lam, d_op


def _s5_kernel(keep_ref, u_hbm, bm_ref, cm_ref, lam_ref, d_ref, o_ref,
               u_scr, u_sem, up_scr, y_scr, y2_scr, bu0_scr, bu1_scr, xs0_scr, xs1_scr, e_scr, i_scr, *, lseg, tblk):
    sup = pl.program_id(0)
    nblk = lseg // tblk
    rb = SUBLANE * tblk
    ns = bu0_scr.shape[1] // 2
    nchunk = ns // LANE
    d_row = d_ref[0]
    pitch = u_scr.shape[1] // S5_SEGMENTS
    n_slab = pl.num_programs(1)
    lin = sup * n_slab + pl.program_id(1)
    slot = lin % 2

    def segment_copies(step, slot):
        b = step // n_slab
        s = step % n_slab
        return [pltpu.make_async_copy(u_hbm.at[s, pl.ds(b * (S5_SEGMENTS * lseg) + r * lseg, lseg), :],
                                      u_scr.at[slot, pl.ds(r * pitch, lseg), :], u_sem.at[slot])
                for r in range(S5_SEGMENTS)]

    @pl.when(lin == 0)
    def _():
        for cp in segment_copies(lin, slot):
            cp.start()

    for cp in segment_copies(lin, slot):
        cp.wait()

    @pl.when(lin + 1 < pl.num_programs(0) * n_slab)
    def _():
        for cp in segment_copies(lin + 1, 1 - slot):
            cp.start()

    u2 = u_scr.at[slot]

    def permute_in(t2, carry):
        t = t2 * 2
        a = u2[pl.ds(t, SUBLANE, stride=pitch), :]
        b = u2[pl.ds(t + 1, SUBLANE, stride=pitch), :]
        ab = jnp.concatenate([a, b], axis=0)
        rows = pl.ds(pl.multiple_of(t * SUBLANE, 2 * SUBLANE), 2 * SUBLANE)
        up_scr[rows, :] = ab.astype(BF16)
        y_scr[rows, :] = ab * d_row
        return carry
    lax.fori_loop(0, lseg // 2, permute_in, 0, unroll=4)
    xs1_scr[...] = jnp.zeros(xs1_scr.shape, xs1_scr.dtype)

    for direction in range(2):
        rev = direction == 1
        lam_r = [lam_ref[direction, 0, :, c * LANE:(c + 1) * LANE] for c in range(nchunk)]
        lam_i = [lam_ref[direction, 0, :, ns + c * LANE:ns + (c + 1) * LANE] for c in range(nchunk)]
        steps = list(range(tblk))[::-1] if rev else list(range(tblk))

        def block_rows(k, rev=rev):
            k = jnp.clip(k, 0, nblk - 1)
            blk = (nblk - 1 - k) if rev else k
            return pl.ds(pl.multiple_of(blk * rb, rb), rb)

        def bu_dot(k, direction=direction, block_rows=block_rows):
            return jnp.dot(up_scr[block_rows(k), :], bm_ref[direction, 0], preferred_element_type=F32)

        def scan_block(bu, x, xs, lam_r=lam_r, lam_i=lam_i, steps=steps):
            xr, xi = list(x[:nchunk]), list(x[nchunk:])
            held = None
            for n, t in enumerate(steps):
                rows = slice(t * SUBLANE, (t + 1) * SUBLANE)
                for c in range(nchunk):
                    b_r = bu[rows, c * LANE:(c + 1) * LANE]
                    b_i = bu[rows, ns + c * LANE:ns + (c + 1) * LANE]
                    r = lam_r[c] * xr[c] - lam_i[c] * xi[c] + b_r
                    i = lam_r[c] * xi[c] + lam_i[c] * xr[c] + b_i
                    xr[c], xi[c] = r, i
                if xs is None:
                    continue
                if n % 2 == 0:
                    held = (t, list(xr), list(xi))
                    continue
                lo, hi = (held, (t, xr, xi)) if held[0] < t else ((t, xr, xi), held)
                pair = slice(lo[0] * SUBLANE, (lo[0] + 2) * SUBLANE)
                for c in range(nchunk):
                    xs[pair, c * LANE:(c + 1) * LANE] = jnp.concatenate([lo[1][c], hi[1][c]], axis=0).astype(BF16)
                    xs[pair, ns + c * LANE:ns + (c + 1) * LANE] = (
                        jnp.concatenate([lo[2][c], hi[2][c]], axis=0).astype(BF16))
            return tuple(xr + xi)

        def pass_a(k2, x, bu_dot=bu_dot, scan_block=scan_block):
            bu1_scr[...] = bu_dot(2 * k2 + 1)
            x = scan_block(bu0_scr, x, None)
            bu0_scr[...] = bu_dot(2 * k2 + 2)
            return scan_block(bu1_scr, x, None)
        bu0_scr[...] = bu_dot(0)
        zero = tuple(jnp.zeros((SUBLANE, LANE), F32) for _ in range(2 * nchunk))
        ends = lax.fori_loop(0, nblk // 2, pass_a, zero)
        for c in range(2 * nchunk):
            e_scr[:, c * LANE:(c + 1) * LANE] = ends[c]

        p_r = lam_ref[direction, 0, 0:1, 0:ns]
        p_i = lam_ref[direction, 0, 0:1, ns:2 * ns]
        for _ in range(int(math.log2(lseg))):
            p_r, p_i = p_r * p_r - p_i * p_i, 2.0 * p_r * p_i
        order = list(range(S5_SEGMENTS))[::-1] if rev else list(range(S5_SEGMENTS))
        s_r = jnp.zeros((1, ns), F32)
        s_i = jnp.zeros((1, ns), F32)
        i_scr[order[0]:order[0] + 1, :] = jnp.zeros((1, 2 * ns), F32)
        for prev, cur in zip(order[:-1], order[1:]):
            e_r = e_scr[prev:prev + 1, 0:ns]
            e_i = e_scr[prev:prev + 1, ns:2 * ns]
            keep = keep_ref[(sup * 2 + direction) * S5_SEGMENTS + cur].astype(F32)
            n_r = (p_r * s_r - p_i * s_i + e_r) * keep
            n_i = (p_r * s_i + p_i * s_r + e_i) * keep
            i_scr[cur:cur + 1, 0:ns] = n_r
            i_scr[cur:cur + 1, ns:2 * ns] = n_i
            s_r, s_i = n_r, n_i

        def project(k, xs, scale, direction=direction, block_rows=block_rows):
            rows = block_rows(k)
            yb = jnp.dot(xs[...], cm_ref[direction, 0], preferred_element_type=F32)
            y_scr[rows, :] = y_scr[rows, :] + yb * scale

        def pass_b(k2, x, bu_dot=bu_dot, scan_block=scan_block, project=project):
            bu1_scr[...] = bu_dot(2 * k2 + 1)
            x = scan_block(bu0_scr, x, xs0_scr)
            project(2 * k2 - 1, xs1_scr, jnp.where(k2 > 0, 1.0, 0.0))
            bu0_scr[...] = bu_dot(2 * k2 + 2)
            x = scan_block(bu1_scr, x, xs1_scr)
            project(2 * k2, xs0_scr, 1.0)
            return x
        bu0_scr[...] = bu_dot(0)
        start = tuple(i_scr[:, c * LANE:(c + 1) * LANE] for c in range(2 * nchunk))
        lax.fori_loop(0, nblk // 2, pass_b, start)
        project(nblk - 1, xs1_scr, 1.0)

    def permute_out(t, carry):
        y = y_scr[pl.ds(pl.multiple_of(t * SUBLANE, SUBLANE), SUBLANE), :]
        gelu = 0.5 * y * (1.0 + jnp.tanh(math.sqrt(2.0 / math.pi) * (y + 0.044715 * (y * y * y))))
        y2_scr[pl.ds(t, SUBLANE, stride=pitch), :] = gelu
        return carry
    lax.fori_loop(0, lseg, permute_out, 0, unroll=8)
    for r in range(S5_SEGMENTS):
        o_ref[r * lseg:(r + 1) * lseg, :] = y2_scr[r * pitch:r * pitch + lseg, :].astype(o_ref.dtype)


def _s5_keep_table(seq_lens, lseg):
    starts, ends = set(), set()
    tok = 0
    for length in seq_lens:
        assert length % lseg == 0
        starts.add(tok // lseg)
        tok += length
        ends.add(tok // lseg - 1)
    nseg = tok // lseg
    assert nseg % S5_SEGMENTS == 0
    keep = np.ones((nseg // S5_SEGMENTS, 2, S5_SEGMENTS), np.int32)
    for s in range(nseg):
        if s in starts:
            keep[s // S5_SEGMENTS, 0, s % S5_SEGMENTS] = 0
        if s in ends:
            keep[s // S5_SEGMENTS, 1, s % S5_SEGMENTS] = 0
    return keep.reshape(-1)


def s5_mixer(proj, b_op, c_op, lam, d_op, seq_lens):
    n = proj.shape[1]
    ns = b_op.shape[1]
    nstate2 = b_op.shape[-1]
    nsup = n // SUPER
    lseg = SUPER // S5_SEGMENTS
    tblk = min(S5_TBLK, lseg)
    assert lseg & (lseg - 1) == 0 and lseg % (2 * tblk) == 0 and tblk % 2 == 0
    keep = _s5_keep_table(seq_lens, lseg)
    rows = SUBLANE * tblk
    pitch = lseg + SUBLANE
    grid_spec = pltpu.PrefetchScalarGridSpec(
        num_scalar_prefetch=1,
        grid=(nsup, ns),
        in_specs=[
            pl.BlockSpec(memory_space=pl.ANY),
            pl.BlockSpec((2, 1, LANE, nstate2), lambda b, s, kp: (0, s, 0, 0)),
            pl.BlockSpec((2, 1, nstate2, LANE), lambda b, s, kp: (0, s, 0, 0)),
            pl.BlockSpec((2, 1, SUBLANE, nstate2), lambda b, s, kp: (0, s, 0, 0)),
            pl.BlockSpec((1, 1, LANE), lambda b, s, kp: (s, 0, 0)),
        ],
        out_specs=pl.BlockSpec((SUPER, LANE), lambda b, s, kp: (b, s)),
        scratch_shapes=[
            pltpu.VMEM((2, S5_SEGMENTS * pitch, LANE), F32),
            pltpu.SemaphoreType.DMA((2,)),
            pltpu.VMEM((SUPER, LANE), BF16),
            pltpu.VMEM((SUPER, LANE), F32),
            pltpu.VMEM((S5_SEGMENTS * pitch, LANE), F32),
            pltpu.VMEM((rows, nstate2), F32),
            pltpu.VMEM((rows, nstate2), F32),
            pltpu.VMEM((rows, nstate2), BF16),
            pltpu.VMEM((rows, nstate2), BF16),
            pltpu.VMEM((SUBLANE, nstate2), F32),
            pltpu.VMEM((SUBLANE, nstate2), F32),
        ],
    )
    return pl.pallas_call(
        functools.partial(_s5_kernel, lseg=lseg, tblk=tblk),
        grid_spec=grid_spec,
        out_shape=jax.ShapeDtypeStruct((n, ns * LANE), BF16),
        compiler_params=_cparams("arbitrary", "arbitrary"),
        name="s5_mixer",
    )(jnp.asarray(keep), proj, b_op, c_op, lam, d_op)


def _glu_kernel(g_ref, w_ref, b_ref, o_ref):
    tn = o_ref.shape[1]
    col = pl.multiple_of(pl.program_id(1) * tn, tn)
    acc = jnp.dot(g_ref[...], w_ref[...], preferred_element_type=F32) + b_ref[...]
    g = g_ref[:, pl.ds(col, tn)].astype(F32)
    o_ref[...] = (g * _sigmoid(acc)).astype(o_ref.dtype)


def glu(g, w, b, tm=1024, tn=1024):
    m, k = g.shape
    assert m % tm == 0 and k % tn == 0
    return pl.pallas_call(
        _glu_kernel,
        grid=(m // tm, k // tn),
        in_specs=[pl.BlockSpec((tm, k), lambda i, j: (i, 0)),
                  pl.BlockSpec((k, tn), lambda i, j: (0, j)),
                  pl.BlockSpec((1, tn), lambda i, j: (0, j))],
        out_specs=pl.BlockSpec((tm, tn), lambda i, j: (i, j)),
        out_shape=jax.ShapeDtypeStruct((m, k), BF16),
        compiler_params=_cparams("parallel", "arbitrary"),
        name="glu",
    )(g, w, b.reshape(1, k).astype(F32))


def _out_proj_kernel(x_ref, tok_ref, mo_ref, w1_ref, w2_ref, o_ref):
    acc = jnp.dot(tok_ref[...], w1_ref[...], preferred_element_type=F32)
    acc = acc + jnp.dot(mo_ref[...], w2_ref[...], preferred_element_type=F32)
    o_ref[...] = x_ref[...] + acc


def out_proj(x, tok, mo, w, tm=1024, tn=512):
    m, d = x.shape
    k1, k2 = tok.shape[1], mo.shape[1]
    assert m % tm == 0 and d % tn == 0 and w.shape[0] == k1 + k2 and k1 % k2 == 0
    return pl.pallas_call(
        _out_proj_kernel,
        grid=(m // tm, d // tn),
        in_specs=[pl.BlockSpec((tm, tn), lambda i, j: (i, j)),
                  pl.BlockSpec((tm, k1), lambda i, j: (i, 0)),
                  pl.BlockSpec((tm, k2), lambda i, j: (i, 0)),
                  pl.BlockSpec((k1, tn), lambda i, j: (0, j)),
                  pl.BlockSpec((k2, tn), lambda i, j: (k1 // k2, j))],
        out_specs=pl.BlockSpec((tm, tn), lambda i, j: (i, j)),
        out_shape=jax.ShapeDtypeStruct((m, d), F32),
        compiler_params=_cparams("parallel", "arbitrary"),
        name="out_proj",
    )(x, tok, mo, w, w)


def _router_kernel(x_ref, g_ref, whi_ref, wlo_ref, b_ref, h_ref, meta_ref):
    t = _rms_norm(x_ref[...], g_ref[...])
    h_ref[...] = _pack_bf16_pairs(t)
    t_hi, t_lo = _split_hi_lo(t)
    logits = (jnp.dot(t_hi, whi_ref[...], preferred_element_type=F32)
              + jnp.dot(t_lo, whi_ref[...], preferred_element_type=F32)
              + jnp.dot(t_hi, wlo_ref[...], preferred_element_type=F32)) + b_ref[...]
    lane = lax.broadcasted_iota(jnp.int32, logits.shape, 1)
    big = jnp.int32(LANE)
    is_g = lane < N_EXPERT_GROUPS
    gl = jnp.where(is_g, logits, -jnp.inf)
    gmax = jnp.max(gl, axis=-1, keepdims=True)
    gsum = jnp.sum(jnp.where(is_g, jnp.exp(gl - gmax), 0.0), axis=-1, keepdims=True)
    g_val = 1.0 / gsum
    g_idx = jnp.min(jnp.where(gl == gmax, lane, big), axis=-1, keepdims=True)
    e_lane = lane - N_EXPERT_GROUPS
    in_grp = (e_lane >= 0) & (e_lane < N_EXPERTS) & ((e_lane >> 2) == g_idx)
    el = jnp.where(in_grp, logits, -jnp.inf)
    v1 = jnp.max(el, axis=-1, keepdims=True)
    i1 = jnp.min(jnp.where(el == v1, lane, big), axis=-1, keepdims=True)
    el2 = jnp.where(lane == i1, -jnp.inf, el)
    v2 = jnp.max(el2, axis=-1, keepdims=True)
    i2 = jnp.min(jnp.where(el2 == v2, lane, big), axis=-1, keepdims=True)
    z = jnp.exp(v2 - v1)
    w1 = g_val / (1.0 + z)
    w2 = g_val * z / (1.0 + z)
    e1 = (i1 - N_EXPERT_GROUPS).astype(F32)
    e2 = (i2 - N_EXPERT_GROUPS).astype(F32)
    meta = jnp.where(lane == 0, w1, jnp.where(lane == 1, w2, jnp.where(lane == 2, e1, jnp.where(lane == 3, e2, 0.0))))
    meta_ref[...] = meta


def router(x, gain, w_rg, b_rg, w_re, b_re, tm=512):
    m, d = x.shape
    assert EXPERTS_PER_GROUP == 4 and m % tm == 0
    pad = LANE - N_EXPERT_GROUPS - N_EXPERTS
    w = jnp.concatenate([w_rg, w_re, jnp.zeros((d, pad), F32)], axis=1).astype(F32)
    w_hi, w_lo = _split_hi_lo(w)
    b = jnp.concatenate([b_rg, b_re, jnp.zeros((pad,), F32)]).astype(F32).reshape(1, LANE)
    return pl.pallas_call(
        _router_kernel,
        grid=(m // tm,),
        in_specs=[pl.BlockSpec((tm, d), lambda i: (i, 0)),
                  pl.BlockSpec((1, d), lambda i: (0, 0)),
                  pl.BlockSpec((d, LANE), lambda i: (0, 0)),
                  pl.BlockSpec((d, LANE), lambda i: (0, 0)),
                  pl.BlockSpec((1, LANE), lambda i: (0, 0))],
        out_specs=(pl.BlockSpec((tm, d // 2), lambda i: (i, 0)),
                   pl.BlockSpec((tm, LANE), lambda i: (i, 0))),
        out_shape=(jax.ShapeDtypeStruct((m, d // 2), U32), jax.ShapeDtypeStruct((m, LANE), F32)),
        compiler_params=_cparams("parallel"),
        name="moe_router",
    )(x, gain.reshape(1, d), w_hi, w_lo, b)


def _routing_plan(meta, n_rows):
    n = meta.shape[0]
    e = meta[:, 2:4].astype(jnp.int32).T
    experts = jnp.arange(N_EXPERTS, dtype=jnp.int32)[:, None]
    onehot = ((e[0][None, :] == experts) | (e[1][None, :] == experts)).astype(jnp.int32)
    counts = onehot.sum(axis=1)
    rank = jnp.cumsum(onehot, axis=1) - onehot
    padded = ((counts + MOE_TILE - 1) // MOE_TILE) * MOE_TILE
    ends = jnp.cumsum(padded)
    offs = ends - padded
    pos = jnp.take_along_axis(offs[:, None] + rank, e, axis=0).astype(jnp.int32)
    tok = jnp.arange(n, dtype=jnp.int32)
    sorted_tok = jnp.zeros((n_rows,), jnp.int32).at[pos.reshape(-1)].set(jnp.concatenate([tok, tok]))
    tile_start = jnp.arange(n_rows // MOE_TILE, dtype=jnp.int32) * MOE_TILE
    tile_expert = jnp.minimum((ends[None, :] <= tile_start[:, None]).sum(axis=1), N_EXPERTS - 1).astype(jnp.int32)
    tile_valid = (tile_start < ends[-1]).astype(jnp.int32)
    return pos[0], pos[1], sorted_tok, tile_expert, tile_valid


def _row_gather_start(idx_ref, base, src_ref, dst_ref, sem):
    for r in range(dst_ref.shape[0]):
        pltpu.make_async_copy(src_ref.at[pl.ds(idx_ref[base + r], 1), :], dst_ref.at[pl.ds(r, 1), :], sem).start()


def _row_gather_wait(src_ref, dst_ref, sem):
    pltpu.make_async_copy(src_ref.at[pl.ds(0, dst_ref.shape[0]), :], dst_ref, sem).wait()


def _moe_up_kernel(st_ref, te_ref, tv_ref, h_ref, wg_ref, wu_ref, o_ref, xbuf, sem):
    del te_ref
    t = pl.program_id(0)
    last = pl.num_programs(0) - 1
    slot = t % 2
    tile = xbuf.shape[1]

    @pl.when(t == 0)
    def _():
        _row_gather_start(st_ref, 0, h_ref, xbuf.at[0], sem.at[0])

    @pl.when(tv_ref[t] == 1)
    def _():
        _row_gather_wait(h_ref, xbuf.at[slot], sem.at[slot])
        nxt = jnp.minimum(t + 1, last)
        _row_gather_start(st_ref, nxt * tile, h_ref, xbuf.at[1 - slot], sem.at[1 - slot])
        lo, hi = _unpack_bf16_pairs(xbuf[slot])
        lo = lo.astype(BF16)
        hi = hi.astype(BF16)
        half = lo.shape[1]
        g = jnp.dot(lo, wg_ref[0, :half], preferred_element_type=F32)
        g = g + jnp.dot(hi, wg_ref[0, half:], preferred_element_type=F32)
        u = jnp.dot(lo, wu_ref[0, :half], preferred_element_type=F32)
        u = u + jnp.dot(hi, wu_ref[0, half:], preferred_element_type=F32)
        o_ref[...] = (g * _sigmoid(g) * u).astype(o_ref.dtype)

        @pl.when(t == last)
        def _():
            _row_gather_wait(h_ref, xbuf.at[1 - slot], sem.at[1 - slot])

    @pl.when(tv_ref[t] == 0)
    def _():
        @pl.when(tv_ref[jnp.maximum(t - 1, 0)] == 1)
        def _():
            _row_gather_wait(h_ref, xbuf.at[slot], sem.at[slot])
        o_ref[...] = jnp.zeros(o_ref.shape, o_ref.dtype)


def moe_up(h, w_gate, w_up, sorted_tok, tile_expert, tile_valid):
    half = h.shape[1]
    d = 2 * half
    f = w_gate.shape[-1]
    n_tiles = tile_expert.shape[0]
    grid_spec = pltpu.PrefetchScalarGridSpec(
        num_scalar_prefetch=3,
        grid=(n_tiles,),
        in_specs=[pl.BlockSpec(memory_space=pl.ANY),
                  pl.BlockSpec((1, d, f), lambda t, st, te, tv: (te[t], 0, 0)),
                  pl.BlockSpec((1, d, f), lambda t, st, te, tv: (te[t], 0, 0))],
        out_specs=pl.BlockSpec((MOE_TILE, f), lambda t, st, te, tv: (t, 0)),
        scratch_shapes=[pltpu.VMEM((2, MOE_TILE, half), U32), pltpu.SemaphoreType.DMA((2,))],
    )
    return pl.pallas_call(
        _moe_up_kernel,
        grid_spec=grid_spec,
        out_shape=jax.ShapeDtypeStruct((n_tiles * MOE_TILE, f), BF16),
        compiler_params=_cparams("arbitrary"),
        name="moe_up",
    )(sorted_tok, tile_expert, tile_valid, h, w_gate, w_up)


def _moe_down_kernel(te_ref, tv_ref, h_ref, wd_ref, o_ref):
    del te_ref
    t = pl.program_id(0)

    @pl.when(tv_ref[t] == 1)
    def _():
        y = jnp.dot(h_ref[...], wd_ref[0], preferred_element_type=F32)
        o_ref[...] = _pack_bf16_pairs(y)

    @pl.when(tv_ref[t] == 0)
    def _():
        o_ref[...] = jnp.zeros(o_ref.shape, o_ref.dtype)


def moe_down(hid, w_down, tile_expert, tile_valid):
    n_rows, f = hid.shape
    d = w_down.shape[-1]
    grid_spec = pltpu.PrefetchScalarGridSpec(
        num_scalar_prefetch=2,
        grid=(n_rows // MOE_TILE,),
        in_specs=[pl.BlockSpec((MOE_TILE, f), lambda t, te, tv: (t, 0)),
                  pl.BlockSpec((1, f, d), lambda t, te, tv: (te[t], 0, 0))],
        out_specs=pl.BlockSpec((MOE_TILE, d // 2), lambda t, te, tv: (t, 0)),
    )
    return pl.pallas_call(
        _moe_down_kernel,
        grid_spec=grid_spec,
        out_shape=jax.ShapeDtypeStruct((n_rows, d // 2), U32),
        compiler_params=_cparams("arbitrary"),
        name="moe_down",
    )(tile_expert, tile_valid, hid, w_down)


def _combine_kernel(p0_ref, p1_ref, x_ref, meta_ref, g_ref, ys_ref, o_ref, buf0, buf1, sem, *, base, final_norm):
    i = pl.program_id(0)
    last = pl.num_programs(0) - 1
    tc = buf0.shape[1]

    def start(tile, buf, sems):
        n0 = base + tile * tc
        _row_gather_start(p0_ref, n0, ys_ref, buf.at[0], sems.at[0])
        _row_gather_start(p1_ref, n0, ys_ref, buf.at[1], sems.at[1])

    def wait(buf, sems):
        _row_gather_wait(ys_ref, buf.at[0], sems.at[0])
        _row_gather_wait(ys_ref, buf.at[1], sems.at[1])

    @pl.when(i == 0)
    def _():
        start(0, buf0, sem.at[0])

    def run(cur, cur_sems, nxt, nxt_sems):
        wait(cur, cur_sems)
        lo0, hi0 = _unpack_bf16_pairs(cur[0])
        lo1, hi1 = _unpack_bf16_pairs(cur[1])
        start(jnp.minimum(i + 1, last), nxt, nxt_sems)
        meta = meta_ref[...]
        w0 = meta[:, 0:1]
        w1 = meta[:, 1:2]
        half = lo0.shape[1]
        y_lo = x_ref[:, :half] + w0 * lo0 + w1 * lo1
        y_hi = x_ref[:, half:] + w0 * hi0 + w1 * hi1
        if final_norm:
            ms = (jnp.sum(y_lo * y_lo, axis=-1, keepdims=True)
                  + jnp.sum(y_hi * y_hi, axis=-1, keepdims=True)) / (2 * half)
            inv = lax.rsqrt(ms + RMS_EPS)
            y_lo = y_lo * inv * g_ref[:, :half]
            y_hi = y_hi * inv * g_ref[:, half:]
        o_ref[:, :half] = y_lo
        o_ref[:, half:] = y_hi

        @pl.when(i == last)
        def _():
            wait(nxt, nxt_sems)

    @pl.when(i % 2 == 0)
    def _():
        run(buf0, sem.at[0], buf1, sem.at[1])

    @pl.when(i % 2 == 1)
    def _():
        run(buf1, sem.at[1], buf0, sem.at[0])


def moe_combine(x, meta, ys, pos0, pos1, gain, *, base, count, final_norm, tc=256):
    d = x.shape[1]
    assert base % tc == 0 and count % tc == 0
    blk0 = base // tc
    grid_spec = pltpu.PrefetchScalarGridSpec(
        num_scalar_prefetch=2,
        grid=(count // tc,),
        in_specs=[pl.BlockSpec((tc, d), lambda i, p0, p1: (blk0 + i, 0)),
                  pl.BlockSpec((tc, LANE), lambda i, p0, p1: (blk0 + i, 0)),
                  pl.BlockSpec((1, d), lambda i, p0, p1: (0, 0)),
                  pl.BlockSpec(memory_space=pl.ANY)],
        out_specs=pl.BlockSpec((tc, d), lambda i, p0, p1: (i, 0)),
        scratch_shapes=[pltpu.VMEM((2, tc, d // 2), U32), pltpu.VMEM((2, tc, d // 2), U32),
                        pltpu.SemaphoreType.DMA((2, 2))],
    )
    return pl.pallas_call(
        functools.partial(_combine_kernel, base=base, final_norm=final_norm),
        grid_spec=grid_spec,
        out_shape=jax.ShapeDtypeStruct((count, d), F32),
        compiler_params=_cparams("arbitrary"),
        name="moe_combine",
    )(pos0, pos1, x, meta, gain.reshape(1, d), ys)


def moe_experts(x, gain, w_rg, b_rg, w_re, b_re, w_gate, w_up, w_down, layer):
    n = x.shape[0]
    n_rows = 2 * n + N_EXPERTS * MOE_TILE
    assert (2 * n) % MOE_TILE == 0
    h, meta = router(x, gain, w_rg, b_rg, w_re, b_re)
    pos0, pos1, sorted_tok, tile_expert, tile_valid = _routing_plan(meta, n_rows)
    tile_expert = tile_expert + layer * N_EXPERTS
    hid = moe_up(h, w_gate, w_up, sorted_tok, tile_expert, tile_valid)
    ys = moe_down(hid, w_down, tile_expert, tile_valid)
    return meta, ys, pos0, pos1


def kernel(x_prompt, x_sample, mem_prompt, mem_sample, norm_mix, norm_mem, norm_ffn, norm_final, na_w_in, na_rpb, s5_w_in, s5_a_re, s5_a_im, s5_log_dt, s5_b_re, s5_b_im, s5_c_re, s5_c_im, s5_d, s5_w_glu, s5_b_glu, mem_w_kv, w_out, moe_w_rg, moe_b_rg, moe_w_re, moe_b_re, moe_w_gate, moe_w_up, moe_w_down):
    d = x_prompt.shape[-1]
    groups = [x_prompt, x_sample]
    mems = [mem_prompt, mem_sample]
    seq_lens = [g.shape[1] for g in groups for _ in range(g.shape[0])]
    n_mem = mem_prompt.shape[1]
    depth = norm_mix.shape[0]
    x = jnp.concatenate([g.reshape(-1, d) for g in groups], axis=0)
    mem = jnp.concatenate([m.reshape(-1, d) for m in mems], axis=0)
    n = x.shape[0]
    tq_mem = 512
    blk_seq = []
    for s, length in enumerate(seq_lens):
        assert length % tq_mem == 0
        blk_seq += [s] * (length // tq_mem)
    blk_seq = jnp.asarray(np.asarray(blk_seq, np.int32))
    na_heads = na_rpb.shape[1]
    w_gate_all = moe_w_gate.astype(BF16).reshape((-1,) + moe_w_gate.shape[2:])
    w_up_all = moe_w_up.astype(BF16).reshape((-1,) + moe_w_up.shape[2:])
    w_down_all = moe_w_down.astype(BF16).reshape((-1,) + moe_w_down.shape[2:])

    meta = ys = pos0 = pos1 = None
    for i in range(depth):
        j = i // 2
        if i > 0:
            x = moe_combine(x, meta, ys, pos0, pos1, norm_final, base=0, count=n, final_norm=False)
        kv = norm_proj(mem, norm_mem[i], mem_w_kv[i].astype(BF16), BF16, tm=n_mem)
        if i % 2 == 0:
            n_q = na_heads * NA_HEAD_DIM
            col_scale = jnp.concatenate([jnp.full((n_q,), NA_HEAD_DIM ** -0.5, F32),
                                         jnp.ones((na_w_in.shape[2] - n_q,), F32)])
            proj = norm_proj(x, norm_mix[i], (na_w_in[j] * col_scale[None, :]).astype(BF16), BF16, tm=512)
            tok = neighbourhood_attention(proj, na_rpb[j], seq_lens)
            q_slab0 = 3 * na_heads
        else:
            proj = norm_proj(x, norm_mix[i], s5_w_in[j].astype(BF16), F32, tm=512)
            lam_re, lam_im, bb_re, bb_im = s5_discretise(s5_a_re[j], s5_a_im[j], s5_log_dt[j], s5_b_re[j], s5_b_im[j])
            b_op, c_op, lam, d_op = _s5_operands(lam_re, lam_im, bb_re, bb_im, s5_c_re[j], s5_c_im[j], s5_d[j])
            g = s5_mixer(proj, b_op, c_op, lam, d_op, seq_lens)
            tok = glu(g, s5_w_glu[j].astype(BF16), s5_b_glu[j])
            q_slab0 = b_op.shape[1]
        mo = memory_attention(proj, q_slab0, kv, blk_seq, n_mem, tq=tq_mem)
        x = out_proj(x, tok, mo, w_out[i].astype(BF16))
        meta, ys, pos0, pos1 = moe_experts(x, norm_ffn[i], moe_w_rg[i], moe_b_rg[i], moe_w_re[i], moe_b_re[i],
                                           w_gate_all, w_up_all, w_down_all, i)
    outs, base = [], 0
    for g in groups:
        count = g.shape[0] * g.shape[1]
        y = moe_combine(x, meta, ys, pos0, pos1, norm_final, base=base, count=count, final_norm=True)
        outs.append(y.reshape(g.shape))
        base += count
    return tuple(outs)
```

```python
import functools
import math

import numpy as np
import jax
import jax.numpy as jnp
from jax import lax
from jax.experimental import pallas as pl
from jax.experimental.pallas import tpu as pltpu

F32 = jnp.float32
BF16 = jnp.bfloat16
U32 = jnp.uint32

LANE = 128
SUBLANE = 8
VMEM_LIMIT_BYTES = 56 * 1024 * 1024

RMS_EPS = 1e-6
NEG_INF = -1e30
GRID_W = 64
ROW_WIN = 8
COL_WIN = 16
NA_HEAD_DIM = 128
MEM_HEADS = 4
S5_GROUP = 16
S5_STATE = 64
N_EXPERT_GROUPS = 4
EXPERTS_PER_GROUP = 4
N_EXPERTS = N_EXPERT_GROUPS * EXPERTS_PER_GROUP

NA_Q_ROWS = 8
NA_K_ROWS = 2 * ROW_WIN
NA_HEADS_PER_STEP = 4
SUPER = 8192
S5_SEGMENTS = SUBLANE
S5_TBLK = 64
MOE_TILE = 256


def _cparams(*sem):
    return pltpu.CompilerParams(dimension_semantics=sem, vmem_limit_bytes=VMEM_LIMIT_BYTES)


def _rms_norm(x, gain):
    ms = jnp.mean(x * x, axis=-1, keepdims=True)
    return x * lax.rsqrt(ms + RMS_EPS) * gain


def _sigmoid(x):
    return 1.0 / (1.0 + jnp.exp(-x))


def _split_hi_lo(x):
    hi = lax.bitcast_convert_type(lax.bitcast_convert_type(x, U32) & jnp.uint32(0xFFFF0000), F32)
    return hi.astype(BF16), (x - hi).astype(BF16)


def _pack_bf16_pairs(x):
    k = x.shape[1] // 2
    bits = lax.bitcast_convert_type(x, U32)
    return (bits[:, k:] & jnp.uint32(0xFFFF0000)) | (bits[:, :k] >> 16)


def _unpack_bf16_pairs(w):
    lo = lax.bitcast_convert_type(w << 16, F32)
    hi = lax.bitcast_convert_type(w & jnp.uint32(0xFFFF0000), F32)
    return lo, hi


def _norm_proj_kernel(x_ref, g_ref, w_ref, o_ref, h_scr):
    @pl.when(pl.program_id(1) == 0)
    def _():
        h_scr[...] = _rms_norm(x_ref[...], g_ref[...]).astype(BF16)

    acc = jnp.dot(h_scr[...], w_ref[...], preferred_element_type=F32)
    for c in range(o_ref.shape[0]):
        o_ref[c] = acc[:, c * LANE:(c + 1) * LANE].astype(o_ref.dtype)


def norm_proj(x, gain, w, out_dtype, tm, tn=1024):
    m, d = x.shape
    n_out = w.shape[1]
    assert m % tm == 0 and n_out % tn == 0 and tn % LANE == 0
    return pl.pallas_call(
        _norm_proj_kernel,
        grid=(m // tm, n_out // tn),
        in_specs=[pl.BlockSpec((tm, d), lambda i, j: (i, 0)),
                  pl.BlockSpec((1, d), lambda i, j: (0, 0)),
                  pl.BlockSpec((d, tn), lambda i, j: (0, j))],
        out_specs=pl.BlockSpec((tn // LANE, tm, LANE), lambda i, j: (j, i, 0)),
        out_shape=jax.ShapeDtypeStruct((n_out // LANE, m, LANE), out_dtype),
        scratch_shapes=[pltpu.VMEM((tm, d), BF16)],
        compiler_params=_cparams("parallel", "arbitrary"),
        name="norm_proj",
    )(x, gain.reshape(1, d), w)


def _na_kernel(ks_ref, ty_ref, pi_ref, q_ref, k_ref, v_ref, b_ref, o_ref, *, nqb):
    step = pl.program_id(1) * nqb + pl.program_id(2)
    start = pl.multiple_of(ks_ref[step] * GRID_W, GRID_W)
    nk = NA_K_ROWS * GRID_W
    npair = NA_K_ROWS // 2
    ty = ty_ref[step]
    for h in range(q_ref.shape[0]):
        q = q_ref[h]
        k = k_ref[h, pl.ds(start, nk), :]
        v = v_ref[h, pl.ds(start, nk), :]
        s = lax.dot_general(q, k, (((1,), (1,)), ((), ())), preferred_element_type=F32)
        bias = jnp.concatenate(
            [jnp.concatenate([b_ref[h, pi_ref[(ty * NA_Q_ROWS + i) * npair + jp]] for jp in range(npair)], axis=1)
             for i in range(NA_Q_ROWS)], axis=0)
        s = s + bias
        m = jnp.max(s, axis=-1, keepdims=True)
        p = jnp.exp(s - m).astype(BF16)
        v_ones = jnp.concatenate([v, jnp.ones(v.shape, v.dtype)], axis=1)
        ol = jnp.dot(p, v_ones, preferred_element_type=F32)
        o = ol[:, :LANE] / ol[:, LANE:]
        o_ref[:, h * LANE:(h + 1) * LANE] = o.astype(o_ref.dtype)


def _na_tables(seq_lens):
    ks, ty = [], []
    tok = 0
    per_super = SUPER // GRID_W
    for length in seq_lens:
        rows = length // GRID_W
        assert length % (NA_Q_ROWS * GRID_W) == 0 and rows >= 3 * NA_Q_ROWS
        assert tok // SUPER == (tok + length - 1) // SUPER
        row0 = (tok % SUPER) // GRID_W
        for qb in range(rows // NA_Q_ROWS):
            ks.append(row0 + int(np.clip(qb * NA_Q_ROWS - ROW_WIN // 2, 0, rows - NA_K_ROWS)))
            ty.append(0 if qb == 0 else (2 if qb == rows // NA_Q_ROWS - 1 else 1))
        tok += length
    assert tok % SUPER == 0 and len(ks) == (tok // SUPER) * (per_super // NA_Q_ROWS)
    return np.asarray(ks, np.int32), np.asarray(ty, np.int32)


def _na_bias(rpb):
    heads = rpb.shape[0]
    cols = np.arange(GRID_W)
    c_start = np.clip(cols - COL_WIN // 2, 0, GRID_W - COL_WIN)
    in_win = (cols[None, :] >= c_start[:, None]) & (cols[None, :] < c_start[:, None] + COL_WIN)
    dc = np.clip(cols[None, :] - cols[:, None], -(COL_WIN - 1), COL_WIN - 1) + COL_WIN - 1
    col_bias = jnp.where(jnp.asarray(in_win), rpb.astype(F32)[:, :, dc], NEG_INF)
    n_dr = 2 * ROW_WIN - 1
    masked = jnp.full((heads, n_dr, GRID_W, GRID_W), NEG_INF, F32)
    left = jnp.concatenate([col_bias[:, :n_dr - 1], masked, col_bias, masked[:, :1]], axis=1)
    right = jnp.concatenate([col_bias[:, 1:], col_bias, masked, masked[:, :1]], axis=1)
    pairs = jnp.concatenate([left, right], axis=-1)
    both, right_only, left_only, none = 0, n_dr - 1, 2 * n_dr - 1, 3 * n_dr - 1
    half = ROW_WIN // 2
    index = []
    for ty in range(3):
        for i in range(NA_Q_ROWS):
            q_rel = (i, i + half, i + ROW_WIN)[ty]
            w_rel = (max(i - half, 0), i, min(i + half, ROW_WIN))[ty]
            for jp in range(NA_K_ROWS // 2):
                j0, j1 = 2 * jp, 2 * jp + 1
                v0, v1 = w_rel <= j0 < w_rel + ROW_WIN, w_rel <= j1 < w_rel + ROW_WIN
                d0, d1 = j0 - q_rel + ROW_WIN - 1, j1 - q_rel + ROW_WIN - 1
                assert (not v0 or 0 <= d0 < n_dr) and (not v1 or 0 <= d1 < n_dr)
                index.append(both + d0 if v0 and v1 else right_only + d1 if v1 else left_only + d0 if v0 else none)
    return pairs, np.asarray(index, np.int32)


def neighbourhood_attention(qkv, rpb, seq_lens):
    heads = rpb.shape[0]
    n = qkv.shape[1]
    nsup = n // SUPER
    tq = NA_Q_ROWS * GRID_W
    nqb = SUPER // tq
    ks_tab, ty_tab = _na_tables(seq_lens)
    pairs, pair_index = _na_bias(rpb)
    hp = NA_HEADS_PER_STEP
    assert heads % hp == 0
    ng = heads // hp
    grid_spec = pltpu.PrefetchScalarGridSpec(
        num_scalar_prefetch=3,
        grid=(ng, nsup, nqb),
        in_specs=[
            pl.BlockSpec((hp, tq, LANE), lambda h, s, b, ks, ty, pi: (h, s * nqb + b, 0)),
            pl.BlockSpec((hp, SUPER, LANE), lambda h, s, b, ks, ty, pi: (ng + h, s, 0)),
            pl.BlockSpec((hp, SUPER, LANE), lambda h, s, b, ks, ty, pi: (2 * ng + h, s, 0)),
            pl.BlockSpec((hp,) + pairs.shape[1:], lambda h, s, b, ks, ty, pi: (h, 0, 0, 0)),
        ],
        out_specs=pl.BlockSpec((tq, hp * LANE), lambda h, s, b, ks, ty, pi: (s * nqb + b, h)),
    )
    return pl.pallas_call(
        functools.partial(_na_kernel, nqb=nqb),
        grid_spec=grid_spec,
        out_shape=jax.ShapeDtypeStruct((n, heads * LANE), BF16),
        compiler_params=_cparams("parallel", "arbitrary", "arbitrary"),
        name="neighbourhood_attention",
    )(jnp.asarray(ks_tab), jnp.asarray(ty_tab), jnp.asarray(pair_index), qkv, qkv, qkv, pairs)


def _mem_attn_kernel(seq_ref, q_ref, kv_ref, o_ref, *, scale):
    del seq_ref
    hd = 2 * LANE
    for h in range(MEM_HEADS):
        q = jnp.concatenate([q_ref[2 * h], q_ref[2 * h + 1]], axis=-1).astype(BF16)
        k = jnp.concatenate([kv_ref[2 * h], kv_ref[2 * h + 1]], axis=-1)
        v = jnp.concatenate([kv_ref[2 * MEM_HEADS + 2 * h], kv_ref[2 * MEM_HEADS + 2 * h + 1]], axis=-1)
        s = lax.dot_general(q, k, (((1,), (1,)), ((), ())), preferred_element_type=F32) * scale
        m = jnp.max(s, axis=-1, keepdims=True)
        p = jnp.exp(s - m)
        l = jnp.sum(p, axis=-1, keepdims=True)
        o = jnp.dot(p.astype(BF16), v, preferred_element_type=F32) / l
        o_ref[:, h * hd:(h + 1) * hd] = o.astype(o_ref.dtype)


def memory_attention(proj, q_slab0, kv, blk_seq, n_mem, tq=512):
    n = proj.shape[1]
    nq = 2 * MEM_HEADS
    assert q_slab0 % nq == 0
    grid_spec = pltpu.PrefetchScalarGridSpec(
        num_scalar_prefetch=1,
        grid=(n // tq,),
        in_specs=[pl.BlockSpec((nq, tq, LANE), lambda i, sq: (q_slab0 // nq, i, 0)),
                  pl.BlockSpec((2 * nq, n_mem, LANE), lambda i, sq: (0, sq[i], 0))],
        out_specs=pl.BlockSpec((tq, nq * LANE), lambda i, sq: (i, 0)),
    )
    return pl.pallas_call(
        functools.partial(_mem_attn_kernel, scale=(2 * LANE) ** -0.5),
        grid_spec=grid_spec,
        out_shape=jax.ShapeDtypeStruct((n, nq * LANE), BF16),
        compiler_params=_cparams("parallel"),
        name="memory_attention",
    )(blk_seq, proj, kv)


def _s5_param_kernel(are_ref, aim_ref, ldt_ref, bre_ref, bim_ref, lre_ref, lim_ref, bbre_ref, bbim_ref):
    a_re = are_ref[...]
    a_im = aim_ref[...]
    dt = jnp.exp(ldt_ref[...])
    mag = jnp.exp(a_re * dt)
    ang = a_im * dt
    lam_re = mag * jnp.cos(ang)
    lam_im = mag * jnp.sin(ang)
    den = a_re * a_re + a_im * a_im
    n_re = lam_re - 1.0
    coef_re = (n_re * a_re + lam_im * a_im) / den
    coef_im = (lam_im * a_re - n_re * a_im) / den
    lre_ref[...] = lam_re
    lim_ref[...] = lam_im
    for c in range(bre_ref.shape[0]):
        b_re = bre_ref[c]
        b_im = bim_ref[c]
        bbre_ref[c] = coef_re * b_re - coef_im * b_im
        bbim_ref[c] = coef_re * b_im + coef_im * b_re


def s5_discretise(a_re, a_im, log_dt, b_re, b_im):
    two, g, p = a_re.shape
    c = b_re.shape[-1]
    rows = two * g
    ldt = jnp.broadcast_to(log_dt.reshape(rows, 1), (rows, p))
    b_re_t = b_re.reshape(rows, p, c).transpose(2, 0, 1)
    b_im_t = b_im.reshape(rows, p, c).transpose(2, 0, 1)
    return pl.pallas_call(
        _s5_param_kernel,
        out_shape=(jax.ShapeDtypeStruct((rows, p), F32), jax.ShapeDtypeStruct((rows, p), F32),
                   jax.ShapeDtypeStruct((c, rows, p), F32), jax.ShapeDtypeStruct((c, rows, p), F32)),
        name="s5_discretise",
    )(a_re.reshape(rows, p), a_im.reshape(rows, p), ldt, b_re_t, b_im_t)


def _s5_operands(lam_re, lam_im, bb_re, bb_im, c_re, c_im, d):
    c, rows, p = bb_re.shape
    g = rows // 2
    gs = LANE // c
    ns = g // gs
    same_group = jnp.eye(gs, dtype=jnp.bool_)

    def bmat(bb):
        x = bb.reshape(c, 2, ns, gs, p).transpose(1, 2, 3, 0, 4)
        blocks = jnp.where(same_group[None, None, :, None, :, None], x[:, :, :, :, None, :], 0.0)
        return blocks.reshape(2, ns, gs * c, gs * p)

    def cmat(cc):
        x = cc.astype(F32).reshape(2, ns, gs, c, p).transpose(0, 1, 2, 4, 3)
        blocks = jnp.where(same_group[None, None, :, None, :, None], x[:, :, :, :, None, :], 0.0)
        return blocks.reshape(2, ns, gs * p, gs * c)

    b_op = jnp.concatenate([bmat(bb_re), bmat(bb_im)], axis=-1).astype(BF16)
    c_op = jnp.concatenate([cmat(c_re), -cmat(c_im)], axis=-2).astype(BF16)
    lam = jnp.concatenate([lam_re.reshape(2, ns, gs * p), lam_im.reshape(2, ns, gs * p)], axis=-1)
    lam = jnp.broadcast_to(lam[:, :, None, :], (2, ns, SUBLANE, 2 * gs * p))
    d_op = d.astype(F32).reshape(ns, 1, gs * c)
    return b_op, c_op, lam, d_op


def _s5_kernel(keep_ref, u_hbm, bm_ref, cm_ref, lam_ref, d_ref, o_ref,
               u_scr, u_sem, up_scr, y_scr, y2_scr, bu0_scr, bu1_scr, xs0_scr, xs1_scr, e_scr, i_scr, *, lseg, tblk):
    sup = pl.program_id(0)
    nblk = lseg // tblk
    rb = SUBLANE * tblk
    ns = bu0_scr.shape[1] // 2
    nchunk = ns // LANE
    d_row = d_ref[0]
    pitch = u_scr.shape[1] // S5_SEGMENTS
    n_slab = pl.num_programs(1)
    lin = sup * n_slab + pl.program_id(1)
    slot = lin % 2

    def segment_copies(step, slot):
        b = step // n_slab
        s = step % n_slab
        return [pltpu.make_async_copy(u_hbm.at[s, pl.ds(b * (S5_SEGMENTS * lseg) + r * lseg, lseg), :],
                                      u_scr.at[slot, pl.ds(r * pitch, lseg), :], u_sem.at[slot])
                for r in range(S5_SEGMENTS)]

    @pl.when(lin == 0)
    def _():
        for cp in segment_copies(lin, slot):
            cp.start()

    for cp in segment_copies(lin, slot):
        cp.wait()

    @pl.when(lin + 1 < pl.num_programs(0) * n_slab)
    def _():
        for cp in segment_copies(lin + 1, 1 - slot):
            cp.start()

    u2 = u_scr.at[slot]

    def permute_in(t2, carry):
        t = t2 * 2
        a = u2[pl.ds(t, SUBLANE, stride=pitch), :]
        b = u2[pl.ds(t + 1, SUBLANE, stride=pitch), :]
        ab = jnp.concatenate([a, b], axis=0)
        rows = pl.ds(pl.multiple_of(t * SUBLANE, 2 * SUBLANE), 2 * SUBLANE)
        up_scr[rows, :] = ab.astype(BF16)
        y_scr[rows, :] = ab * d_row
        return carry
    lax.fori_loop(0, lseg // 2, permute_in, 0, unroll=4)
    xs1_scr[...] = jnp.zeros(xs1_scr.shape, xs1_scr.dtype)

    for direction in range(2):
        rev = direction == 1
        lam_r = [lam_ref[direction, 0, :, c * LANE:(c + 1) * LANE] for c in range(nchunk)]
        lam_i = [lam_ref[direction, 0, :, ns + c * LANE:ns + (c + 1) * LANE] for c in range(nchunk)]
        steps = list(range(tblk))[::-1] if rev else list(range(tblk))

        def block_rows(k, rev=rev):
            k = jnp.clip(k, 0, nblk - 1)
            blk = (nblk - 1 - k) if rev else k
            return pl.ds(pl.multiple_of(blk * rb, rb), rb)

        def bu_dot(k, direction=direction, block_rows=block_rows):
            return jnp.dot(up_scr[block_rows(k), :], bm_ref[direction, 0], preferred_element_type=F32)

        def scan_block(bu, x, xs, lam_r=lam_r, lam_i=lam_i, steps=steps):
            xr, xi = list(x[:nchunk]), list(x[nchunk:])
            held = None
            for n, t in enumerate(steps):
                rows = slice(t * SUBLANE, (t + 1) * SUBLANE)
                for c in range(nchunk):
                    b_r = bu[rows, c * LANE:(c + 1) * LANE]
                    b_i = bu[rows, ns + c * LANE:ns + (c + 1) * LANE]
                    r = lam_r[c] * xr[c] - lam_i[c] * xi[c] + b_r
                    i = lam_r[c] * xi[c] + lam_i[c] * xr[c] + b_i
                    xr[c], xi[c] = r, i
                if xs is None:
                    continue
                if n % 2 == 0:
                    held = (t, list(xr), list(xi))
                    continue
                lo, hi = (held, (t, xr, xi)) if held[0] < t else ((t, xr, xi), held)
                pair = slice(lo[0] * SUBLANE, (lo[0] + 2) * SUBLANE)
                for c in range(nchunk):
                    xs[pair, c * LANE:(c + 1) * LANE] = jnp.concatenate([lo[1][c], hi[1][c]], axis=0).astype(BF16)
                    xs[pair, ns + c * LANE:ns + (c + 1) * LANE] = (
                        jnp.concatenate([lo[2][c], hi[2][c]], axis=0).astype(BF16))
            return tuple(xr + xi)

        def pass_a(k2, x, bu_dot=bu_dot, scan_block=scan_block):
            bu1_scr[...] = bu_dot(2 * k2 + 1)
            x = scan_block(bu0_scr, x, None)
            bu0_scr[...] = bu_dot(2 * k2 + 2)
            return scan_block(bu1_scr, x, None)
        bu0_scr[...] = bu_dot(0)
        zero = tuple(jnp.zeros((SUBLANE, LANE), F32) for _ in range(2 * nchunk))
        ends = lax.fori_loop(0, nblk // 2, pass_a, zero)
        for c in range(2 * nchunk):
            e_scr[:, c * LANE:(c + 1) * LANE] = ends[c]

        p_r = lam_ref[direction, 0, 0:1, 0:ns]
        p_i = lam_ref[direction, 0, 0:1, ns:2 * ns]
        for _ in range(int(math.log2(lseg))):
            p_r, p_i = p_r * p_r - p_i * p_i, 2.0 * p_r * p_i
        order = list(range(S5_SEGMENTS))[::-1] if rev else list(range(S5_SEGMENTS))
        s_r = jnp.zeros((1, ns), F32)
        s_i = jnp.zeros((1, ns), F32)
        i_scr[order[0]:order[0] + 1, :] = jnp.zeros((1, 2 * ns), F32)
        for prev, cur in zip(order[:-1], order[1:]):
            e_r = e_scr[prev:prev + 1, 0:ns]
            e_i = e_scr[prev:prev + 1, ns:2 * ns]
            keep = keep_ref[(sup * 2 + direction) * S5_SEGMENTS + cur].astype(F32)
            n_r = (p_r * s_r - p_i * s_i + e_r) * keep
            n_i = (p_r * s_i + p_i * s_r + e_i) * keep
            i_scr[cur:cur + 1, 0:ns] = n_r
            i_scr[cur:cur + 1, ns:2 * ns] = n_i
            s_r, s_i = n_r, n_i

        def project(k, xs, scale, direction=direction, block_rows=block_rows):
            rows = block_rows(k)
            yb = jnp.dot(xs[...], cm_ref[direction, 0], preferred_element_type=F32)
            y_scr[rows, :] = y_scr[rows, :] + yb * scale

        def pass_b(k2, x, bu_dot=bu_dot, scan_block=scan_block, project=project):
            bu1_scr[...] = bu_dot(2 * k2 + 1)
            x = scan_block(bu0_scr, x, xs0_scr)
            project(2 * k2 - 1, xs1_scr, jnp.where(k2 > 0, 1.0, 0.0))
            bu0_scr[...] = bu_dot(2 * k2 + 2)
            x = scan_block(bu1_scr, x, xs1_scr)
            project(2 * k2, xs0_scr, 1.0)
            return x
        bu0_scr[...] = bu_dot(0)
        start = tuple(i_scr[:, c * LANE:(c + 1) * LANE] for c in range(2 * nchunk))
        lax.fori_loop(0, nblk // 2, pass_b, start)
        project(nblk - 1, xs1_scr, 1.0)

    def permute_out(t, carry):
        y = y_scr[pl.ds(pl.multiple_of(t * SUBLANE, SUBLANE), SUBLANE), :]
        gelu = 0.5 * y * (1.0 + jnp.tanh(math.sqrt(2.0 / math.pi) * (y + 0.044715 * (y * y * y))))
        y2_scr[pl.ds(t, SUBLANE, stride=pitch), :] = gelu
        return carry
    lax.fori_loop(0, lseg, permute_out, 0, unroll=8)
    for r in range(S5_SEGMENTS):
        o_ref[r * lseg:(r + 1) * lseg, :] = y2_scr[r * pitch:r * pitch + lseg, :].astype(o_ref.dtype)


def _s5_keep_table(seq_lens, lseg):
    starts, ends = set(), set()
    tok = 0
    for length in seq_lens:
        assert length % lseg == 0
        starts.add(tok // lseg)
        tok += length
        ends.add(tok // lseg - 1)
    nseg = tok // lseg
    assert nseg % S5_SEGMENTS == 0
    keep = np.ones((nseg // S5_SEGMENTS, 2, S5_SEGMENTS), np.int32)
    for s in range(nseg):
        if s in starts:
            keep[s // S5_SEGMENTS, 0, s % S5_SEGMENTS] = 0
        if s in ends:
            keep[s // S5_SEGMENTS, 1, s % S5_SEGMENTS] = 0
    return keep.reshape(-1)


def s5_mixer(proj, b_op, c_op, lam, d_op, seq_lens):
    n = proj.shape[1]
    ns = b_op.shape[1]
    nstate2 = b_op.shape[-1]
    nsup = n // SUPER
    lseg = SUPER // S5_SEGMENTS
    tblk = min(S5_TBLK, lseg)
    assert lseg & (lseg - 1) == 0 and lseg % (2 * tblk) == 0 and tblk % 2 == 0
    keep = _s5_keep_table(seq_lens, lseg)
    rows = SUBLANE * tblk
    pitch = lseg + SUBLANE
    grid_spec = pltpu.PrefetchScalarGridSpec(
        num_scalar_prefetch=1,
        grid=(nsup, ns),
        in_specs=[
            pl.BlockSpec(memory_space=pl.ANY),
            pl.BlockSpec((2, 1, LANE, nstate2), lambda b, s, kp: (0, s, 0, 0)),
            pl.BlockSpec((2, 1, nstate2, LANE), lambda b, s, kp: (0, s, 0, 0)),
            pl.BlockSpec((2, 1, SUBLANE, nstate2), lambda b, s, kp: (0, s, 0, 0)),
            pl.BlockSpec((1, 1, LANE), lambda b, s, kp: (s, 0, 0)),
        ],
        out_specs=pl.BlockSpec((SUPER, LANE), lambda b, s, kp: (b, s)),
        scratch_shapes=[
            pltpu.VMEM((2, S5_SEGMENTS * pitch, LANE), F32),
            pltpu.SemaphoreType.DMA((2,)),
            pltpu.VMEM((SUPER, LANE), BF16),
            pltpu.VMEM((SUPER, LANE), F32),
            pltpu.VMEM((S5_SEGMENTS * pitch, LANE), F32),
            pltpu.VMEM((rows, nstate2), F32),
            pltpu.VMEM((rows, nstate2), F32),
            pltpu.VMEM((rows, nstate2), BF16),
            pltpu.VMEM((rows, nstate2), BF16),
            pltpu.VMEM((SUBLANE, nstate2), F32),
            pltpu.VMEM((SUBLANE, nstate2), F32),
        ],
    )
    return pl.pallas_call(
        functools.partial(_s5_kernel, lseg=lseg, tblk=tblk),
        grid_spec=grid_spec,
        out_shape=jax.ShapeDtypeStruct((n, ns * LANE), BF16),
        compiler_params=_cparams("arbitrary", "arbitrary"),
        name="s5_mixer",
    )(jnp.asarray(keep), proj, b_op, c_op, lam, d_op)


def _glu_kernel(g_ref, w_ref, b_ref, o_ref):
    tn = o_ref.shape[1]
    col = pl.multiple_of(pl.program_id(1) * tn, tn)
    acc = jnp.dot(g_ref[...], w_ref[...], preferred_element_type=F32) + b_ref[...]
    g = g_ref[:, pl.ds(col, tn)].astype(F32)
    o_ref[...] = (g * _sigmoid(acc)).astype(o_ref.dtype)


def glu(g, w, b, tm=1024, tn=1024):
    m, k = g.shape
    assert m % tm == 0 and k % tn == 0
    return pl.pallas_call(
        _glu_kernel,
        grid=(m // tm, k // tn),
        in_specs=[pl.BlockSpec((tm, k), lambda i, j: (i, 0)),
                  pl.BlockSpec((k, tn), lambda i, j: (0, j)),
                  pl.BlockSpec((1, tn), lambda i, j: (0, j))],
        out_specs=pl.BlockSpec((tm, tn), lambda i, j: (i, j)),
        out_shape=jax.ShapeDtypeStruct((m, k), BF16),
        compiler_params=_cparams("parallel", "arbitrary"),
        name="glu",
    )(g, w, b.reshape(1, k).astype(F32))


def _out_proj_kernel(x_ref, tok_ref, mo_ref, w1_ref, w2_ref, o_ref):
    acc = jnp.dot(tok_ref[...], w1_ref[...], preferred_element_type=F32)
    acc = acc + jnp.dot(mo_ref[...], w2_ref[...], preferred_element_type=F32)
    o_ref[...] = x_ref[...] + acc


def out_proj(x, tok, mo, w, tm=1024, tn=512):
    m, d = x.shape
    k1, k2 = tok.shape[1], mo.shape[1]
    assert m % tm == 0 and d % tn == 0 and w.shape[0] == k1 + k2 and k1 % k2 == 0
    return pl.pallas_call(
        _out_proj_kernel,
        grid=(m // tm, d // tn),
        in_specs=[pl.BlockSpec((tm, tn), lambda i, j: (i, j)),
                  pl.BlockSpec((tm, k1), lambda i, j: (i, 0)),
                  pl.BlockSpec((tm, k2), lambda i, j: (i, 0)),
                  pl.BlockSpec((k1, tn), lambda i, j: (0, j)),
                  pl.BlockSpec((k2, tn), lambda i, j: (k1 // k2, j))],
        out_specs=pl.BlockSpec((tm, tn), lambda i, j: (i, j)),
        out_shape=jax.ShapeDtypeStruct((m, d), F32),
        compiler_params=_cparams("parallel", "arbitrary"),
        name="out_proj",
    )(x, tok, mo, w, w)


def _router_kernel(x_ref, g_ref, whi_ref, wlo_ref, b_ref, h_ref, meta_ref):
    t = _rms_norm(x_ref[...], g_ref[...])
    h_ref[...] = _pack_bf16_pairs(t)
    t_hi, t_lo = _split_hi_lo(t)
    logits = (jnp.dot(t_hi, whi_ref[...], preferred_element_type=F32)
              + jnp.dot(t_lo, whi_ref[...], preferred_element_type=F32)
              + jnp.dot(t_hi, wlo_ref[...], preferred_element_type=F32)) + b_ref[...]
    lane = lax.broadcasted_iota(jnp.int32, logits.shape, 1)
    big = jnp.int32(LANE)
    is_g = lane < N_EXPERT_GROUPS
    gl = jnp.where(is_g, logits, -jnp.inf)
    gmax = jnp.max(gl, axis=-1, keepdims=True)
    gsum = jnp.sum(jnp.where(is_g, jnp.exp(gl - gmax), 0.0), axis=-1, keepdims=True)
    g_val = 1.0 / gsum
    g_idx = jnp.min(jnp.where(gl == gmax, lane, big), axis=-1, keepdims=True)
    e_lane = lane - N_EXPERT_GROUPS
    in_grp = (e_lane >= 0) & (e_lane < N_EXPERTS) & ((e_lane >> 2) == g_idx)
    el = jnp.where(in_grp, logits, -jnp.inf)
    v1 = jnp.max(el, axis=-1, keepdims=True)
    i1 = jnp.min(jnp.where(el == v1, lane, big), axis=-1, keepdims=True)
    el2 = jnp.where(lane == i1, -jnp.inf, el)
    v2 = jnp.max(el2, axis=-1, keepdims=True)
    i2 = jnp.min(jnp.where(el2 == v2, lane, big), axis=-1, keepdims=True)
    z = jnp.exp(v2 - v1)
    w1 = g_val / (1.0 + z)
    w2 = g_val * z / (1.0 + z)
    e1 = (i1 - N_EXPERT_GROUPS).astype(F32)
    e2 = (i2 - N_EXPERT_GROUPS).astype(F32)
    meta = jnp.where(lane == 0, w1, jnp.where(lane == 1, w2, jnp.where(lane == 2, e1, jnp.where(lane == 3, e2, 0.0))))
    meta_ref[...] = meta


def router(x, gain, w_rg, b_rg, w_re, b_re, tm=512):
    m, d = x.shape
    assert EXPERTS_PER_GROUP == 4 and m % tm == 0
    pad = LANE - N_EXPERT_GROUPS - N_EXPERTS
    w = jnp.concatenate([w_rg, w_re, jnp.zeros((d, pad), F32)], axis=1).astype(F32)
    w_hi, w_lo = _split_hi_lo(w)
    b = jnp.concatenate([b_rg, b_re, jnp.zeros((pad,), F32)]).astype(F32).reshape(1, LANE)
    return pl.pallas_call(
        _router_kernel,
        grid=(m // tm,),
        in_specs=[pl.BlockSpec((tm, d), lambda i: (i, 0)),
                  pl.BlockSpec((1, d), lambda i: (0, 0)),
                  pl.BlockSpec((d, LANE), lambda i: (0, 0)),
                  pl.BlockSpec((d, LANE), lambda i: (0, 0)),
                  pl.BlockSpec((1, LANE), lambda i: (0, 0))],
        out_specs=(pl.BlockSpec((tm, d // 2), lambda i: (i, 0)),
                   pl.BlockSpec((tm, LANE), lambda i: (i, 0))),
        out_shape=(jax.ShapeDtypeStruct((m, d // 2), U32), jax.ShapeDtypeStruct((m, LANE), F32)),
        compiler_params=_cparams("parallel"),
        name="moe_router",
    )(x, gain.reshape(1, d), w_hi, w_lo, b)


def _routing_plan(meta, n_rows):
    n = meta.shape[0]
    e = meta[:, 2:4].astype(jnp.int32).T
    experts = jnp.arange(N_EXPERTS, dtype=jnp.int32)[:, None]
    onehot = ((e[0][None, :] == experts) | (e[1][None, :] == experts)).astype(jnp.int32)
    counts = onehot.sum(axis=1)
    rank = jnp.cumsum(onehot, axis=1) - onehot
    padded = ((counts + MOE_TILE - 1) // MOE_TILE) * MOE_TILE
    ends = jnp.cumsum(padded)
    offs = ends - padded
    pos = jnp.take_along_axis(offs[:, None] + rank, e, axis=0).astype(jnp.int32)
    tok = jnp.arange(n, dtype=jnp.int32)
    sorted_tok = jnp.zeros((n_rows,), jnp.int32).at[pos.reshape(-1)].set(jnp.concatenate([tok, tok]))
    tile_start = jnp.arange(n_rows // MOE_TILE, dtype=jnp.int32) * MOE_TILE
    tile_expert = jnp.minimum((ends[None, :] <= tile_start[:, None]).sum(axis=1), N_EXPERTS - 1).astype(jnp.int32)
    tile_valid = (tile_start < ends[-1]).astype(jnp.int32)
    return pos[0], pos[1], sorted_tok, tile_expert, tile_valid


def _row_gather_start(idx_ref, base, src_ref, dst_ref, sem, rows=None):
    for r in (range(dst_ref.shape[0]) if rows is None else rows):
        pltpu.make_async_copy(src_ref.at[pl.ds(idx_ref[base + r], 1), :], dst_ref.at[pl.ds(r, 1), :], sem).start()


def _row_gather_wait(src_ref, dst_ref, sem):
    pltpu.make_async_copy(src_ref.at[pl.ds(0, dst_ref.shape[0]), :], dst_ref, sem).wait()


def _moe_up_kernel(st_ref, te_ref, tv_ref, h_ref, wg_ref, wu_ref, o_ref, xbuf, sem):
    del te_ref
    t = pl.program_id(0)
    last = pl.num_programs(0) - 1
    slot = t % 2
    tile = xbuf.shape[1]

    @pl.when(t == 0)
    def _():
        _row_gather_start(st_ref, 0, h_ref, xbuf.at[0], sem.at[0])

    @pl.when(tv_ref[t] == 1)
    def _():
        _row_gather_wait(h_ref, xbuf.at[slot], sem.at[slot])
        nxt = jnp.minimum(t + 1, last)
        lo, hi = _unpack_bf16_pairs(xbuf[slot])
        lo = lo.astype(BF16)
        hi = hi.astype(BF16)
        half = lo.shape[1]

        def prefetch(part, parts=4):
            rows = range(part * tile // parts, (part + 1) * tile // parts)
            _row_gather_start(st_ref, nxt * tile, h_ref, xbuf.at[1 - slot], sem.at[1 - slot], rows)

        prefetch(0)
        g = jnp.dot(lo, wg_ref[0, :half], preferred_element_type=F32)
        prefetch(1)
        g = g + jnp.dot(hi, wg_ref[0, half:], preferred_element_type=F32)
        prefetch(2)
        u = jnp.dot(lo, wu_ref[0, :half], preferred_element_type=F32)
        prefetch(3)
        u = u + jnp.dot(hi, wu_ref[0, half:], preferred_element_type=F32)
        o_ref[...] = (g * _sigmoid(g) * u).astype(o_ref.dtype)

        @pl.when(t == last)
        def _():
            _row_gather_wait(h_ref, xbuf.at[1 - slot], sem.at[1 - slot])

    @pl.when(tv_ref[t] == 0)
    def _():
        @pl.when(tv_ref[jnp.maximum(t - 1, 0)] == 1)
        def _():
            _row_gather_wait(h_ref, xbuf.at[slot], sem.at[slot])
        o_ref[...] = jnp.zeros(o_ref.shape, o_ref.dtype)


def moe_up(h, w_gate, w_up, sorted_tok, tile_expert, tile_valid):
    half = h.shape[1]
    d = 2 * half
    f = w_gate.shape[-1]
    n_tiles = tile_expert.shape[0]
    grid_spec = pltpu.PrefetchScalarGridSpec(
        num_scalar_prefetch=3,
        grid=(n_tiles,),
        in_specs=[pl.BlockSpec(memory_space=pl.ANY),
                  pl.BlockSpec((1, d, f), lambda t, st, te, tv: (te[t], 0, 0)),
                  pl.BlockSpec((1, d, f), lambda t, st, te, tv: (te[t], 0, 0))],
        out_specs=pl.BlockSpec((MOE_TILE, f), lambda t, st, te, tv: (t, 0)),
        scratch_shapes=[pltpu.VMEM((2, MOE_TILE, half), U32), pltpu.SemaphoreType.DMA((2,))],
    )
    return pl.pallas_call(
        _moe_up_kernel,
        grid_spec=grid_spec,
        out_shape=jax.ShapeDtypeStruct((n_tiles * MOE_TILE, f), BF16),
        compiler_params=_cparams("arbitrary"),
        name="moe_up",
    )(sorted_tok, tile_expert, tile_valid, h, w_gate, w_up)


def _moe_down_kernel(te_ref, tv_ref, h_ref, wd_ref, o_ref):
    del te_ref
    t = pl.program_id(0)

    @pl.when(tv_ref[t] == 1)
    def _():
        y = jnp.dot(h_ref[...], wd_ref[0], preferred_element_type=F32)
        o_ref[...] = _pack_bf16_pairs(y)

    @pl.when(tv_ref[t] == 0)
    def _():
        o_ref[...] = jnp.zeros(o_ref.shape, o_ref.dtype)


def moe_down(hid, w_down, tile_expert, tile_valid):
    n_rows, f = hid.shape
    d = w_down.shape[-1]
    grid_spec = pltpu.PrefetchScalarGridSpec(
        num_scalar_prefetch=2,
        grid=(n_rows // MOE_TILE,),
        in_specs=[pl.BlockSpec((MOE_TILE, f), lambda t, te, tv: (t, 0)),
                  pl.BlockSpec((1, f, d), lambda t, te, tv: (te[t], 0, 0))],
        out_specs=pl.BlockSpec((MOE_TILE, d // 2), lambda t, te, tv: (t, 0)),
    )
    return pl.pallas_call(
        _moe_down_kernel,
        grid_spec=grid_spec,
        out_shape=jax.ShapeDtypeStruct((n_rows, d // 2), U32),
        compiler_params=_cparams("arbitrary"),
        name="moe_down",
    )(tile_expert, tile_valid, hid, w_down)


def _combine_kernel(p0_ref, p1_ref, x_ref, meta_ref, g_ref, ys_ref, o_ref, buf0, buf1, sem, *, base, final_norm):
    i = pl.program_id(0)
    last = pl.num_programs(0) - 1
    tc = buf0.shape[1]

    def start(tile, buf, sems):
        n0 = base + tile * tc
        _row_gather_start(p0_ref, n0, ys_ref, buf.at[0], sems.at[0])
        _row_gather_start(p1_ref, n0, ys_ref, buf.at[1], sems.at[1])

    def wait(buf, sems):
        _row_gather_wait(ys_ref, buf.at[0], sems.at[0])
        _row_gather_wait(ys_ref, buf.at[1], sems.at[1])

    @pl.when(i == 0)
    def _():
        start(0, buf0, sem.at[0])

    def run(cur, cur_sems, nxt, nxt_sems):
        wait(cur, cur_sems)
        lo0, hi0 = _unpack_bf16_pairs(cur[0])
        lo1, hi1 = _unpack_bf16_pairs(cur[1])
        start(jnp.minimum(i + 1, last), nxt, nxt_sems)
        meta = meta_ref[...]
        w0 = meta[:, 0:1]
        w1 = meta[:, 1:2]
        half = lo0.shape[1]
        y_lo = x_ref[:, :half] + w0 * lo0 + w1 * lo1
        y_hi = x_ref[:, half:] + w0 * hi0 + w1 * hi1
        if final_norm:
            ms = (jnp.sum(y_lo * y_lo, axis=-1, keepdims=True)
                  + jnp.sum(y_hi * y_hi, axis=-1, keepdims=True)) / (2 * half)
            inv = lax.rsqrt(ms + RMS_EPS)
            y_lo = y_lo * inv * g_ref[:, :half]
            y_hi = y_hi * inv * g_ref[:, half:]
        o_ref[:, :half] = y_lo
        o_ref[:, half:] = y_hi

        @pl.when(i == last)
        def _():
            wait(nxt, nxt_sems)

    @pl.when(i % 2 == 0)
    def _():
        run(buf0, sem.at[0], buf1, sem.at[1])

    @pl.when(i % 2 == 1)
    def _():
        run(buf1, sem.at[1], buf0, sem.at[0])


def moe_combine(x, meta, ys, pos0, pos1, gain, *, base, count, final_norm, tc=256):
    d = x.shape[1]
    assert base % tc == 0 and count % tc == 0
    blk0 = base // tc
    grid_spec = pltpu.PrefetchScalarGridSpec(
        num_scalar_prefetch=2,
        grid=(count // tc,),
        in_specs=[pl.BlockSpec((tc, d), lambda i, p0, p1: (blk0 + i, 0)),
                  pl.BlockSpec((tc, LANE), lambda i, p0, p1: (blk0 + i, 0)),
                  pl.BlockSpec((1, d), lambda i, p0, p1: (0, 0)),
                  pl.BlockSpec(memory_space=pl.ANY)],
        out_specs=pl.BlockSpec((tc, d), lambda i, p0, p1: (i, 0)),
        scratch_shapes=[pltpu.VMEM((2, tc, d // 2), U32), pltpu.VMEM((2, tc, d // 2), U32),
                        pltpu.SemaphoreType.DMA((2, 2))],
    )
    return pl.pallas_call(
        functools.partial(_combine_kernel, base=base, final_norm=final_norm),
        grid_spec=grid_spec,
        out_shape=jax.ShapeDtypeStruct((count, d), F32),
        compiler_params=_cparams("arbitrary"),
        name="moe_combine",
    )(pos0, pos1, x, meta, gain.reshape(1, d), ys)


def moe_experts(x, gain, w_rg, b_rg, w_re, b_re, w_gate, w_up, w_down, layer):
    n = x.shape[0]
    n_rows = 2 * n + N_EXPERTS * MOE_TILE
    assert (2 * n) % MOE_TILE == 0
    h, meta = router(x, gain, w_rg, b_rg, w_re, b_re)
    pos0, pos1, sorted_tok, tile_expert, tile_valid = _routing_plan(meta, n_rows)
    tile_expert = tile_expert + layer * N_EXPERTS
    hid = moe_up(h, w_gate, w_up, sorted_tok, tile_expert, tile_valid)
    ys = moe_down(hid, w_down, tile_expert, tile_valid)
    return meta, ys, pos0, pos1


def kernel(x_prompt, x_sample, mem_prompt, mem_sample, norm_mix, norm_mem, norm_ffn, norm_final, na_w_in, na_rpb, s5_w_in, s5_a_re, s5_a_im, s5_log_dt, s5_b_re, s5_b_im, s5_c_re, s5_c_im, s5_d, s5_w_glu, s5_b_glu, mem_w_kv, w_out, moe_w_rg, moe_b_rg, moe_w_re, moe_b_re, moe_w_gate, moe_w_up, moe_w_down):
    d = x_prompt.shape[-1]
    groups = [x_prompt, x_sample]
    mems = [mem_prompt, mem_sample]
    seq_lens = [g.shape[1] for g in groups for _ in range(g.shape[0])]
    n_mem = mem_prompt.shape[1]
    depth = norm_mix.shape[0]
    x = jnp.concatenate([g.reshape(-1, d) for g in groups], axis=0)
    mem = jnp.concatenate([m.reshape(-1, d) for m in mems], axis=0)
    n = x.shape[0]
    tq_mem = 512
    blk_seq = []
    for s, length in enumerate(seq_lens):
        assert length % tq_mem == 0
        blk_seq += [s] * (length // tq_mem)
    blk_seq = jnp.asarray(np.asarray(blk_seq, np.int32))
    na_heads = na_rpb.shape[1]
    w_gate_all = moe_w_gate.astype(BF16).reshape((-1,) + moe_w_gate.shape[2:])
    w_up_all = moe_w_up.astype(BF16).reshape((-1,) + moe_w_up.shape[2:])
    w_down_all = moe_w_down.astype(BF16).reshape((-1,) + moe_w_down.shape[2:])

    meta = ys = pos0 = pos1 = None
    for i in range(depth):
        j = i // 2
        if i > 0:
            x = moe_combine(x, meta, ys, pos0, pos1, norm_final, base=0, count=n, final_norm=False)
        kv = norm_proj(mem, norm_mem[i], mem_w_kv[i].astype(BF16), BF16, tm=n_mem)
        if i % 2 == 0:
            n_q = na_heads * NA_HEAD_DIM
            col_scale = jnp.concatenate([jnp.full((n_q,), NA_HEAD_DIM ** -0.5, F32),
                                         jnp.ones((na_w_in.shape[2] - n_q,), F32)])
            proj = norm_proj(x, norm_mix[i], (na_w_in[j] * col_scale[None, :]).astype(BF16), BF16, tm=512)
            tok = neighbourhood_attention(proj, na_rpb[j], seq_lens)
            q_slab0 = 3 * na_heads
        else:
            proj = norm_proj(x, norm_mix[i], s5_w_in[j].astype(BF16), F32, tm=512)
            lam_re, lam_im, bb_re, bb_im = s5_discretise(s5_a_re[j], s5_a_im[j], s5_log_dt[j], s5_b_re[j], s5_b_im[j])
            b_op, c_op, lam, d_op = _s5_operands(lam_re, lam_im, bb_re, bb_im, s5_c_re[j], s5_c_im[j], s5_d[j])
            g = s5_mixer(proj, b_op, c_op, lam, d_op, seq_lens)
            tok = glu(g, s5_w_glu[j].astype(BF16), s5_b_glu[j])
            q_slab0 = b_op.shape[1]
        mo = memory_attention(proj, q_slab0, kv, blk_seq, n_mem, tq=tq_mem)
        x = out_proj(x, tok, mo, w_out[i].astype(BF16))
        meta, ys, pos0, pos1 = moe_experts(x, norm_ffn[i], moe_w_rg[i], moe_b_rg[i], moe_w_re[i], moe_b_re[i],
                                           w_gate_all, w_up_all, w_down_all, i)
    outs, base = [], 0
    for g in groups:
        count = g.shape[0] * g.shape[1]
        y = moe_combine(x, meta, ys, pos0, pos1, norm_final, base=base, count=count, final_norm=True)
        outs.append(y.reshape(g.shape))
        base += count
    return tuple(outs)
```
